```python
import jax, jax.numpy as jnp
from jax import lax
import numpy as np

D_MODEL = 1024
BATCH = 2
SEQ = 8192
DEPTH = 1

N_MEM = 256
ROPE_THETA = 10000.0
EPS = 1e-6
MAX_POS_OFFSET = 4096
NEG = -1e30
FORCE = 1e9

GLA_HEADS = 4
GLA_DK = D_MODEL // 2 // GLA_HEADS
GLA_DV = D_MODEL // GLA_HEADS
GLA_RANK = 16
GLA_TAU = 16.0
GLA_CHUNK = 64

NSA_HEADS = 16
NSA_KV_HEADS = 4
NSA_GROUP = NSA_HEADS // NSA_KV_HEADS
NSA_HD = D_MODEL // NSA_HEADS
CMP_LEN = 32
CMP_STRIDE = 16
CMP_HIDDEN = 256
SLC_BLOCK = 64
SLC_TOPN = 16
WINDOW = 512
NSA_Q_BLOCK = 64
N_NSA_BRANCH = 3

MEM_HEADS = 4
MEM_HD = D_MODEL // MEM_HEADS

N_BRANCH = 3
BRANCH_WIDTH = D_MODEL

SPLITS = (
    GLA_HEADS * GLA_DK,
    GLA_HEADS * GLA_DK,
    GLA_HEADS * GLA_DV,
    GLA_RANK,
    GLA_HEADS * GLA_DV,
    NSA_HEADS * NSA_HD,
    6 * NSA_KV_HEADS * NSA_HD,
    NSA_HEADS * N_NSA_BRANCH,
    NSA_HEADS * NSA_HD,
    MEM_HEADS * MEM_HD,
    MEM_HEADS * MEM_HD,
    N_BRANCH * D_MODEL,
)
D_IN = sum(SPLITS)

kernel_name = "hybrid_gla_nsa_memory_gated_block"


def rms_norm(x, gain):
    xf = x.astype(jnp.float32)
    y = xf * lax.rsqrt(jnp.mean(xf * xf, axis=-1, keepdims=True) + EPS)
    return (y * gain.astype(jnp.float32)).astype(x.dtype)


def rope(x, positions):
    half = x.shape[-1] // 2
    inv = ROPE_THETA ** (-jnp.arange(half, dtype=jnp.float32) / half)
    ang = positions.astype(jnp.float32)[..., None] * inv
    cos = jnp.cos(ang)[:, :, None, :]
    sin = jnp.sin(ang)[:, :, None, :]
    xf = x.astype(jnp.float32)
    x1, x2 = xf[..., :half], xf[..., half:]
    return jnp.concatenate([x1 * cos - x2 * sin, x2 * cos + x1 * sin], axis=-1).astype(x.dtype)


def gla_chunked(q, k, v, log_a):
    B, S, H, DK = q.shape
    DV = v.shape[-1]
    C = GLA_CHUNK
    nC = S // C
    f32 = jnp.float32

    def to_chunks(t):
        return t.astype(f32).reshape(B, nC, C, H, t.shape[-1]).transpose(1, 0, 3, 2, 4)

    qc, kc, vc = to_chunks(q), to_chunks(k), to_chunks(v)
    gc = jnp.cumsum(to_chunks(log_a), axis=3)
    causal = jnp.tril(jnp.ones((C, C), dtype=bool))

    def step(state, inp):
        qi, ki, vi, gi = inp
        diff = gi[:, :, :, None, :] - gi[:, :, None, :, :]
        decay = jnp.where(causal[:, :, None], jnp.exp(jnp.minimum(diff, 0.0)), 0.0)
        attn = jnp.einsum('bhijk,bhjk->bhij', qi[:, :, :, None, :] * decay, ki)
        o = attn @ vi + jnp.einsum('bhik,bhkv->bhiv', qi * jnp.exp(gi), state)
        g_last = gi[:, :, -1:, :]
        state = jnp.exp(g_last[:, :, 0, :])[..., None] * state + jnp.einsum(
            'bhjk,bhjv->bhkv', ki * jnp.exp(g_last - gi), vi)
        return state, o

    state0 = jnp.zeros((B, H, DK, DV), f32)
    _, o = lax.scan(step, state0, (qc, kc, vc, gc))
    return o.transpose(1, 0, 3, 2, 4).reshape(B, S, H, DV).astype(q.dtype)


def nsa_compress(tok, pe, w1, w2):
    S = tok.shape[1]
    n_cmp = (S - CMP_LEN) // CMP_STRIDE + 1
    idx = jnp.arange(n_cmp)[:, None] * CMP_STRIDE + jnp.arange(CMP_LEN)[None, :]
    blocks = tok[:, idx] + pe[None, None, :, None, :]
    h = jax.nn.gelu(jnp.einsum('bnlhd,ldf->bnhf', blocks, w1))
    return jnp.einsum('bnhf,fd->bnhd', h, w2)


def nsa_attention(q, kc, vc, k_slc, v_slc, k_win, v_win, gates):
    B, S, H, HD = q.shape
    Hkv, G, QB = NSA_KV_HEADS, NSA_GROUP, NSA_Q_BLOCK
    nQB = S // QB
    n_cmp = kc.shape[1]
    n_slc = S // SLC_BLOCK
    topn = min(SLC_TOPN, n_slc)
    scale = HD ** -0.5
    f32 = jnp.float32

    cmp_end = jnp.arange(n_cmp) * CMP_STRIDE + CMP_LEN - 1
    cs = np.arange(n_cmp)[:, None] * CMP_STRIDE
    ss = np.arange(n_slc)[None, :] * SLC_BLOCK
    ov = np.clip(np.minimum(cs + CMP_LEN, ss + SLC_BLOCK) - np.maximum(cs, ss), 0, None) / CMP_LEN
    ov = jnp.asarray(ov, f32)

    kc32, vc32 = kc.astype(f32), vc.astype(f32)
    k_sb = k_slc.reshape(B, n_slc, SLC_BLOCK, Hkv, HD).transpose(0, 3, 1, 2, 4)
    v_sb = v_slc.reshape(B, n_slc, SLC_BLOCK, Hkv, HD).transpose(0, 3, 1, 2, 4)
    kw = jnp.pad(k_win, ((0, 0), (WINDOW, 0), (0, 0), (0, 0)))
    vw = jnp.pad(v_win, ((0, 0), (WINDOW, 0), (0, 0), (0, 0)))
    qb = q.reshape(B, nQB, QB, Hkv, G, HD).transpose(1, 0, 3, 4, 2, 5)
    gb = gates.reshape(B, nQB, QB, H, N_NSA_BRANCH).transpose(1, 0, 2, 3, 4)
    bi = jnp.arange(B)[:, None, None, None]
    hi = jnp.arange(Hkv)[None, :, None, None]
    blk_id = jnp.arange(n_slc)

    def block_fn(args):
        blk, qi, gi = args
        t = blk * QB + jnp.arange(QB)
        qf = qi.astype(f32) * scale
        s = jnp.einsum('bhgqd,bnhd->bhgqn', qf, kc32)
        valid = cmp_end[None, :] <= t[:, None]
        p_cmp = jax.nn.softmax(jnp.where(valid, s, NEG), axis=-1) * valid
        o_cmp = jnp.einsum('bhgqn,bnhd->bhgqd', p_cmp, vc32)
        imp = jnp.einsum('bhgqn,ns->bhqs', p_cmp, ov)
        cur = t // SLC_BLOCK
        forced = (blk_id[None, :] == 0) | (blk_id[None, :] == cur[:, None]) | (blk_id[None, :] == cur[:, None] - 1)
        causal_blk = blk_id[None, :] <= cur[:, None]
        imp = jnp.where(forced, FORCE, jnp.where(causal_blk, imp, NEG))
        _, idx = lax.top_k(imp, topn)
        ks = k_sb[bi, hi, idx].astype(f32)
        vs = v_sb[bi, hi, idx].astype(f32)
        s = jnp.einsum('bhgqd,bhqnld->bhgqnl', qf, ks)
        kpos = idx[..., None] * SLC_BLOCK + jnp.arange(SLC_BLOCK)
        valid = kpos <= t[None, None, :, None, None]
        s = jnp.where(valid[:, :, None], s, NEG)
        p = jax.nn.softmax(s.reshape(s.shape[:4] + (-1,)), axis=-1).reshape(s.shape)
        o_slc = jnp.einsum('bhgqnl,bhqnld->bhgqd', p, vs)
        kwi = lax.dynamic_slice_in_dim(kw, blk * QB, WINDOW + QB, axis=1).astype(f32)
        vwi = lax.dynamic_slice_in_dim(vw, blk * QB, WINDOW + QB, axis=1).astype(f32)
        kp = blk * QB - WINDOW + jnp.arange(WINDOW + QB)
        valid = (kp[None, :] <= t[:, None]) & (kp[None, :] > t[:, None] - WINDOW) & (kp[None, :] >= 0)
        s = jnp.einsum('bhgqd,bkhd->bhgqk', qf, kwi)
        p = jax.nn.softmax(jnp.where(valid, s, NEG), axis=-1)
        o_win = jnp.einsum('bhgqk,bkhd->bhgqd', p, vwi)
        g = gi.astype(f32).reshape(B, QB, Hkv, G, N_NSA_BRANCH).transpose(0, 2, 3, 1, 4)
        o = g[..., 0:1] * o_cmp + g[..., 1:2] * o_slc + g[..., 2:3] * o_win
        return o.transpose(0, 3, 1, 2, 4).reshape(B, QB, H, HD).astype(q.dtype)

    out = lax.map(block_fn, (jnp.arange(nQB), qb, gb))
    return out.transpose(1, 0, 2, 3, 4).reshape(B, S, H * HD)


def memory_attention(q, k, v):
    s = jnp.einsum('bshd,bmhd->bhsm', q.astype(jnp.float32), k.astype(jnp.float32)) * (q.shape[-1] ** -0.5)
    p = jax.nn.softmax(s, axis=-1)
    return jnp.einsum('bhsm,bmhd->bshd', p, v.astype(jnp.float32)).astype(q.dtype)


def hybrid_layer(x, mem, positions, norm_gain, mem_norm_gain, w_in, w_gla_alpha, b_gla_alpha,
                 gla_out_norm, nsa_q_norm, nsa_k_norm, pe_cmp_k, pe_cmp_v, w_cmp_k1, w_cmp_k2,
                 w_cmp_v1, w_cmp_v2, w_mem_kv, mem_q_norm, mem_k_norm, w_branch, w_out):
    B, S, _ = x.shape
    xn = rms_norm(x, norm_gain)
    proj = xn @ w_in
    (gla_q, gla_k, gla_v, gla_lr, gla_gate, nsa_q, nsa_kv, nsa_g, nsa_gate,
     mem_q, mem_gate, merge) = jnp.split(proj, [int(c) for c in np.cumsum(SPLITS)[:-1]], axis=-1)

    q = gla_q.reshape(B, S, GLA_HEADS, GLA_DK) * (GLA_DK ** -0.5)
    k = gla_k.reshape(B, S, GLA_HEADS, GLA_DK)
    v = gla_v.reshape(B, S, GLA_HEADS, GLA_DV)
    log_a = jax.nn.log_sigmoid((gla_lr @ w_gla_alpha + b_gla_alpha).astype(jnp.float32)) / GLA_TAU
    o = gla_chunked(q, k, v, log_a.reshape(B, S, GLA_HEADS, GLA_DK))
    y_a = rms_norm(o, gla_out_norm).reshape(B, S, -1) * jax.nn.silu(gla_gate)

    q = rope(rms_norm(nsa_q.reshape(B, S, NSA_HEADS, NSA_HD), nsa_q_norm), positions)
    kv = nsa_kv.reshape(B, S, 6, NSA_KV_HEADS, NSA_HD)
    k_c = rope(rms_norm(kv[:, :, 0], nsa_k_norm[0]), positions)
    k_s = rope(rms_norm(kv[:, :, 2], nsa_k_norm[1]), positions)
    k_w = rope(rms_norm(kv[:, :, 4], nsa_k_norm[2]), positions)
    kc = nsa_compress(k_c, pe_cmp_k, w_cmp_k1, w_cmp_k2)
    vc = nsa_compress(kv[:, :, 1], pe_cmp_v, w_cmp_v1, w_cmp_v2)
    gates = jax.nn.sigmoid(nsa_g.reshape(B, S, NSA_HEADS, N_NSA_BRANCH))
    o = nsa_attention(q, kc, vc, k_s, kv[:, :, 3], k_w, kv[:, :, 5], gates)
    y_b = o * jax.nn.silu(nsa_gate)

    memn = rms_norm(mem, mem_norm_gain)
    mkv = (memn @ w_mem_kv).reshape(B, mem.shape[1], 2, MEM_HEADS, MEM_HD)
    mk = rms_norm(mkv[:, :, 0], mem_k_norm)
    mq = rms_norm(mem_q.reshape(B, S, MEM_HEADS, MEM_HD), mem_q_norm)
    o = memory_attention(mq, mk, mkv[:, :, 1])
    y_m = o.reshape(B, S, -1) * jax.nn.silu(mem_gate)

    branches = jnp.stack([y_a, y_b, y_m], axis=2)
    z = jnp.einsum('bscw,cwd->bscd', branches, w_branch)
    g = jax.nn.sigmoid(merge.reshape(B, S, N_BRANCH, D_MODEL))
    mixed = jnp.sum(g * z, axis=2)
    return x + mixed @ w_out


def setup_inputs(seed: int = 0) -> dict:
    key = jax.random.key(seed)
    ks = jax.random.split(key, 24)
    f32 = jnp.float32

    def normal(k, shape, scale):
        return jax.random.normal(k, shape, f32) * scale

    x = normal(ks[0], (BATCH, SEQ, D_MODEL), 1.0)
    mem = normal(ks[1], (BATCH, N_MEM, D_MODEL), 1.0)
    offset = jax.random.randint(ks[2], (BATCH,), 0, MAX_POS_OFFSET, dtype=jnp.int32)
    positions = (offset[:, None] + jnp.arange(SEQ, dtype=jnp.int32)[None, :]).astype(jnp.int32)
    return {
        "x": x,
        "mem": mem,
        "positions": positions,
        "norm_gain": 1.0 + normal(ks[3], (DEPTH, D_MODEL), 0.02),
        "mem_norm_gain": 1.0 + normal(ks[4], (DEPTH, D_MODEL), 0.02),
        "w_in": normal(ks[5], (DEPTH, D_MODEL, D_IN), D_MODEL ** -0.5),
        "w_gla_alpha": normal(ks[6], (DEPTH, GLA_RANK, GLA_HEADS * GLA_DK), GLA_RANK ** -0.5),
        "b_gla_alpha": normal(ks[7], (DEPTH, GLA_HEADS * GLA_DK), 0.1),
        "gla_out_norm": 1.0 + normal(ks[8], (DEPTH, GLA_DV), 0.02),
        "nsa_q_norm": 1.0 + normal(ks[9], (DEPTH, NSA_HD), 0.02),
        "nsa_k_norm": 1.0 + normal(ks[10], (DEPTH, N_NSA_BRANCH, NSA_HD), 0.02),
        "pe_cmp_k": normal(ks[11], (DEPTH, CMP_LEN, NSA_HD), 0.1),
        "pe_cmp_v": normal(ks[12], (DEPTH, CMP_LEN, NSA_HD), 0.1),
        "w_cmp_k1": normal(ks[13], (DEPTH, CMP_LEN, NSA_HD, CMP_HIDDEN), (CMP_LEN * NSA_HD) ** -0.5),
        "w_cmp_k2": normal(ks[14], (DEPTH, CMP_HIDDEN, NSA_HD), CMP_HIDDEN ** -0.5),
        "w_cmp_v1": normal(ks[15], (DEPTH, CMP_LEN, NSA_HD, CMP_HIDDEN), (CMP_LEN * NSA_HD) ** -0.5),
        "w_cmp_v2": normal(ks[16], (DEPTH, CMP_HIDDEN, NSA_HD), CMP_HIDDEN ** -0.5),
        "w_mem_kv": normal(ks[17], (DEPTH, D_MODEL, 2 * MEM_HEADS * MEM_HD), D_MODEL ** -0.5),
        "mem_q_norm": 1.0 + normal(ks[18], (DEPTH, MEM_HD), 0.02),
        "mem_k_norm": 1.0 + normal(ks[19], (DEPTH, MEM_HD), 0.02),
        "w_branch": normal(ks[20], (DEPTH, N_BRANCH, BRANCH_WIDTH, D_MODEL), BRANCH_WIDTH ** -0.5),
        "w_out": normal(ks[21], (DEPTH, D_MODEL, D_MODEL), D_MODEL ** -0.5),
    }


def reference(x, mem, positions, norm_gain, mem_norm_gain, w_in, w_gla_alpha, b_gla_alpha,
              gla_out_norm, nsa_q_norm, nsa_k_norm, pe_cmp_k, pe_cmp_v, w_cmp_k1, w_cmp_k2,
              w_cmp_v1, w_cmp_v2, w_mem_kv, mem_q_norm, mem_k_norm, w_branch, w_out):
    h = x
    for l in range(DEPTH):
        h = hybrid_layer(h, mem, positions, norm_gain[l], mem_norm_gain[l], w_in[l], w_gla_alpha[l],
                         b_gla_alpha[l], gla_out_norm[l], nsa_q_norm[l], nsa_k_norm[l], pe_cmp_k[l],
                         pe_cmp_v[l], w_cmp_k1[l], w_cmp_k2[l], w_cmp_v1[l], w_cmp_v2[l], w_mem_kv[l],
                         mem_q_norm[l], mem_k_norm[l], w_branch[l], w_out[l])
    return h
```

```python
import functools

import numpy as np
import jax
import jax.numpy as jnp
from jax import lax
from jax.experimental import pallas as pl
from jax.experimental.pallas import tpu as pltpu

F32 = jnp.float32
BF16 = jnp.bfloat16
HIGHEST = lax.Precision.HIGHEST

D_MODEL = 1024
ROPE_THETA = 10000.0
EPS = 1e-6
NEG = -1e30
FORCE = 1e9

GLA_HEADS = 4
GLA_DK = 128
GLA_DV = 256
GLA_RANK = 16
GLA_TAU = 16.0
GLA_CHUNK = 64
GLA_SUB = 16

NSA_HEADS = 16
NSA_KV_HEADS = 4
NSA_GROUP = 4
NSA_HD = 64
CMP_LEN = 32
CMP_STRIDE = 16
CMP_HIDDEN = 256
SLC_BLOCK = 64
SLC_TOPN = 16
WINDOW = 512
N_NSA_BRANCH = 3
SLC_LANES = 128
SLC_CHUNK = 512
WIN_KEYS = 640

MEM_HEADS = 4
MEM_HD = 256

C_GV, C_GG, C_NQ, C_NGATE, C_MQ, C_MG, C_MERGE = 0, 1024, 2048, 3072, 4096, 5120, 6144
C_NKV, C_GQ, C_GK, C_LR, C_NSG = 9216, 10752, 11264, 11776, 11904
NP = 12288

VMEM_LIMIT = 48 * 1024 * 1024

NT = (((1,), (1,)), ((), ()))
TN = (((0,), (0,)), ((), ()))


def _cp(sem):
    return pltpu.CompilerParams(dimension_semantics=sem, vmem_limit_bytes=VMEM_LIMIT)


def _silu(x):
    return x * (1.0 / (1.0 + jnp.exp(-x)))


def _sigmoid(x):
    return 1.0 / (1.0 + jnp.exp(-x))


def _proj_kernel(x_ref, g_ref, w_ref, o_ref, xn_ref):
    @pl.when(pl.program_id(1) == 0)
    def _():
        x = x_ref[...]
        ms = jnp.mean(x * x, axis=-1, keepdims=True)
        xn_ref[...] = (x * lax.rsqrt(ms + EPS) * g_ref[...]).astype(BF16)

    o_ref[...] = jnp.dot(xn_ref[...], w_ref[...], preferred_element_type=F32)


def _proj(x2, gain, w_all, tm=512, tn=1024):
    M = x2.shape[0]
    return pl.pallas_call(
        _proj_kernel,
        grid=(M // tm, NP // tn),
        in_specs=[
            pl.BlockSpec((tm, D_MODEL), lambda i, j: (i, 0)),
            pl.BlockSpec((1, D_MODEL), lambda i, j: (0, 0)),
            pl.BlockSpec((D_MODEL, tn), lambda i, j: (0, j)),
        ],
        out_specs=pl.BlockSpec((tm, tn), lambda i, j: (i, j)),
        out_shape=jax.ShapeDtypeStruct((M, NP), F32),
        scratch_shapes=[pltpu.VMEM((tm, D_MODEL), BF16)],
        compiler_params=_cp(("parallel", "arbitrary")),
        name="proj",
    )(x2, gain, w_all)


def _gla_kernel(q_ref, k_ref, v_ref, gate_ref, lr_ref, wa_ref, ba_ref, gn_ref, o_ref, st_ref, *, n_chunks):
    C, SB = GLA_CHUNK, GLA_SUB

    @pl.when(pl.program_id(2) == 0)
    def _():
        st_ref[...] = jnp.zeros_like(st_ref)

    wa = wa_ref[...]
    ba = ba_ref[...]
    gn = gn_ref[...]
    ri = lax.broadcasted_iota(jnp.int32, (C, C), 0)
    ci = lax.broadcasted_iota(jnp.int32, (C, C), 1)
    tri = (ri >= ci).astype(F32)
    si = lax.broadcasted_iota(jnp.int32, (SB, SB), 0)
    sj = lax.broadcasted_iota(jnp.int32, (SB, SB), 1)
    sub_causal = si >= sj

    def chunk(c, carry):
        rows = pl.ds(pl.multiple_of(c * C, C), C)
        q = q_ref[0, rows, :] * (GLA_DK ** -0.5)
        k = k_ref[0, rows, :]
        v = v_ref[0, rows, :]
        vb = v.astype(BF16)
        z = jnp.dot(lr_ref[0, rows, :], wa, precision=HIGHEST, preferred_element_type=F32) + ba
        la = -(jnp.maximum(-z, 0.0) + jnp.log(1.0 + jnp.exp(-jnp.abs(z)))) * (1.0 / GLA_TAU)
        g = jnp.dot(tri, la, precision=HIGHEST, preferred_element_type=F32)

        outs = []
        for i in range(C // SB):
            lo = i * SB
            gi = g[lo:lo + SB]
            qi = q[lo:lo + SB]
            ki = k[lo:lo + SB]
            d = gi[:, None, :] - gi[None, :, :]
            e = jnp.exp(jnp.minimum(d, 0.0))
            a_d = jnp.sum(qi[:, None, :] * ki[None, :, :] * e, axis=-1)
            a_d = jnp.where(sub_causal, a_d, 0.0)
            o_i = jnp.dot(a_d.astype(BF16), vb[lo:lo + SB], preferred_element_type=F32)
            if i > 0:
                r = g[lo:lo + 1]
                qt = qi * jnp.exp(gi - r)
                kt = k[:lo] * jnp.exp(r - g[:lo])
                a_o = lax.dot_general(qt.astype(BF16), kt.astype(BF16), NT, preferred_element_type=F32)
                o_i = o_i + jnp.dot(a_o.astype(BF16), vb[:lo], preferred_element_type=F32)
            outs.append(o_i)
        o = jnp.concatenate(outs, axis=0)

        st = st_ref[...]
        qg = q * jnp.exp(g)
        o = o + lax.dot_general(qg.astype(BF16), st.astype(BF16), NT, preferred_element_type=F32)
        gl = g[C - 1:C]
        kd = k * jnp.exp(gl - g)
        st_ref[...] = st * jnp.exp(gl) + lax.dot_general(vb, kd.astype(BF16), TN, preferred_element_type=F32)

        ms = jnp.mean(o * o, axis=-1, keepdims=True)
        y = o * lax.rsqrt(ms + EPS) * gn
        o_ref[0, rows, :] = (y * _silu(gate_ref[0, rows, :])).astype(o_ref.dtype)
        return carry

    lax.fori_loop(0, n_chunks, chunk, 0)


def _gla(proj3, wa, ba, gn, tr=512):
    B, S, _ = proj3.shape
    H = GLA_HEADS
    kern = functools.partial(_gla_kernel, n_chunks=tr // GLA_CHUNK)
    return pl.pallas_call(
        kern,
        grid=(B, H, S // tr),
        in_specs=[
            pl.BlockSpec((1, tr, GLA_DK), lambda b, h, r: (b, r, C_GQ // GLA_DK + h)),
            pl.BlockSpec((1, tr, GLA_DK), lambda b, h, r: (b, r, C_GK // GLA_DK + h)),
            pl.BlockSpec((1, tr, GLA_DV), lambda b, h, r: (b, r, C_GV // GLA_DV + h)),
            pl.BlockSpec((1, tr, GLA_DV), lambda b, h, r: (b, r, C_GG // GLA_DV + h)),
            pl.BlockSpec((1, tr, 128), lambda b, h, r: (b, r, C_LR // 128)),
            pl.BlockSpec((128, GLA_DK), lambda b, h, r: (0, h)),
            pl.BlockSpec((1, GLA_DK), lambda b, h, r: (0, h)),
            pl.BlockSpec((1, GLA_DV), lambda b, h, r: (0, 0)),
        ],
        out_specs=pl.BlockSpec((1, tr, GLA_DV), lambda b, h, r: (b, r, h)),
        out_shape=jax.ShapeDtypeStruct((B, S, H * GLA_DV), BF16),
        scratch_shapes=[pltpu.VMEM((GLA_DV, GLA_DK), F32)],
        compiler_params=_cp(("parallel", "parallel", "arbitrary")),
        name="gla",
    )(proj3, proj3, proj3, proj3, proj3, wa, ba, gn)


def _group_meansq(x, gmat):
    sq = x * x
    hi = sq.astype(BF16)
    lo = (sq - hi.astype(F32)).astype(BF16)
    s = jnp.dot(hi, gmat, preferred_element_type=F32) + jnp.dot(lo, gmat, preferred_element_type=F32)
    return s * (1.0 / NSA_HD)


def _rope(x, cos, sin, first_half):
    w = x.shape[-1]
    rot = jnp.where(first_half, -pltpu.roll(x, w - NSA_HD // 2, 1), pltpu.roll(x, NSA_HD // 2, 1))
    return x * cos + rot * sin


def _nsa_prep_kernel(q_ref, kvc_ref, kvs_ref, kvw_ref, pos_ref, inv_ref, qn_ref, kn_ref, gm_ref,
                     qr_ref, kc_ref, vc_ref, ks_ref, vs_ref, kw_ref, vw_ref):
    tr = q_ref.shape[1]
    W = NSA_KV_HEADS * NSA_HD
    gmat = gm_ref[...]
    ang = pos_ref[0].astype(F32) * inv_ref[...]
    cos1, sin1 = jnp.cos(ang), jnp.sin(ang)
    cos = jnp.concatenate([cos1, cos1], axis=1)
    sin = jnp.concatenate([sin1, sin1], axis=1)
    lane = lax.broadcasted_iota(jnp.int32, (tr, W), 1)
    first_half = (lane % NSA_HD) < (NSA_HD // 2)

    def norm_rope(x, gain):
        y = x * lax.rsqrt(_group_meansq(x, gmat) + EPS) * gain
        return _rope(y, cos, sin, first_half)

    qn = qn_ref[...]
    for s in range(NSA_HEADS * NSA_HD // W):
        xq = q_ref[0, :, s * W:(s + 1) * W]
        qr_ref[0, :, s * W:(s + 1) * W] = (norm_rope(xq, qn) * (NSA_HD ** -0.5)).astype(qr_ref.dtype)

    def split_heads(dst_ref, slab):
        for h in range(NSA_KV_HEADS):
            dst_ref[0, h] = slab[:, h * NSA_HD:(h + 1) * NSA_HD].astype(dst_ref.dtype)

    for src_ref, kdst, vdst, bi in ((kvc_ref, kc_ref, vc_ref, 0), (kvs_ref, ks_ref, vs_ref, 1), (kvw_ref, kw_ref, vw_ref, 2)):
        split_heads(kdst, norm_rope(src_ref[0, :, :W], kn_ref[bi:bi + 1, :]))
        split_heads(vdst, src_ref[0, :, W:])


def _nsa_prep(proj3, pos3, inv, qn, kn, gmat, tr=256):
    B, S, _ = proj3.shape
    Hk, HD = NSA_KV_HEADS, NSA_HD
    hm = lambda dt: jax.ShapeDtypeStruct((B, Hk, S, HD), dt)
    hspec = pl.BlockSpec((1, Hk, tr, HD), lambda b, r: (b, 0, r, 0))
    return pl.pallas_call(
        _nsa_prep_kernel,
        grid=(B, S // tr),
        in_specs=[
            pl.BlockSpec((1, tr, 1024), lambda b, r: (b, r, C_NQ // 1024)),
            pl.BlockSpec((1, tr, 512), lambda b, r: (b, r, C_NKV // 512)),
            pl.BlockSpec((1, tr, 512), lambda b, r: (b, r, C_NKV // 512 + 1)),
            pl.BlockSpec((1, tr, 512), lambda b, r: (b, r, C_NKV // 512 + 2)),
            pl.BlockSpec((1, tr, 1), lambda b, r: (b, r, 0)),
            pl.BlockSpec((1, 128), lambda b, r: (0, 0)),
            pl.BlockSpec((1, 256), lambda b, r: (0, 0)),
            pl.BlockSpec((3, 256), lambda b, r: (0, 0)),
            pl.BlockSpec((256, 256), lambda b, r: (0, 0)),
        ],
        out_specs=[pl.BlockSpec((1, tr, 1024), lambda b, r: (b, r, 0))] + [hspec] * 6,
        out_shape=[jax.ShapeDtypeStruct((B, S, 1024), BF16), hm(F32), hm(F32), hm(BF16), hm(BF16), hm(BF16), hm(BF16)],
        compiler_params=_cp(("parallel", "parallel")),
        name="nsa_prep",
    )(proj3, proj3, proj3, proj3, pos3, inv, qn, kn, gmat)


def _compress_kernel(t_ref, pe_ref, w1a_ref, w1b_ref, w2_ref, o_ref):
    t = t_ref[0, 0].astype(BF16)
    w1a, w1b = w1a_ref[...], w1b_ref[...]
    u = jnp.dot(t, w1a, preferred_element_type=F32)
    v = jnp.dot(t, w1b, preferred_element_type=F32)
    pe = pe_ref[...].astype(BF16)
    c = (jnp.dot(pe[0:8], w1a, preferred_element_type=F32) + jnp.dot(pe[8:16], w1b, preferred_element_type=F32))[0:1]
    n = v.shape[0]
    h = u + pltpu.roll(v, n - 1, 0) + c
    h = jax.nn.gelu(h)
    o_ref[0, 0] = jnp.dot(h.astype(BF16), w2_ref[...], preferred_element_type=F32).astype(o_ref.dtype)


def _compress(tok, pe2, w1a, w1b, w2):
    B, Hk, S, HD = tok.shape
    n = S // CMP_STRIDE
    t2 = tok.reshape(B, Hk, n, CMP_STRIDE * HD)
    return pl.pallas_call(
        _compress_kernel,
        grid=(B, Hk),
        in_specs=[
            pl.BlockSpec((1, 1, n, CMP_STRIDE * HD), lambda b, h: (b, h, 0, 0)),
            pl.BlockSpec((16, CMP_STRIDE * HD), lambda b, h: (0, 0)),
            pl.BlockSpec((CMP_STRIDE * HD, CMP_HIDDEN), lambda b, h: (0, 0)),
            pl.BlockSpec((CMP_STRIDE * HD, CMP_HIDDEN), lambda b, h: (0, 0)),
            pl.BlockSpec((CMP_HIDDEN, HD), lambda b, h: (0, 0)),
        ],
        out_specs=pl.BlockSpec((1, 1, n, HD), lambda b, h: (b, h, 0, 0)),
        out_shape=jax.ShapeDtypeStruct((B, Hk, n, HD), BF16),
        compiler_params=_cp(("parallel", "parallel")),
        name="compress",
    )(t2, pe2, w1a, w1b, w2)


def _nsa_attn_kernel(q_ref, kc_ref, vc_ref, ks_ref, vs_ref, kw_ref, vw_ref, gt_ref, gate_ref, ov_ref, o_ref):
    G, HD, QB = NSA_GROUP, NSA_HD, SLC_BLOCK
    R = G * QB
    hk = pl.program_id(1)
    tile = pl.program_id(2)
    t0 = tile * QB
    qs = q_ref[0]
    q = jnp.concatenate([qs[:, g * HD:(g + 1) * HD] for g in range(G)], axis=0)
    t = t0 + lax.broadcasted_iota(jnp.int32, (R, 1), 0) % QB

    nc = kc_ref.shape[2]
    s = lax.dot_general(q, kc_ref[0, 0], NT, preferred_element_type=F32)
    n_id = lax.broadcasted_iota(jnp.int32, (1, nc), 1)
    valid = (n_id * CMP_STRIDE + (CMP_LEN - 1)) <= t
    sm = jnp.where(valid, s, NEG)
    m = jnp.max(sm, axis=-1, keepdims=True)
    e = jnp.where(valid, jnp.exp(sm - m), 0.0)
    den = jnp.sum(e, axis=-1, keepdims=True)
    p = e / jnp.where(den > 0.0, den, 1.0)
    o_cmp = jnp.dot(p.astype(BF16), vc_ref[0, 0], preferred_element_type=F32)

    p4 = p[0:QB] + p[QB:2 * QB] + p[2 * QB:3 * QB] + p[3 * QB:4 * QB]
    p4h = p4.astype(BF16)
    p4l = (p4 - p4h.astype(F32)).astype(BF16)
    ov = ov_ref[...]
    imp = jnp.dot(p4h, ov, preferred_element_type=F32) + jnp.dot(p4l, ov, preferred_element_type=F32)
    blk = lax.broadcasted_iota(jnp.int32, (QB, SLC_LANES), 1)
    cur = tile
    forced = (blk == 0) | (blk == cur) | (blk == cur - 1)
    score = jnp.where(forced, FORCE, jnp.where(blk <= cur, imp, NEG))

    def pick(_, carry):
        sc, sel = carry
        mx = jnp.max(sc, axis=-1, keepdims=True)
        first = jnp.min(jnp.where(sc == mx, blk, SLC_LANES), axis=-1, keepdims=True)
        hit = blk == first
        return jnp.where(hit, -jnp.inf, sc), jnp.where(hit, 1.0, sel)

    _, sel = lax.fori_loop(0, SLC_TOPN, pick, (score, jnp.zeros((QB, SLC_LANES), F32)))
    notsel = (1.0 - sel).astype(BF16)
    notsel4 = jnp.concatenate([notsel] * G, axis=0)

    erow = lax.broadcasted_iota(jnp.int32, (SLC_LANES, SLC_CHUNK), 0)
    ecol = lax.broadcasted_iota(jnp.int32, (SLC_LANES, SLC_CHUNK), 1) // SLC_BLOCK
    kcol = lax.broadcasted_iota(jnp.int32, (1, SLC_CHUNK), 1)

    def slc_step(c, carry):
        m_i, l_i, acc = carry
        k0 = pl.multiple_of(c * SLC_CHUNK, SLC_CHUNK)
        kk = ks_ref[0, 0, pl.ds(k0, SLC_CHUNK), :]
        vv = vs_ref[0, 0, pl.ds(k0, SLC_CHUNK), :]
        sc = lax.dot_general(q, kk, NT, preferred_element_type=F32)
        expand = (erow == ecol + c * (SLC_CHUNK // SLC_BLOCK)).astype(BF16)
        off = jnp.dot(notsel4, expand, preferred_element_type=F32)
        ok = (off < 0.5) & ((kcol + k0) <= t)
        sc = jnp.where(ok, sc, NEG)
        m_new = jnp.maximum(m_i, jnp.max(sc, axis=-1, keepdims=True))
        alpha = jnp.exp(m_i - m_new)
        pp = jnp.exp(sc - m_new)
        l_new = alpha * l_i + jnp.sum(pp, axis=-1, keepdims=True)
        acc = alpha * acc + jnp.dot(pp.astype(BF16), vv, preferred_element_type=F32)
        return m_new, l_new, acc

    n_chunks = (t0 + QB + SLC_CHUNK - 1) // SLC_CHUNK
    init = (jnp.full((R, 1), NEG, F32), jnp.zeros((R, 1), F32), jnp.zeros((R, HD), F32))
    _, l_s, acc_s = lax.fori_loop(0, n_chunks, slc_step, init)
    o_slc = acc_s / l_s

    w0 = pl.multiple_of(jnp.maximum(t0 + QB - WIN_KEYS, 0), QB)
    kk = kw_ref[0, 0, pl.ds(w0, WIN_KEYS), :]
    vv = vw_ref[0, 0, pl.ds(w0, WIN_KEYS), :]
    sw = lax.dot_general(q, kk, NT, preferred_element_type=F32)
    kp = w0 + lax.broadcasted_iota(jnp.int32, (1, WIN_KEYS), 1)
    okw = (kp <= t) & (kp > t - WINDOW)
    sw = jnp.where(okw, sw, NEG)
    mw = jnp.max(sw, axis=-1, keepdims=True)
    pw = jnp.exp(sw - mw)
    o_win = jnp.dot(pw.astype(BF16), vv, preferred_element_type=F32) / jnp.sum(pw, axis=-1, keepdims=True)

    sig = _sigmoid(gt_ref[0])
    glane = lax.broadcasted_iota(jnp.int32, (QB, 128), 1)
    outs = []
    for g in range(G):
        base = (hk * G + g) * N_NSA_BRANCH
        gc, gs, gw = [jnp.sum(jnp.where(glane == base + b, sig, 0.0), axis=-1, keepdims=True) for b in range(3)]
        r = slice(g * QB, (g + 1) * QB)
        outs.append(gc * o_cmp[r] + gs * o_slc[r] + gw * o_win[r])
    o = jnp.concatenate(outs, axis=1)
    o_ref[0] = (o * _silu(gate_ref[0])).astype(o_ref.dtype)


def _nsa_attn(qr, kc, vc, ks, vs, kw, vw, proj3, ov):
    B, S, _ = qr.shape
    Hk, HD, QB = NSA_KV_HEADS, NSA_HD, SLC_BLOCK
    nc = kc.shape[2]
    W = NSA_GROUP * HD
    full = lambda n: pl.BlockSpec((1, 1, n, HD), lambda b, h, t: (b, h, 0, 0))
    return pl.pallas_call(
        _nsa_attn_kernel,
        grid=(B, Hk, S // QB),
        in_specs=[
            pl.BlockSpec((1, QB, W), lambda b, h, t: (b, t, h)),
            full(nc), full(nc), full(S), full(S), full(S), full(S),
            pl.BlockSpec((1, QB, 128), lambda b, h, t: (b, t, C_NSG // 128)),
            pl.BlockSpec((1, QB, W), lambda b, h, t: (b, t, C_NGATE // W + h)),
            pl.BlockSpec((nc, SLC_LANES), lambda b, h, t: (0, 0)),
        ],
        out_specs=pl.BlockSpec((1, QB, W), lambda b, h, t: (b, t, h)),
        out_shape=jax.ShapeDtypeStruct((B, S, NSA_HEADS * HD), BF16),
        compiler_params=_cp(("parallel", "parallel", "arbitrary")),
        name="nsa_attn",
    )(qr, kc, vc, ks, vs, kw, vw, proj3, proj3, ov)


def _head_rms(x, gain, hd):
    outs = []
    for h in range(x.shape[-1] // hd):
        xh = x[:, h * hd:(h + 1) * hd]
        ms = jnp.mean(xh * xh, axis=-1, keepdims=True)
        outs.append(xh * lax.rsqrt(ms + EPS) * gain)
    return jnp.concatenate(outs, axis=1)


def _mem_prep_kernel(mem_ref, g_ref, w_ref, kn_ref, mk_ref, mv_ref):
    x = mem_ref[0]
    ms = jnp.mean(x * x, axis=-1, keepdims=True)
    xn = (x * lax.rsqrt(ms + EPS) * g_ref[...]).astype(BF16)
    kv = jnp.dot(xn, w_ref[...], preferred_element_type=F32)
    W = MEM_HEADS * MEM_HD
    mk_ref[0] = _head_rms(kv[:, :W], kn_ref[...], MEM_HD).astype(mk_ref.dtype)
    mv_ref[0] = kv[:, W:].astype(mv_ref.dtype)


def _mem_prep(mem, gain, w_kv, kn):
    B, N, D = mem.shape
    W = MEM_HEADS * MEM_HD
    return pl.pallas_call(
        _mem_prep_kernel,
        grid=(B,),
        in_specs=[
            pl.BlockSpec((1, N, D), lambda b: (b, 0, 0)),
            pl.BlockSpec((1, D), lambda b: (0, 0)),
            pl.BlockSpec((D, 2 * W), lambda b: (0, 0)),
            pl.BlockSpec((1, MEM_HD), lambda b: (0, 0)),
        ],
        out_specs=[pl.BlockSpec((1, N, W), lambda b: (b, 0, 0))] * 2,
        out_shape=[jax.ShapeDtypeStruct((B, N, W), BF16)] * 2,
        compiler_params=_cp(("parallel",)),
        name="mem_prep",
    )(mem, gain, w_kv, kn)


def _final_kernel(x_ref, ya_ref, yb_ref, mq_ref, mg_ref, mr_ref, mk_ref, mv_ref, qn_ref, wb_ref, wo_ref, o_ref):
    mq = _head_rms(mq_ref[...], qn_ref[...], MEM_HD)
    mk = mk_ref[0]
    mv = mv_ref[0]
    heads = []
    for h in range(MEM_HEADS):
        sl = slice(h * MEM_HD, (h + 1) * MEM_HD)
        s = lax.dot_general(mq[:, sl].astype(BF16), mk[:, sl], NT, preferred_element_type=F32) * (MEM_HD ** -0.5)
        m = jnp.max(s, axis=-1, keepdims=True)
        p = jnp.exp(s - m)
        o = jnp.dot(p.astype(BF16), mv[:, sl], preferred_element_type=F32) / jnp.sum(p, axis=-1, keepdims=True)
        heads.append(o)
    ym = jnp.concatenate(heads, axis=1) * _silu(mg_ref[...])

    mixed = None
    for c, y in enumerate((ya_ref[...], yb_ref[...], ym.astype(BF16))):
        z = jnp.dot(y, wb_ref[c], preferred_element_type=F32)
        term = _sigmoid(mr_ref[:, c * D_MODEL:(c + 1) * D_MODEL]) * z
        mixed = term if mixed is None else mixed + term
    o_ref[...] = x_ref[...] + jnp.dot(mixed.astype(BF16), wo_ref[...], preferred_element_type=F32)


def _final(x2, ya2, yb2, proj2, mk, mv, qn, wb, wo, S, tr=256):
    M, D = x2.shape
    N = mk.shape[1]
    nb = S // tr
    row = lambda c: pl.BlockSpec((tr, D), lambda i: (i, c))
    return pl.pallas_call(
        _final_kernel,
        grid=(M // tr,),
        in_specs=[
            row(0), row(0), row(0),
            row(C_MQ // D), row(C_MG // D),
            pl.BlockSpec((tr, 3 * D), lambda i: (i, C_MERGE // (3 * D))),
            pl.BlockSpec((1, N, D), lambda i: (i // nb, 0, 0)),
            pl.BlockSpec((1, N, D), lambda i: (i // nb, 0, 0)),
            pl.BlockSpec((1, MEM_HD), lambda i: (0, 0)),
            pl.BlockSpec((3, D, D), lambda i: (0, 0, 0)),
            pl.BlockSpec((D, D), lambda i: (0, 0)),
        ],
        out_specs=pl.BlockSpec((tr, D), lambda i: (i, 0)),
        out_shape=jax.ShapeDtypeStruct((M, D), F32),
        compiler_params=_cp(("parallel",)),
        name="final",
    )(x2, ya2, yb2, proj2, proj2, proj2, mk, mv, qn, wb, wo)


def _overlap_matrix(S):
    n_cmp = (S - CMP_LEN) // CMP_STRIDE + 1
    n_slc = S // SLC_BLOCK
    cs = np.arange(n_cmp)[:, None] * CMP_STRIDE
    ss = np.arange(n_slc)[None, :] * SLC_BLOCK
    ov = np.clip(np.minimum(cs + CMP_LEN, ss + SLC_BLOCK) - np.maximum(cs, ss), 0, None) / CMP_LEN
    out = np.zeros((S // CMP_STRIDE, SLC_LANES), np.float32)
    out[:n_cmp, :n_slc] = ov
    return jnp.asarray(out, BF16)


def _pad_cols(w, n):
    return jnp.pad(w, ((0, 0), (0, n - w.shape[1])))


def _layer(x, mem, positions, norm_gain, mem_norm_gain, w_in, w_gla_alpha, b_gla_alpha, gla_out_norm,
           nsa_q_norm, nsa_k_norm, pe_cmp_k, pe_cmp_v, w_cmp_k1, w_cmp_k2, w_cmp_v1, w_cmp_v2,
           w_mem_kv, mem_q_norm, mem_k_norm, w_branch, w_out):
    B, S, D = x.shape
    assert D == D_MODEL and S % SLC_CHUNK == 0 and S >= WIN_KEYS and S // SLC_BLOCK <= SLC_LANES

    o = np.cumsum([0, 512, 512, 1024, 16, 1024, 1024, 1536, 48, 1024, 1024, 1024, 3072])
    sec = [w_in[:, o[i]:o[i + 1]] for i in range(12)]
    gq, gk, gv, glr, gg, nq, nkv, nsg, ngate, mq, mg, merge = sec
    w_all = jnp.concatenate(
        [gv, gg, nq, ngate, mq, mg, merge, nkv, gq, gk, _pad_cols(glr, 128), _pad_cols(nsg, 128),
         jnp.zeros((D, NP - C_NSG - 128), w_in.dtype)], axis=1).astype(BF16)

    x2 = x.reshape(B * S, D)
    proj2 = _proj(x2, norm_gain.reshape(1, D), w_all)
    proj3 = proj2.reshape(B, S, NP)

    wa = jnp.pad(w_gla_alpha, ((0, 128 - GLA_RANK), (0, 0)))
    ya = _gla(proj3, wa, b_gla_alpha.reshape(1, -1), gla_out_norm.reshape(1, -1))

    half = NSA_HD // 2
    inv = ROPE_THETA ** (-jnp.arange(half, dtype=F32) / half)
    inv = jnp.tile(inv, 128 // half).reshape(1, 128)
    qn = jnp.tile(nsa_q_norm, NSA_KV_HEADS).reshape(1, -1)
    kn = jnp.tile(nsa_k_norm, (1, NSA_KV_HEADS))
    gid = np.arange(256) // NSA_HD
    gmat = jnp.asarray(gid[:, None] == gid[None, :], BF16)
    qr, kc_tok, vc_tok, ks, vs, kw, vw = _nsa_prep(proj3, positions.reshape(B, S, 1), inv, qn, kn, gmat)

    def cmp_args(pe, w1, w2):
        pe2 = pe.reshape(2, CMP_STRIDE * NSA_HD)
        pe16 = jnp.concatenate([jnp.broadcast_to(pe2[0:1], (8, pe2.shape[1])), jnp.broadcast_to(pe2[1:2], (8, pe2.shape[1]))], 0)
        w1f = w1.reshape(CMP_LEN * NSA_HD, CMP_HIDDEN).astype(BF16)
        return pe16, w1f[:CMP_STRIDE * NSA_HD], w1f[CMP_STRIDE * NSA_HD:], w2.astype(BF16)

    kc = _compress(kc_tok, *cmp_args(pe_cmp_k, w_cmp_k1, w_cmp_k2))
    vc = _compress(vc_tok, *cmp_args(pe_cmp_v, w_cmp_v1, w_cmp_v2))
    yb = _nsa_attn(qr, kc, vc, ks, vs, kw, vw, proj3, _overlap_matrix(S))

    mk, mv = _mem_prep(mem, mem_norm_gain.reshape(1, D), w_mem_kv.astype(BF16), mem_k_norm.reshape(1, -1))
    out = _final(x2, ya.reshape(B * S, D), yb.reshape(B * S, D), proj2, mk, mv, mem_q_norm.reshape(1, -1),
                 w_branch.astype(BF16), w_out.astype(BF16), S)
    return out.reshape(B, S, D)


def kernel(x, mem, positions, norm_gain, mem_norm_gain, w_in, w_gla_alpha, b_gla_alpha, gla_out_norm, nsa_q_norm, nsa_k_norm, pe_cmp_k, pe_cmp_v, w_cmp_k1, w_cmp_k2, w_cmp_v1, w_cmp_v2, w_mem_kv, mem_q_norm, mem_k_norm, w_branch, w_out):
    h = x
    for l in range(norm_gain.shape[0]):
        h = _layer(h, mem, positions, norm_gain[l], mem_norm_gain[l], w_in[l], w_gla_alpha[l], b_gla_alpha[l],
                   gla_out_norm[l], nsa_q_norm[l], nsa_k_norm[l], pe_cmp_k[l], pe_cmp_v[l], w_cmp_k1[l],
                   w_cmp_k2[l], w_cmp_v1[l], w_cmp_v2[l], w_mem_kv[l], mem_q_norm[l], mem_k_norm[l],
                   w_branch[l], w_out[l])
    return h
```

```python
import functools

import numpy as np
import jax
import jax.numpy as jnp
from jax import lax
from jax.experimental import pallas as pl
from jax.experimental.pallas import tpu as pltpu

F32 = jnp.float32
BF16 = jnp.bfloat16
HIGHEST = lax.Precision.HIGHEST

D_MODEL = 1024
ROPE_THETA = 10000.0
EPS = 1e-6
NEG = -1e30
FORCE = 1e9

GLA_HEADS = 4
GLA_DK = 128
GLA_DV = 256
GLA_RANK = 16
GLA_TAU = 16.0
GLA_CHUNK = 64
GLA_SUB = 16

NSA_HEADS = 16
NSA_KV_HEADS = 4
NSA_GROUP = 4
NSA_HD = 64
CMP_LEN = 32
CMP_STRIDE = 16
CMP_HIDDEN = 256
SLC_BLOCK = 64
SLC_TOPN = 16
WINDOW = 512
N_NSA_BRANCH = 3
SLC_LANES = 128
SLC_CHUNK = 512
VT_ROWS = NSA_HD + 16
LOG2E = 1.4426950408889634
WIN_ALIGN = 128
WIN_KEYS = 640
KAUG = SLC_LANES + NSA_HD

MEM_HEADS = 4
MEM_HD = 256

C_GV, C_GG, C_NQ, C_NGATE, C_MQ, C_MG, C_MERGE = 0, 1024, 2048, 3072, 4096, 5120, 6144
C_NKV, C_GQ, C_GK, C_LR, C_NSG = 9216, 10752, 11264, 11776, 11904
NP = 12288

VMEM_LIMIT = 48 * 1024 * 1024

NT = (((1,), (1,)), ((), ()))
TN = (((0,), (0,)), ((), ()))


def _cp(sem):
    return pltpu.CompilerParams(dimension_semantics=sem, vmem_limit_bytes=VMEM_LIMIT)


def _silu(x):
    return x * (1.0 / (1.0 + jnp.exp(-x)))


def _sigmoid(x):
    return 1.0 / (1.0 + jnp.exp(-x))


def _proj_kernel(x_ref, g_ref, w_ref, o_ref, xn_ref):
    @pl.when(pl.program_id(1) == 0)
    def _():
        x = x_ref[...]
        ms = jnp.mean(x * x, axis=-1, keepdims=True)
        xn_ref[...] = (x * lax.rsqrt(ms + EPS) * g_ref[...]).astype(BF16)

    o_ref[...] = jnp.dot(xn_ref[...], w_ref[...], preferred_element_type=F32)


def _proj(x2, gain, w_all, tm=512, tn=1024):
    M = x2.shape[0]
    return pl.pallas_call(
        _proj_kernel,
        grid=(M // tm, NP // tn),
        in_specs=[
            pl.BlockSpec((tm, D_MODEL), lambda i, j: (i, 0)),
            pl.BlockSpec((1, D_MODEL), lambda i, j: (0, 0)),
            pl.BlockSpec((D_MODEL, tn), lambda i, j: (0, j)),
        ],
        out_specs=pl.BlockSpec((tm, tn), lambda i, j: (i, j)),
        out_shape=jax.ShapeDtypeStruct((M, NP), F32),
        scratch_shapes=[pltpu.VMEM((tm, D_MODEL), BF16)],
        compiler_params=_cp(("parallel", "arbitrary")),
        name="proj",
    )(x2, gain, w_all)


def _gla_kernel(q_ref, k_ref, v_ref, gate_ref, lr_ref, wa_ref, ba_ref, gn_ref, o_ref, st_ref, *, n_chunks):
    C, SB = GLA_CHUNK, GLA_SUB

    @pl.when(pl.program_id(2) == 0)
    def _():
        st_ref[...] = jnp.zeros_like(st_ref)

    wa = wa_ref[...]
    ba = ba_ref[...]
    gn = gn_ref[...]
    ri = lax.broadcasted_iota(jnp.int32, (C, C), 0)
    ci = lax.broadcasted_iota(jnp.int32, (C, C), 1)
    tri = (ri >= ci).astype(F32)
    si = lax.broadcasted_iota(jnp.int32, (SB, SB), 0)
    sj = lax.broadcasted_iota(jnp.int32, (SB, SB), 1)
    sub_causal = si >= sj

    def chunk(c, carry):
        rows = pl.ds(pl.multiple_of(c * C, C), C)
        q = q_ref[0, rows, :] * (GLA_DK ** -0.5)
        k = k_ref[0, rows, :]
        v = v_ref[0, rows, :]
        vb = v.astype(BF16)
        z = jnp.dot(lr_ref[0, rows, :], wa, precision=HIGHEST, preferred_element_type=F32) + ba
        la = -(jnp.maximum(-z, 0.0) + jnp.log(1.0 + jnp.exp(-jnp.abs(z)))) * (1.0 / GLA_TAU)
        g = jnp.dot(tri, la, precision=HIGHEST, preferred_element_type=F32)

        outs = []
        for i in range(C // SB):
            lo = i * SB
            gi = g[lo:lo + SB]
            qi = q[lo:lo + SB]
            ki = k[lo:lo + SB]
            d = gi[:, None, :] - gi[None, :, :]
            e = jnp.exp(jnp.minimum(d, 0.0))
            a_d = jnp.sum(qi[:, None, :] * ki[None, :, :] * e, axis=-1)
            a_d = jnp.where(sub_causal, a_d, 0.0)
            o_i = jnp.dot(a_d.astype(BF16), vb[lo:lo + SB], preferred_element_type=F32)
            if i > 0:
                r = g[lo:lo + 1]
                qt = qi * jnp.exp(gi - r)
                kt = k[:lo] * jnp.exp(r - g[:lo])
                a_o = lax.dot_general(qt.astype(BF16), kt.astype(BF16), NT, preferred_element_type=F32)
                o_i = o_i + jnp.dot(a_o.astype(BF16), vb[:lo], preferred_element_type=F32)
            outs.append(o_i)
        o = jnp.concatenate(outs, axis=0)

        st = st_ref[...]
        qg = q * jnp.exp(g)
        o = o + lax.dot_general(qg.astype(BF16), st.astype(BF16), NT, preferred_element_type=F32)
        gl = g[C - 1:C]
        kd = k * jnp.exp(gl - g)
        st_ref[...] = st * jnp.exp(gl) + lax.dot_general(vb, kd.astype(BF16), TN, preferred_element_type=F32)

        ms = jnp.mean(o * o, axis=-1, keepdims=True)
        y = o * lax.rsqrt(ms + EPS) * gn
        o_ref[0, rows, :] = (y * _silu(gate_ref[0, rows, :])).astype(o_ref.dtype)
        return carry

    lax.fori_loop(0, n_chunks, chunk, 0)


def _gla(proj3, wa, ba, gn, tr=512):
    B, S, _ = proj3.shape
    H = GLA_HEADS
    kern = functools.partial(_gla_kernel, n_chunks=tr // GLA_CHUNK)
    return pl.pallas_call(
        kern,
        grid=(B, H, S // tr),
        in_specs=[
            pl.BlockSpec((1, tr, GLA_DK), lambda b, h, r: (b, r, C_GQ // GLA_DK + h)),
            pl.BlockSpec((1, tr, GLA_DK), lambda b, h, r: (b, r, C_GK // GLA_DK + h)),
            pl.BlockSpec((1, tr, GLA_DV), lambda b, h, r: (b, r, C_GV // GLA_DV + h)),
            pl.BlockSpec((1, tr, GLA_DV), lambda b, h, r: (b, r, C_GG // GLA_DV + h)),
            pl.BlockSpec((1, tr, 128), lambda b, h, r: (b, r, C_LR // 128)),
            pl.BlockSpec((128, GLA_DK), lambda b, h, r: (0, h)),
            pl.BlockSpec((1, GLA_DK), lambda b, h, r: (0, h)),
            pl.BlockSpec((1, GLA_DV), lambda b, h, r: (0, 0)),
        ],
        out_specs=pl.BlockSpec((1, tr, GLA_DV), lambda b, h, r: (b, r, h)),
        out_shape=jax.ShapeDtypeStruct((B, S, H * GLA_DV), BF16),
        scratch_shapes=[pltpu.VMEM((GLA_DV, GLA_DK), F32)],
        compiler_params=_cp(("parallel", "parallel", "arbitrary")),
        name="gla",
    )(proj3, proj3, proj3, proj3, proj3, wa, ba, gn)


def _group_meansq(x, gmat):
    sq = x * x
    hi = sq.astype(BF16)
    lo = (sq - hi.astype(F32)).astype(BF16)
    s = jnp.dot(hi, gmat, preferred_element_type=F32) + jnp.dot(lo, gmat, preferred_element_type=F32)
    return s * (1.0 / NSA_HD)


def _rope(x, cos, sin, first_half):
    w = x.shape[-1]
    rot = jnp.where(first_half, -pltpu.roll(x, w - NSA_HD // 2, 1), pltpu.roll(x, NSA_HD // 2, 1))
    return x * cos + rot * sin


def _nsa_prep_kernel(q_ref, kvc_ref, kvs_ref, kvw_ref, pos_ref, inv_ref, qn_ref, kn_ref, gm_ref,
                     qr_ref, kc_ref, vc_ref, ks_ref, vst_ref, kw_ref, vwt_ref):
    tr = q_ref.shape[1]
    W = NSA_KV_HEADS * NSA_HD
    HD = NSA_HD
    gmat = gm_ref[...]
    ang = pos_ref[0].astype(F32) * inv_ref[...]
    cos1, sin1 = jnp.cos(ang), jnp.sin(ang)
    cos = jnp.concatenate([cos1, cos1], axis=1)
    sin = jnp.concatenate([sin1, sin1], axis=1)
    lane = lax.broadcasted_iota(jnp.int32, (tr, W), 1)
    first_half = (lane % HD) < (HD // 2)

    def norm_rope(x, gain):
        y = x * lax.rsqrt(_group_meansq(x, gmat) + EPS) * gain
        return _rope(y, cos, sin, first_half)

    qn = qn_ref[...]
    for s in range(NSA_HEADS * HD // W):
        xq = q_ref[0, :, s * W:(s + 1) * W]
        qr_ref[0, :, s * W:(s + 1) * W] = (norm_rope(xq, qn) * (HD ** -0.5 * LOG2E)).astype(qr_ref.dtype)

    kc = norm_rope(kvc_ref[0, :, :W], kn_ref[0:1, :])
    vc = kvc_ref[0, :, W:]
    for h in range(NSA_KV_HEADS):
        kc_ref[0, h] = kc[:, h * HD:(h + 1) * HD]
        vc_ref[0, h] = vc[:, h * HD:(h + 1) * HD]

    ks = norm_rope(kvs_ref[0, :, :W], kn_ref[1:2, :]).astype(BF16)
    row_blk = (pl.program_id(1) * tr + lax.broadcasted_iota(jnp.int32, (tr, SLC_LANES), 0)) // SLC_BLOCK
    onehot = (row_blk == lax.broadcasted_iota(jnp.int32, (tr, SLC_LANES), 1)).astype(BF16)
    vst = kvs_ref[0, :, W:].T
    for h in range(NSA_KV_HEADS):
        ks_ref[0, h] = jnp.concatenate([onehot, ks[:, h * HD:(h + 1) * HD]], axis=1)
        for j in range(tr // SLC_CHUNK):
            vst_ref[0, h, j, :HD] = vst[h * HD:(h + 1) * HD, j * SLC_CHUNK:(j + 1) * SLC_CHUNK].astype(BF16)
            vst_ref[0, h, j, HD:] = jnp.ones((VT_ROWS - HD, SLC_CHUNK), BF16)

    kw = norm_rope(kvw_ref[0, :, :W], kn_ref[2:3, :]).astype(BF16)
    vwt = kvw_ref[0, :, W:].T
    for h in range(NSA_KV_HEADS):
        kw_ref[0, h] = kw[:, h * HD:(h + 1) * HD]
        for j in range(tr // WIN_ALIGN):
            vwt_ref[0, h, j] = vwt[h * HD:(h + 1) * HD, j * WIN_ALIGN:(j + 1) * WIN_ALIGN].astype(BF16)


def _nsa_prep(proj3, pos3, inv, qn, kn, gmat, tr=512):
    B, S, _ = proj3.shape
    Hk, HD = NSA_KV_HEADS, NSA_HD
    hm = lambda w, dt: jax.ShapeDtypeStruct((B, Hk, S, w), dt)
    hspec = lambda w: pl.BlockSpec((1, Hk, tr, w), lambda b, r: (b, 0, r, 0))
    tspec = lambda c, rows=HD: pl.BlockSpec((1, Hk, tr // c, rows, c), lambda b, r: (b, 0, r, 0, 0))
    return pl.pallas_call(
        _nsa_prep_kernel,
        grid=(B, S // tr),
        in_specs=[
            pl.BlockSpec((1, tr, 1024), lambda b, r: (b, r, C_NQ // 1024)),
            pl.BlockSpec((1, tr, 512), lambda b, r: (b, r, C_NKV // 512)),
            pl.BlockSpec((1, tr, 512), lambda b, r: (b, r, C_NKV // 512 + 1)),
            pl.BlockSpec((1, tr, 512), lambda b, r: (b, r, C_NKV // 512 + 2)),
            pl.BlockSpec((1, tr, 1), lambda b, r: (b, r, 0)),
            pl.BlockSpec((1, 128), lambda b, r: (0, 0)),
            pl.BlockSpec((1, 256), lambda b, r: (0, 0)),
            pl.BlockSpec((3, 256), lambda b, r: (0, 0)),
            pl.BlockSpec((256, 256), lambda b, r: (0, 0)),
        ],
        out_specs=[pl.BlockSpec((1, tr, 1024), lambda b, r: (b, r, 0)), hspec(HD), hspec(HD), hspec(KAUG),
                   tspec(SLC_CHUNK, VT_ROWS), hspec(HD), tspec(WIN_ALIGN)],
        out_shape=[jax.ShapeDtypeStruct((B, S, 1024), BF16), hm(HD, F32), hm(HD, F32), hm(KAUG, BF16),
                   jax.ShapeDtypeStruct((B, Hk, S // SLC_CHUNK, VT_ROWS, SLC_CHUNK), BF16), hm(HD, BF16),
                   jax.ShapeDtypeStruct((B, Hk, S // WIN_ALIGN, HD, WIN_ALIGN), BF16)],
        compiler_params=_cp(("parallel", "parallel")),
        name="nsa_prep",
    )(proj3, proj3, proj3, proj3, pos3, inv, qn, kn, gmat)


def _compress_kernel(t_ref, pe_ref, w1a_ref, w1b_ref, w2_ref, o_ref, *, transposed):
    t = t_ref[0, 0].astype(BF16)
    w1a, w1b = w1a_ref[...], w1b_ref[...]
    u = jnp.dot(t, w1a, preferred_element_type=F32)
    v = jnp.dot(t, w1b, preferred_element_type=F32)
    pe = pe_ref[...].astype(BF16)
    c = (jnp.dot(pe[0:8], w1a, preferred_element_type=F32) + jnp.dot(pe[8:16], w1b, preferred_element_type=F32))[0:1]
    n = v.shape[0]
    h = u + pltpu.roll(v, n - 1, 0) + c
    h = jax.nn.gelu(h).astype(BF16)
    if transposed:
        o_ref[0, 0] = lax.dot_general(w2_ref[...], h, NT, preferred_element_type=F32).astype(o_ref.dtype)
    else:
        o_ref[0, 0] = jnp.dot(h, w2_ref[...], preferred_element_type=F32).astype(o_ref.dtype)


def _compress(tok, pe2, w1a, w1b, w2, transposed):
    B, Hk, S, HD = tok.shape
    n = S // CMP_STRIDE
    t2 = tok.reshape(B, Hk, n, CMP_STRIDE * HD)
    oshape = (HD, n) if transposed else (n, HD)
    w2 = w2.T if transposed else w2
    return pl.pallas_call(
        functools.partial(_compress_kernel, transposed=transposed),
        grid=(B, Hk),
        in_specs=[
            pl.BlockSpec((1, 1, n, CMP_STRIDE * HD), lambda b, h: (b, h, 0, 0)),
            pl.BlockSpec((16, CMP_STRIDE * HD), lambda b, h: (0, 0)),
            pl.BlockSpec((CMP_STRIDE * HD, CMP_HIDDEN), lambda b, h: (0, 0)),
            pl.BlockSpec((CMP_STRIDE * HD, CMP_HIDDEN), lambda b, h: (0, 0)),
            pl.BlockSpec(w2.shape, lambda b, h: (0, 0)),
        ],
        out_specs=pl.BlockSpec((1, 1) + oshape, lambda b, h: (b, h, 0, 0)),
        out_shape=jax.ShapeDtypeStruct((B, Hk) + oshape, BF16),
        compiler_params=_cp(("parallel", "parallel")),
        name="compress_v" if transposed else "compress_k",
    )(t2, pe2, w1a, w1b, w2)


def _nsa_attn_kernel(q_ref, kc_ref, vct_ref, ks_ref, vst_ref, kw_ref, vwt_ref, gt_ref, gate_ref, ovt_ref, o_ref,
                     sa_ref, sb_ref):
    G, HD, QB = NSA_GROUP, NSA_HD, SLC_BLOCK
    R = G * QB
    hk = pl.program_id(1)
    tile = pl.program_id(2)
    t0 = tile * QB
    qs = q_ref[0]
    q = jnp.concatenate([qs[:, g * HD:(g + 1) * HD] for g in range(G)], axis=0)
    t = t0 + lax.broadcasted_iota(jnp.int32, (1, R), 1) % QB

    nc = kc_ref.shape[2]
    s = lax.dot_general(kc_ref[0, 0], q, NT, preferred_element_type=F32)
    n_id = lax.broadcasted_iota(jnp.int32, (nc, 1), 0)
    valid = (n_id * CMP_STRIDE + (CMP_LEN - 1)) <= t
    sm = jnp.where(valid, s, NEG)
    m = jnp.max(sm, axis=0, keepdims=True)
    e = jnp.where(valid, jnp.exp2(sm - m), 0.0)
    den = jnp.sum(e, axis=0, keepdims=True)
    p = e / jnp.where(den > 0.0, den, 1.0)
    o_cmp = jnp.dot(vct_ref[0, 0], p.astype(BF16), preferred_element_type=F32)

    wc = jnp.maximum(t0 - WINDOW, 0) // WIN_ALIGN
    w0 = pl.multiple_of(wc * WIN_ALIGN, WIN_ALIGN)
    kk = kw_ref[0, 0, pl.ds(w0, WIN_KEYS), :]
    sw = lax.dot_general(kk, q, NT, preferred_element_type=F32)
    kp = w0 + lax.broadcasted_iota(jnp.int32, (WIN_KEYS, 1), 0)
    sw = jnp.where((kp <= t) & (kp > t - WINDOW), sw, NEG)
    mw = jnp.max(sw, axis=0, keepdims=True)
    pw = jnp.exp2(sw - mw)
    vv = jnp.concatenate([vwt_ref[0, 0, wc + j] for j in range(WIN_KEYS // WIN_ALIGN)], axis=1)
    o_win = jnp.dot(vv, pw.astype(BF16), preferred_element_type=F32) / jnp.sum(pw, axis=0, keepdims=True)

    p2 = p[:, :2 * QB] + p[:, 2 * QB:]
    p4 = p2 + pltpu.roll(p2, QB, 1)
    p4h = p4.astype(BF16)
    p4l = (p4 - p4h.astype(F32)).astype(BF16)
    ovt = ovt_ref[...]
    imp = jnp.dot(ovt, p4h, preferred_element_type=F32) + jnp.dot(ovt, p4l, preferred_element_type=F32)
    blk = lax.broadcasted_iota(jnp.int32, (SLC_LANES, 2 * QB), 0)
    cur = tile
    forced = (blk == 0) | (blk == cur) | (blk == cur - 1)
    score = jnp.where(forced, FORCE, jnp.where(blk <= cur, imp, NEG))

    def pick(_, carry):
        sc, sel = carry
        mx = jnp.max(sc, axis=0, keepdims=True)
        first = jnp.min(jnp.where(sc == mx, blk, SLC_LANES), axis=0, keepdims=True)
        hit = blk == first
        return jnp.where(hit, -jnp.inf, sc), jnp.where(hit, 1.0, sel)

    _, sel = lax.fori_loop(0, SLC_TOPN, pick, (score, jnp.zeros((SLC_LANES, 2 * QB), F32)))
    bias_t = jnp.where(sel > 0.5, 0.0, NEG)
    bias = bias_t.T.astype(BF16)
    q_aug = jnp.concatenate([jnp.concatenate([bias, bias], axis=0), q], axis=1)

    kpos = lax.broadcasted_iota(jnp.int32, (SLC_CHUNK, 1), 0)

    def scores(c, dst_ref):
        k0 = pl.multiple_of(c * SLC_CHUNK, SLC_CHUNK)
        dst_ref[...] = lax.dot_general(ks_ref[0, 0, pl.ds(k0, SLC_CHUNK), :], q_aug, NT, preferred_element_type=F32)

    def absorb(c, src_ref, carry, masked):
        m_i, acc = carry
        sc = src_ref[...]
        if masked:
            sc = jnp.where((kpos + c * SLC_CHUNK) <= t, sc, NEG)
        m_new = jnp.maximum(m_i, jnp.max(sc, axis=0, keepdims=True))
        pp = jnp.exp2(sc - m_new).astype(BF16)
        acc = jnp.exp2(m_i - m_new) * acc + jnp.dot(vst_ref[0, 0, c], pp, preferred_element_type=F32)
        return m_new, acc

    diag = t0 // SLC_CHUNK
    n_pairs = diag // 2
    scores(0, sa_ref)

    def pair(j, carry):
        scores(2 * j + 1, sb_ref)
        carry = absorb(2 * j, sa_ref, carry, False)
        scores(2 * j + 2, sa_ref)
        return absorb(2 * j + 1, sb_ref, carry, False)

    init = (jnp.full((1, R), NEG, F32), jnp.zeros((VT_ROWS, R), F32))
    carry = lax.fori_loop(0, n_pairs, pair, init)
    scores(2 * n_pairs + 1, sb_ref)
    carry = absorb(2 * n_pairs, sa_ref, carry, True)
    _, acc_s = absorb(2 * n_pairs + 1, sb_ref, carry, True)
    o_slc = acc_s[:HD] / acc_s[HD:HD + 1]

    o_cmp, o_slc, o_win = o_cmp.T, o_slc.T, o_win.T
    sig = _sigmoid(gt_ref[0])
    glane = lax.broadcasted_iota(jnp.int32, (QB, 128), 1)
    outs = []
    for g in range(G):
        base = (hk * G + g) * N_NSA_BRANCH
        gc, gs, gw = [jnp.sum(jnp.where(glane == base + b, sig, 0.0), axis=-1, keepdims=True) for b in range(3)]
        r = slice(g * QB, (g + 1) * QB)
        outs.append(gc * o_cmp[r] + gs * o_slc[r] + gw * o_win[r])
    o = jnp.concatenate(outs, axis=1)
    o_ref[0] = (o * _silu(gate_ref[0])).astype(o_ref.dtype)


def _nsa_attn(qr, kc, vct, ks, vst, kw, vwt, proj3, ovt):
    B, S, _ = qr.shape
    Hk, HD, QB = NSA_KV_HEADS, NSA_HD, SLC_BLOCK
    nc = kc.shape[2]
    W = NSA_GROUP * HD
    full = lambda a: pl.BlockSpec((1, 1) + a.shape[2:], lambda b, h, t: (b, h) + (0,) * (a.ndim - 2))
    return pl.pallas_call(
        _nsa_attn_kernel,
        grid=(B, Hk, S // QB),
        in_specs=[
            pl.BlockSpec((1, QB, W), lambda b, h, t: (b, t, h)),
            full(kc), full(vct), full(ks), full(vst), full(kw), full(vwt),
            pl.BlockSpec((1, QB, 128), lambda b, h, t: (b, t, C_NSG // 128)),
            pl.BlockSpec((1, QB, W), lambda b, h, t: (b, t, C_NGATE // W + h)),
            pl.BlockSpec((SLC_LANES, nc), lambda b, h, t: (0, 0)),
        ],
        out_specs=pl.BlockSpec((1, QB, W), lambda b, h, t: (b, t, h)),
        out_shape=jax.ShapeDtypeStruct((B, S, NSA_HEADS * HD), BF16),
        scratch_shapes=[pltpu.VMEM((SLC_CHUNK, NSA_GROUP * QB), F32)] * 2,
        compiler_params=_cp(("parallel", "parallel", "arbitrary")),
        name="nsa_attn",
    )(qr, kc, vct, ks, vst, kw, vwt, proj3, proj3, ovt)


def _head_rms(x, gain, hd):
    outs = []
    for h in range(x.shape[-1] // hd):
        xh = x[:, h * hd:(h + 1) * hd]
        ms = jnp.mean(xh * xh, axis=-1, keepdims=True)
        outs.append(xh * lax.rsqrt(ms + EPS) * gain)
    return jnp.concatenate(outs, axis=1)


def _mem_prep_kernel(mem_ref, g_ref, w_ref, kn_ref, mk_ref, mv_ref):
    x = mem_ref[0]
    ms = jnp.mean(x * x, axis=-1, keepdims=True)
    xn = (x * lax.rsqrt(ms + EPS) * g_ref[...]).astype(BF16)
    kv = jnp.dot(xn, w_ref[...], preferred_element_type=F32)
    W = MEM_HEADS * MEM_HD
    mk_ref[0] = _head_rms(kv[:, :W], kn_ref[...], MEM_HD).astype(mk_ref.dtype)
    mv_ref[0] = kv[:, W:].astype(mv_ref.dtype)


def _mem_prep(mem, gain, w_kv, kn):
    B, N, D = mem.shape
    W = MEM_HEADS * MEM_HD
    return pl.pallas_call(
        _mem_prep_kernel,
        grid=(B,),
        in_specs=[
            pl.BlockSpec((1, N, D), lambda b: (b, 0, 0)),
            pl.BlockSpec((1, D), lambda b: (0, 0)),
            pl.BlockSpec((D, 2 * W), lambda b: (0, 0)),
            pl.BlockSpec((1, MEM_HD), lambda b: (0, 0)),
        ],
        out_specs=[pl.BlockSpec((1, N, W), lambda b: (b, 0, 0))] * 2,
        out_shape=[jax.ShapeDtypeStruct((B, N, W), BF16)] * 2,
        compiler_params=_cp(("parallel",)),
        name="mem_prep",
    )(mem, gain, w_kv, kn)


def _final_kernel(x_ref, ya_ref, yb_ref, mq_ref, mg_ref, mr_ref, mk_ref, mv_ref, qn_ref, wb_ref, wo_ref, o_ref):
    mq = _head_rms(mq_ref[...], qn_ref[...], MEM_HD)
    mk = mk_ref[0]
    mv = mv_ref[0]
    heads = []
    for h in range(MEM_HEADS):
        sl = slice(h * MEM_HD, (h + 1) * MEM_HD)
        s = lax.dot_general(mq[:, sl].astype(BF16), mk[:, sl], NT, preferred_element_type=F32) * (MEM_HD ** -0.5)
        m = jnp.max(s, axis=-1, keepdims=True)
        p = jnp.exp(s - m)
        o = jnp.dot(p.astype(BF16), mv[:, sl], preferred_element_type=F32) / jnp.sum(p, axis=-1, keepdims=True)
        heads.append(o)
    ym = jnp.concatenate(heads, axis=1) * _silu(mg_ref[...])

    mixed = None
    for c, y in enumerate((ya_ref[...], yb_ref[...], ym.astype(BF16))):
        z = jnp.dot(y, wb_ref[c], preferred_element_type=F32)
        term = _sigmoid(mr_ref[:, c * D_MODEL:(c + 1) * D_MODEL]) * z
        mixed = term if mixed is None else mixed + term
    o_ref[...] = x_ref[...] + jnp.dot(mixed.astype(BF16), wo_ref[...], preferred_element_type=F32)


def _final(x2, ya2, yb2, proj2, mk, mv, qn, wb, wo, S, tr=256):
    M, D = x2.shape
    N = mk.shape[1]
    nb = S // tr
    row = lambda c: pl.BlockSpec((tr, D), lambda i: (i, c))
    return pl.pallas_call(
        _final_kernel,
        grid=(M // tr,),
        in_specs=[
            row(0), row(0), row(0),
            row(C_MQ // D), row(C_MG // D),
            pl.BlockSpec((tr, 3 * D), lambda i: (i, C_MERGE // (3 * D))),
            pl.BlockSpec((1, N, D), lambda i: (i // nb, 0, 0)),
            pl.BlockSpec((1, N, D), lambda i: (i // nb, 0, 0)),
            pl.BlockSpec((1, MEM_HD), lambda i: (0, 0)),
            pl.BlockSpec((3, D, D), lambda i: (0, 0, 0)),
            pl.BlockSpec((D, D), lambda i: (0, 0)),
        ],
        out_specs=pl.BlockSpec((tr, D), lambda i: (i, 0)),
        out_shape=jax.ShapeDtypeStruct((M, D), F32),
        compiler_params=_cp(("parallel",)),
        name="final",
    )(x2, ya2, yb2, proj2, proj2, proj2, mk, mv, qn, wb, wo)


def _overlap_matrix_t(S):
    n_cmp = (S - CMP_LEN) // CMP_STRIDE + 1
    n_slc = S // SLC_BLOCK
    cs = np.arange(n_cmp)[:, None] * CMP_STRIDE
    ss = np.arange(n_slc)[None, :] * SLC_BLOCK
    ov = np.clip(np.minimum(cs + CMP_LEN, ss + SLC_BLOCK) - np.maximum(cs, ss), 0, None) / CMP_LEN
    out = np.zeros((SLC_LANES, S // CMP_STRIDE), np.float32)
    out[:n_slc, :n_cmp] = ov.T
    return jnp.asarray(out, BF16)


def _pad_cols(w, n):
    return jnp.pad(w, ((0, 0), (0, n - w.shape[1])))


def _layer(x, mem, positions, norm_gain, mem_norm_gain, w_in, w_gla_alpha, b_gla_alpha, gla_out_norm,
           nsa_q_norm, nsa_k_norm, pe_cmp_k, pe_cmp_v, w_cmp_k1, w_cmp_k2, w_cmp_v1, w_cmp_v2,
           w_mem_kv, mem_q_norm, mem_k_norm, w_branch, w_out):
    B, S, D = x.shape
    assert D == D_MODEL and S % (2 * SLC_CHUNK) == 0 and S >= WIN_KEYS and S // SLC_BLOCK <= SLC_LANES

    o = np.cumsum([0, 512, 512, 1024, 16, 1024, 1024, 1536, 48, 1024, 1024, 1024, 3072])
    sec = [w_in[:, o[i]:o[i + 1]] for i in range(12)]
    gq, gk, gv, glr, gg, nq, nkv, nsg, ngate, mq, mg, merge = sec
    w_all = jnp.concatenate(
        [gv, gg, nq, ngate, mq, mg, merge, nkv, gq, gk, _pad_cols(glr, 128), _pad_cols(nsg, 128),
         jnp.zeros((D, NP - C_NSG - 128), w_in.dtype)], axis=1).astype(BF16)

    x2 = x.reshape(B * S, D)
    proj2 = _proj(x2, norm_gain.reshape(1, D), w_all)
    proj3 = proj2.reshape(B, S, NP)

    wa = jnp.pad(w_gla_alpha, ((0, 128 - GLA_RANK), (0, 0)))
    ya = _gla(proj3, wa, b_gla_alpha.reshape(1, -1), gla_out_norm.reshape(1, -1))

    half = NSA_HD // 2
    inv = ROPE_THETA ** (-jnp.arange(half, dtype=F32) / half)
    inv = jnp.tile(inv, 128 // half).reshape(1, 128)
    qn = jnp.tile(nsa_q_norm, NSA_KV_HEADS).reshape(1, -1)
    kn = jnp.tile(nsa_k_norm, (1, NSA_KV_HEADS))
    gid = np.arange(256) // NSA_HD
    gmat = jnp.asarray(gid[:, None] == gid[None, :], BF16)
    qr, kc_tok, vc_tok, ks, vst, kw, vwt = _nsa_prep(proj3, positions.reshape(B, S, 1), inv, qn, kn, gmat)

    def cmp_args(pe, w1, w2):
        pe2 = pe.reshape(2, CMP_STRIDE * NSA_HD)
        pe16 = jnp.concatenate([jnp.broadcast_to(pe2[0:1], (8, pe2.shape[1])), jnp.broadcast_to(pe2[1:2], (8, pe2.shape[1]))], 0)
        w1f = w1.reshape(CMP_LEN * NSA_HD, CMP_HIDDEN).astype(BF16)
        return pe16, w1f[:CMP_STRIDE * NSA_HD], w1f[CMP_STRIDE * NSA_HD:], w2.astype(BF16)

    kc = _compress(kc_tok, *cmp_args(pe_cmp_k, w_cmp_k1, w_cmp_k2), transposed=False)
    vct = _compress(vc_tok, *cmp_args(pe_cmp_v, w_cmp_v1, w_cmp_v2), transposed=True)
    yb = _nsa_attn(qr, kc, vct, ks, vst, kw, vwt, proj3, _overlap_matrix_t(S))

    mk, mv = _mem_prep(mem, mem_norm_gain.reshape(1, D), w_mem_kv.astype(BF16), mem_k_norm.reshape(1, -1))
    out = _final(x2, ya.reshape(B * S, D), yb.reshape(B * S, D), proj2, mk, mv, mem_q_norm.reshape(1, -1),
                 w_branch.astype(BF16), w_out.astype(BF16), S)
    return out.reshape(B, S, D)


def kernel(x, mem, positions, norm_gain, mem_norm_gain, w_in, w_gla_alpha, b_gla_alpha, gla_out_norm, nsa_q_norm, nsa_k_norm, pe_cmp_k, pe_cmp_v, w_cmp_k1, w_cmp_k2, w_cmp_v1, w_cmp_v2, w_mem_kv, mem_q_norm, mem_k_norm, w_branch, w_out):
    h = x
    for l in range(norm_gain.shape[0]):
        h = _layer(h, mem, positions, norm_gain[l], mem_norm_gain[l], w_in[l], w_gla_alpha[l], b_gla_alpha[l],
                   gla_out_norm[l], nsa_q_norm[l], nsa_k_norm[l], pe_cmp_k[l], pe_cmp_v[l], w_cmp_k1[l],
                   w_cmp_k2[l], w_cmp_v1[l], w_cmp_v2[l], w_mem_kv[l], mem_q_norm[l], mem_k_norm[l],
                   w_branch[l], w_out[l])
    return h
```

```python
import functools

import numpy as np
import jax
import jax.numpy as jnp
from jax import lax
from jax.experimental import pallas as pl
from jax.experimental.pallas import tpu as pltpu

F32 = jnp.float32
BF16 = jnp.bfloat16
HIGHEST = lax.Precision.HIGHEST

D_MODEL = 1024
ROPE_THETA = 10000.0
EPS = 1e-6
NEG = -1e30
FORCE = 1e9

GLA_HEADS = 4
GLA_DK = 128
GLA_DV = 256
GLA_RANK = 16
GLA_TAU = 16.0
GLA_CHUNK = 64
GLA_SUB = 16

NSA_HEADS = 16
NSA_KV_HEADS = 4
NSA_GROUP = 4
NSA_HD = 64
CMP_LEN = 32
CMP_SHIFT = 4
CMP_STRIDE = 1 << CMP_SHIFT
CMP_HIDDEN = 256
SLC_BLOCK = 64
SLC_TOPN = 16
WINDOW = 512
N_NSA_BRANCH = 3
SLC_LANES = 128
SLC_CHUNK = 512
VT_ROWS = NSA_HD + 16
LOG2E = 1.4426950408889634
NSA_STEP_HEADS = 2
NSA_STEP_BLOCKS = 2
WIN_ALIGN = 128
WIN_KEYS = 640
KAUG = SLC_LANES + NSA_HD

MEM_HEADS = 4
MEM_HD = 256

C_GV, C_GG, C_NQ, C_NGATE, C_MQ, C_MG, C_MERGE = 0, 1024, 2048, 3072, 4096, 5120, 6144
C_NKV, C_GQ, C_GK, C_LR, C_NSG = 9216, 10752, 11264, 11776, 11904
NP = 12288

VMEM_LIMIT = 48 * 1024 * 1024

NT = (((1,), (1,)), ((), ()))
TN = (((0,), (0,)), ((), ()))


def _cp(sem):
    return pltpu.CompilerParams(dimension_semantics=sem, vmem_limit_bytes=VMEM_LIMIT)


def _silu(x):
    return x * (1.0 / (1.0 + jnp.exp(-x)))


def _sigmoid(x):
    return 1.0 / (1.0 + jnp.exp(-x))


def _proj_kernel(x_ref, g_ref, w_ref, o_ref, xn_ref):
    @pl.when(pl.program_id(1) == 0)
    def _():
        x = x_ref[...]
        ms = jnp.mean(x * x, axis=-1, keepdims=True)
        xn_ref[...] = (x * lax.rsqrt(ms + EPS) * g_ref[...]).astype(BF16)

    o_ref[...] = jnp.dot(xn_ref[...], w_ref[...], preferred_element_type=F32)


def _proj(x2, gain, w_all, tm=512, tn=1024):
    M = x2.shape[0]
    return pl.pallas_call(
        _proj_kernel,
        grid=(M // tm, NP // tn),
        in_specs=[
            pl.BlockSpec((tm, D_MODEL), lambda i, j: (i, 0)),
            pl.BlockSpec((1, D_MODEL), lambda i, j: (0, 0)),
            pl.BlockSpec((D_MODEL, tn), lambda i, j: (0, j)),
        ],
        out_specs=pl.BlockSpec((tm, tn), lambda i, j: (i, j)),
        out_shape=jax.ShapeDtypeStruct((M, NP), F32),
        scratch_shapes=[pltpu.VMEM((tm, D_MODEL), BF16)],
        compiler_params=_cp(("parallel", "arbitrary")),
        name="proj",
    )(x2, gain, w_all)


def _gla_kernel(q_ref, k_ref, v_ref, gate_ref, lr_ref, wa_ref, ba_ref, gn_ref, o_ref, st_ref, g_ref, *, n_chunks):
    C, SB, H, DK, DV = GLA_CHUNK, GLA_SUB, GLA_HEADS, GLA_DK, GLA_DV

    @pl.when(pl.program_id(1) == 0)
    def _():
        st_ref[...] = jnp.zeros_like(st_ref)

    gn = gn_ref[...]
    ri = lax.broadcasted_iota(jnp.int32, (C, C), 0)
    ci = lax.broadcasted_iota(jnp.int32, (C, C), 1)
    tri = (ri >= ci).astype(F32)
    si = lax.broadcasted_iota(jnp.int32, (SB, SB), 0)
    sj = lax.broadcasted_iota(jnp.int32, (SB, SB), 1)
    sub_causal = si >= sj

    z = jnp.dot(lr_ref[0], wa_ref[...], precision=HIGHEST, preferred_element_type=F32) + ba_ref[...]
    la = -(jnp.maximum(-z, 0.0) + jnp.log(1.0 + jnp.exp(-jnp.abs(z)))) * (1.0 / GLA_TAU)
    for c in range(n_chunks):
        g_ref[c * C:(c + 1) * C, :] = jnp.dot(tri, la[c * C:(c + 1) * C], precision=HIGHEST, preferred_element_type=F32)

    def chunk(c, carry):
        rows = pl.ds(pl.multiple_of(c * C, C), C)
        for h in range(H):
            kl = slice(h * DK, (h + 1) * DK)
            vl = slice(h * DV, (h + 1) * DV)
            q = q_ref[0, rows, kl] * (DK ** -0.5)
            k = k_ref[0, rows, kl]
            vb = v_ref[0, rows, vl].astype(BF16)
            g = g_ref[rows, kl]

            outs = []
            for i in range(C // SB):
                lo = i * SB
                gi = g[lo:lo + SB]
                qi = q[lo:lo + SB]
                ki = k[lo:lo + SB]
                d = gi[:, None, :] - gi[None, :, :]
                e = jnp.exp(jnp.minimum(d, 0.0))
                a_d = jnp.sum(qi[:, None, :] * ki[None, :, :] * e, axis=-1)
                a_d = jnp.where(sub_causal, a_d, 0.0)
                o_i = jnp.dot(a_d.astype(BF16), vb[lo:lo + SB], preferred_element_type=F32)
                if i > 0:
                    r = g[lo:lo + 1]
                    qt = qi * jnp.exp(gi - r)
                    kt = k[:lo] * jnp.exp(r - g[:lo])
                    a_o = lax.dot_general(qt.astype(BF16), kt.astype(BF16), NT, preferred_element_type=F32)
                    o_i = o_i + jnp.dot(a_o.astype(BF16), vb[:lo], preferred_element_type=F32)
                outs.append(o_i)
            o = jnp.concatenate(outs, axis=0)

            st = st_ref[h]
            qg = q * jnp.exp(g)
            o = o + lax.dot_general(qg.astype(BF16), st.astype(BF16), NT, preferred_element_type=F32)
            gl = g[C - 1:C]
            kd = k * jnp.exp(gl - g)
            st_ref[h] = st * jnp.exp(gl) + lax.dot_general(vb, kd.astype(BF16), TN, preferred_element_type=F32)

            ms = jnp.mean(o * o, axis=-1, keepdims=True)
            y = o * lax.rsqrt(ms + EPS) * gn
            o_ref[0, rows, vl] = (y * _silu(gate_ref[0, rows, vl])).astype(o_ref.dtype)
        return carry

    lax.fori_loop(0, n_chunks, chunk, 0)


def _gla(proj3, wa, ba, gn, tr=512):
    B, S, _ = proj3.shape
    H, DK, DV = GLA_HEADS, GLA_DK, GLA_DV
    kern = functools.partial(_gla_kernel, n_chunks=tr // GLA_CHUNK)
    return pl.pallas_call(
        kern,
        grid=(B, S // tr),
        in_specs=[
            pl.BlockSpec((1, tr, H * DK), lambda b, r: (b, r, C_GQ // (H * DK))),
            pl.BlockSpec((1, tr, H * DK), lambda b, r: (b, r, C_GK // (H * DK))),
            pl.BlockSpec((1, tr, H * DV), lambda b, r: (b, r, C_GV // (H * DV))),
            pl.BlockSpec((1, tr, H * DV), lambda b, r: (b, r, C_GG // (H * DV))),
            pl.BlockSpec((1, tr, 128), lambda b, r: (b, r, C_LR // 128)),
            pl.BlockSpec((128, H * DK), lambda b, r: (0, 0)),
            pl.BlockSpec((1, H * DK), lambda b, r: (0, 0)),
            pl.BlockSpec((1, DV), lambda b, r: (0, 0)),
        ],
        out_specs=pl.BlockSpec((1, tr, H * DV), lambda b, r: (b, r, 0)),
        out_shape=jax.ShapeDtypeStruct((B, S, H * DV), BF16),
        scratch_shapes=[pltpu.VMEM((H, DV, DK), F32), pltpu.VMEM((tr, H * DK), F32)],
        compiler_params=_cp(("parallel", "arbitrary")),
        name="gla",
    )(proj3, proj3, proj3, proj3, proj3, wa, ba, gn)


def _group_meansq(x, gmat):
    sq = x * x
    hi = sq.astype(BF16)
    lo = (sq - hi.astype(F32)).astype(BF16)
    s = jnp.dot(hi, gmat, preferred_element_type=F32) + jnp.dot(lo, gmat, preferred_element_type=F32)
    return s * (1.0 / NSA_HD)


def _rope(x, cos, sin, first_half):
    w = x.shape[-1]
    rot = jnp.where(first_half, -pltpu.roll(x, w - NSA_HD // 2, 1), pltpu.roll(x, NSA_HD // 2, 1))
    return x * cos + rot * sin


def _nsa_prep_kernel(q_ref, kvc_ref, kvs_ref, kvw_ref, pos_ref, inv_ref, qn_ref, kn_ref, gm_ref,
                     qr_ref, kc_ref, vc_ref, ks_ref, vst_ref, kw_ref, vwt_ref):
    tr = q_ref.shape[1]
    W = NSA_KV_HEADS * NSA_HD
    HD = NSA_HD
    gmat = gm_ref[...]
    ang = pos_ref[0].astype(F32) * inv_ref[...]
    cos1, sin1 = jnp.cos(ang), jnp.sin(ang)
    cos = jnp.concatenate([cos1, cos1], axis=1)
    sin = jnp.concatenate([sin1, sin1], axis=1)
    lane = lax.broadcasted_iota(jnp.int32, (tr, W), 1)
    first_half = (lane % HD) < (HD // 2)

    def norm_rope(x, gain):
        y = x * lax.rsqrt(_group_meansq(x, gmat) + EPS) * gain
        return _rope(y, cos, sin, first_half)

    qn = qn_ref[...]
    for s in range(NSA_HEADS * HD // W):
        xq = q_ref[0, :, s * W:(s + 1) * W]
        qr_ref[0, :, s * W:(s + 1) * W] = (norm_rope(xq, qn) * (HD ** -0.5 * LOG2E)).astype(qr_ref.dtype)

    kc = norm_rope(kvc_ref[0, :, :W], kn_ref[0:1, :])
    vc = kvc_ref[0, :, W:]
    for h in range(NSA_KV_HEADS):
        kc_ref[0, h] = kc[:, h * HD:(h + 1) * HD]
        vc_ref[0, h] = vc[:, h * HD:(h + 1) * HD]

    ks = norm_rope(kvs_ref[0, :, :W], kn_ref[1:2, :]).astype(BF16)
    row_blk = (pl.program_id(1) * tr + lax.broadcasted_iota(jnp.int32, (tr, SLC_LANES), 0)) // SLC_BLOCK
    onehot = (row_blk == lax.broadcasted_iota(jnp.int32, (tr, SLC_LANES), 1)).astype(BF16)
    vst = kvs_ref[0, :, W:].T
    for h in range(NSA_KV_HEADS):
        ks_ref[0, h] = jnp.concatenate([onehot, ks[:, h * HD:(h + 1) * HD]], axis=1)
        for j in range(tr // SLC_CHUNK):
            vst_ref[0, h, j, :HD] = vst[h * HD:(h + 1) * HD, j * SLC_CHUNK:(j + 1) * SLC_CHUNK].astype(BF16)
            vst_ref[0, h, j, HD:] = jnp.ones((VT_ROWS - HD, SLC_CHUNK), BF16)

    kw = norm_rope(kvw_ref[0, :, :W], kn_ref[2:3, :]).astype(BF16)
    vwt = kvw_ref[0, :, W:].T
    for h in range(NSA_KV_HEADS):
        kw_ref[0, h] = kw[:, h * HD:(h + 1) * HD]
        for j in range(tr // WIN_ALIGN):
            vwt_ref[0, h, j, :HD] = vwt[h * HD:(h + 1) * HD, j * WIN_ALIGN:(j + 1) * WIN_ALIGN].astype(BF16)
            vwt_ref[0, h, j, HD:] = jnp.ones((VT_ROWS - HD, WIN_ALIGN), BF16)


def _nsa_prep(proj3, pos3, inv, qn, kn, gmat, tr=512):
    B, S, _ = proj3.shape
    Hk, HD = NSA_KV_HEADS, NSA_HD
    hm = lambda w, dt: jax.ShapeDtypeStruct((B, Hk, S, w), dt)
    hspec = lambda w: pl.BlockSpec((1, Hk, tr, w), lambda b, r: (b, 0, r, 0))
    tspec = lambda c, rows=HD: pl.BlockSpec((1, Hk, tr // c, rows, c), lambda b, r: (b, 0, r, 0, 0))
    return pl.pallas_call(
        _nsa_prep_kernel,
        grid=(B, S // tr),
        in_specs=[
            pl.BlockSpec((1, tr, 1024), lambda b, r: (b, r, C_NQ // 1024)),
            pl.BlockSpec((1, tr, 512), lambda b, r: (b, r, C_NKV // 512)),
            pl.BlockSpec((1, tr, 512), lambda b, r: (b, r, C_NKV // 512 + 1)),
            pl.BlockSpec((1, tr, 512), lambda b, r: (b, r, C_NKV // 512 + 2)),
            pl.BlockSpec((1, tr, 1), lambda b, r: (b, r, 0)),
            pl.BlockSpec((1, 128), lambda b, r: (0, 0)),
            pl.BlockSpec((1, 256), lambda b, r: (0, 0)),
            pl.BlockSpec((3, 256), lambda b, r: (0, 0)),
            pl.BlockSpec((256, 256), lambda b, r: (0, 0)),
        ],
        out_specs=[pl.BlockSpec((1, tr, 1024), lambda b, r: (b, r, 0)), hspec(HD), hspec(HD), hspec(KAUG),
                   tspec(SLC_CHUNK, VT_ROWS), hspec(HD), tspec(WIN_ALIGN, VT_ROWS)],
        out_shape=[jax.ShapeDtypeStruct((B, S, 1024), BF16), hm(HD, F32), hm(HD, F32), hm(KAUG, BF16),
                   jax.ShapeDtypeStruct((B, Hk, S // SLC_CHUNK, VT_ROWS, SLC_CHUNK), BF16), hm(HD, BF16),
                   jax.ShapeDtypeStruct((B, Hk, S // WIN_ALIGN, VT_ROWS, WIN_ALIGN), BF16)],
        compiler_params=_cp(("parallel", "parallel")),
        name="nsa_prep",
    )(proj3, proj3, proj3, proj3, pos3, inv, qn, kn, gmat)


def _compress_kernel(t_ref, pe_ref, w1a_ref, w1b_ref, w2_ref, o_ref, *, transposed):
    t = t_ref[0, 0].astype(BF16)
    w1a, w1b = w1a_ref[...], w1b_ref[...]
    u = jnp.dot(t, w1a, preferred_element_type=F32)
    v = jnp.dot(t, w1b, preferred_element_type=F32)
    pe = pe_ref[...].astype(BF16)
    c = (jnp.dot(pe[0:8], w1a, preferred_element_type=F32) + jnp.dot(pe[8:16], w1b, preferred_element_type=F32))[0:1]
    n = v.shape[0]
    h = u + pltpu.roll(v, n - 1, 0) + c
    h = jax.nn.gelu(h).astype(BF16)
    if transposed:
        hd = w2_ref.shape[0]
        o_ref[0, 0, :hd] = lax.dot_general(w2_ref[...], h, NT, preferred_element_type=F32).astype(o_ref.dtype)
        o_ref[0, 0, hd:] = jnp.ones((o_ref.shape[2] - hd, n), o_ref.dtype)
    else:
        o_ref[0, 0] = jnp.dot(h, w2_ref[...], preferred_element_type=F32).astype(o_ref.dtype)


def _compress(tok, pe2, w1a, w1b, w2, transposed):
    B, Hk, S, HD = tok.shape
    n = S // CMP_STRIDE
    t2 = tok.reshape(B, Hk, n, CMP_STRIDE * HD)
    oshape = (VT_ROWS, n) if transposed else (n, HD)
    w2 = w2.T if transposed else w2
    return pl.pallas_call(
        functools.partial(_compress_kernel, transposed=transposed),
        grid=(B, Hk),
        in_specs=[
            pl.BlockSpec((1, 1, n, CMP_STRIDE * HD), lambda b, h: (b, h, 0, 0)),
            pl.BlockSpec((16, CMP_STRIDE * HD), lambda b, h: (0, 0)),
            pl.BlockSpec((CMP_STRIDE * HD, CMP_HIDDEN), lambda b, h: (0, 0)),
            pl.BlockSpec((CMP_STRIDE * HD, CMP_HIDDEN), lambda b, h: (0, 0)),
            pl.BlockSpec(w2.shape, lambda b, h: (0, 0)),
        ],
        out_specs=pl.BlockSpec((1, 1) + oshape, lambda b, h: (b, h, 0, 0)),
        out_shape=jax.ShapeDtypeStruct((B, Hk) + oshape, BF16),
        compiler_params=_cp(("parallel", "parallel")),
        name="compress_v" if transposed else "compress_k",
    )(t2, pe2, w1a, w1b, w2)


def _nsa_attn_kernel(q_ref, kc_ref, vct_ref, ks_ref, vst_ref, kw_ref, vwt_ref, gt_ref, gate_ref, ovt_ref, o_ref,
                     sa_ref, sb_ref):
    G, HD, QB, NH, NB = NSA_GROUP, NSA_HD, SLC_BLOCK, NSA_STEP_HEADS, NSA_STEP_BLOCKS
    R = NB * G * QB
    heads = range(NH)
    hk0 = pl.program_id(1) * NH
    step = pl.program_id(2)
    t0 = step * (NB * QB)
    lane = lax.broadcasted_iota(jnp.int32, (1, R), 1)
    t = t0 + (lane // (G * QB)) * QB + lane % QB
    q = [jnp.concatenate([q_ref[0, a * QB:(a + 1) * QB, (h * G + g) * HD:(h * G + g + 1) * HD]
                          for a in range(NB) for g in range(G)], axis=0)
         for h in heads]

    nc = kc_ref.shape[2]
    valid = lax.broadcasted_iota(jnp.int32, (nc, 1), 0) <= ((t - (CMP_LEN - 1)) >> CMP_SHIFT)
    p, o_cmp = [], []
    for h in heads:
        s = lax.dot_general(kc_ref[0, h], q[h], NT, preferred_element_type=F32)
        sm = jnp.where(valid, s, NEG)
        m = jnp.maximum(jnp.max(sm, axis=0, keepdims=True), 0.1 * NEG)
        e = jnp.exp2(sm - m)
        oc = jnp.dot(vct_ref[0, h], e.astype(BF16), preferred_element_type=F32)
        den = oc[HD:HD + 1]
        rden = 1.0 / jnp.where(den > 0.0, den, 1.0)
        p.append(e * rden)
        o_cmp.append(oc[:HD] * rden)

    wc = jnp.maximum(t0 - WINDOW, 0) // WIN_ALIGN
    w0 = pl.multiple_of(wc * WIN_ALIGN, WIN_ALIGN)
    kp = w0 + lax.broadcasted_iota(jnp.int32, (WIN_KEYS, 1), 0)
    win_ok = (kp <= t) & (kp > t - WINDOW)
    o_win = []
    for h in heads:
        sw = lax.dot_general(kw_ref[0, h, pl.ds(w0, WIN_KEYS), :], q[h], NT, preferred_element_type=F32)
        sw = jnp.where(win_ok, sw, NEG)
        pw = jnp.exp2(sw - jnp.max(sw, axis=0, keepdims=True))
        vv = jnp.concatenate([vwt_ref[0, h, wc + j] for j in range(WIN_KEYS // WIN_ALIGN)], axis=1)
        ow = jnp.dot(vv, pw.astype(BF16), preferred_element_type=F32)
        o_win.append(ow[:HD] * (1.0 / ow[HD:HD + 1]))

    ovt = ovt_ref[...]
    blk = lax.broadcasted_iota(jnp.int32, (SLC_LANES, 2 * QB), 0)
    qlane = lax.broadcasted_iota(jnp.int32, (1, 2 * QB), 1)
    cur = step * NB + (qlane // QB) % NB
    forced = (blk == 0) | (blk == cur) | (blk == cur - 1)
    score = []
    for h in heads:
        parts = []
        for a in range(NB):
            pa = p[h][:, a * G * QB:(a + 1) * G * QB]
            p2 = pa[:, :2 * QB] + pa[:, 2 * QB:]
            parts.append(p2 + pltpu.roll(p2, QB, 1))
        p4 = parts[0] if NB == 1 else jnp.where(qlane < QB, parts[0], parts[1])
        p4h = p4.astype(BF16)
        p4l = (p4 - p4h.astype(F32)).astype(BF16)
        imp = jnp.dot(ovt, p4h, preferred_element_type=F32) + jnp.dot(ovt, p4l, preferred_element_type=F32)
        score.append(jnp.where(forced, FORCE, jnp.where(blk <= cur, imp, NEG)))

    def pick(_, carry):
        out = []
        for sc, sel in carry:
            mx = jnp.max(sc, axis=0, keepdims=True)
            first = jnp.min(jnp.where(sc == mx, blk, SLC_LANES), axis=0, keepdims=True)
            hit = blk == first
            out.append((jnp.where(hit, -jnp.inf, sc), jnp.where(hit, 1.0, sel)))
        return tuple(out)

    picked = lax.fori_loop(0, SLC_TOPN, pick, tuple((score[h], jnp.zeros((SLC_LANES, 2 * QB), F32)) for h in heads))
    q_aug = []
    for h in heads:
        bias_t = jnp.where(picked[h][1] > 0.5, 0.0, NEG)
        bias = bias_t.T.astype(BF16)
        rows = jnp.concatenate([bias[a * QB:(a + 1) * QB] for a in range(NB) for g in range(G)], axis=0)
        q_aug.append(jnp.concatenate([rows, q[h]], axis=1))

    kpos = lax.broadcasted_iota(jnp.int32, (SLC_CHUNK, 1), 0)

    def scores(c, dst_ref):
        k0 = pl.multiple_of(c * SLC_CHUNK, SLC_CHUNK)
        for h in heads:
            dst_ref[h] = lax.dot_general(ks_ref[0, h, pl.ds(k0, SLC_CHUNK), :], q_aug[h], NT, preferred_element_type=F32)

    def absorb(c, src_ref, carry, masked):
        out = []
        for h in heads:
            m_i, acc = carry[h]
            sc = src_ref[h]
            if masked:
                sc = jnp.where((kpos + c * SLC_CHUNK) <= t, sc, NEG)
            m_new = jnp.maximum(m_i, jnp.max(sc, axis=0, keepdims=True))
            pp = jnp.exp2(sc - m_new).astype(BF16)
            acc = jnp.exp2(m_i - m_new) * acc + jnp.dot(vst_ref[0, h, c], pp, preferred_element_type=F32)
            out.append((m_new, acc))
        return tuple(out)

    diag = t0 // SLC_CHUNK
    n_pairs = diag // 2
    scores(0, sa_ref)

    def pair(j, carry):
        scores(2 * j + 1, sb_ref)
        carry = absorb(2 * j, sa_ref, carry, False)
        scores(2 * j + 2, sa_ref)
        return absorb(2 * j + 1, sb_ref, carry, False)

    init = tuple((jnp.full((1, R), NEG, F32), jnp.zeros((VT_ROWS, R), F32)) for h in heads)
    carry = lax.fori_loop(0, n_pairs, pair, init)
    scores(2 * n_pairs + 1, sb_ref)
    carry = absorb(2 * n_pairs, sa_ref, carry, True)
    carry = absorb(2 * n_pairs + 1, sb_ref, carry, True)

    sig = _sigmoid(gt_ref[0])
    glane = lax.broadcasted_iota(jnp.int32, (QB, 128), 1)
    outs = [[] for a in range(NB)]
    for h in heads:
        acc_s = carry[h][1]
        oc, os_, ow = o_cmp[h].T, (acc_s[:HD] * (1.0 / acc_s[HD:HD + 1])).T, o_win[h].T
        for a in range(NB):
            sig_a = sig[a * QB:(a + 1) * QB]
            for g in range(G):
                base = ((hk0 + h) * G + g) * N_NSA_BRANCH
                gc, gs, gw = [jnp.sum(jnp.where(glane == base + b, sig_a, 0.0), axis=-1, keepdims=True) for b in range(3)]
                r = slice((a * G + g) * QB, (a * G + g + 1) * QB)
                outs[a].append(gc * oc[r] + gs * os_[r] + gw * ow[r])
    o = jnp.concatenate([jnp.concatenate(oa, axis=1) for oa in outs], axis=0)
    o_ref[0] = (o * _silu(gate_ref[0])).astype(o_ref.dtype)


def _nsa_attn(qr, kc, vct, ks, vst, kw, vwt, proj3, ovt):
    B, S, _ = qr.shape
    Hk, HD, NH = NSA_KV_HEADS, NSA_HD, NSA_STEP_HEADS
    TQ = NSA_STEP_BLOCKS * SLC_BLOCK
    nc = kc.shape[2]
    W = NH * NSA_GROUP * HD
    full = lambda a: pl.BlockSpec((1, NH) + a.shape[2:], lambda b, h, t: (b, h) + (0,) * (a.ndim - 2),
                                  pipeline_mode=pl.Buffered(1))
    return pl.pallas_call(
        _nsa_attn_kernel,
        grid=(B, Hk // NH, S // TQ),
        in_specs=[
            pl.BlockSpec((1, TQ, W), lambda b, h, t: (b, t, h)),
            full(kc), full(vct), full(ks), full(vst), full(kw), full(vwt),
            pl.BlockSpec((1, TQ, 128), lambda b, h, t: (b, t, C_NSG // 128)),
            pl.BlockSpec((1, TQ, W), lambda b, h, t: (b, t, C_NGATE // W + h)),
            pl.BlockSpec((SLC_LANES, nc), lambda b, h, t: (0, 0)),
        ],
        out_specs=pl.BlockSpec((1, TQ, W), lambda b, h, t: (b, t, h)),
        out_shape=jax.ShapeDtypeStruct((B, S, NSA_HEADS * HD), BF16),
        scratch_shapes=[pltpu.VMEM((NH, SLC_CHUNK, NSA_GROUP * TQ), F32)] * 2,
        compiler_params=_cp(("parallel", "parallel", "arbitrary")),
        name="nsa_attn",
    )(qr, kc, vct, ks, vst, kw, vwt, proj3, proj3, ovt)


def _head_rms(x, gain, hd):
    outs = []
    for h in range(x.shape[-1] // hd):
        xh = x[:, h * hd:(h + 1) * hd]
        ms = jnp.mean(xh * xh, axis=-1, keepdims=True)
        outs.append(xh * lax.rsqrt(ms + EPS) * gain)
    return jnp.concatenate(outs, axis=1)


def _mem_prep_kernel(mem_ref, g_ref, w_ref, kn_ref, mk_ref, mv_ref):
    x = mem_ref[0]
    ms = jnp.mean(x * x, axis=-1, keepdims=True)
    xn = (x * lax.rsqrt(ms + EPS) * g_ref[...]).astype(BF16)
    kv = jnp.dot(xn, w_ref[...], preferred_element_type=F32)
    W = MEM_HEADS * MEM_HD
    mk_ref[0] = _head_rms(kv[:, :W], kn_ref[...], MEM_HD).astype(mk_ref.dtype)
    mv_ref[0] = kv[:, W:].astype(mv_ref.dtype)


def _mem_prep(mem, gain, w_kv, kn):
    B, N, D = mem.shape
    W = MEM_HEADS * MEM_HD
    return pl.pallas_call(
        _mem_prep_kernel,
        grid=(B,),
        in_specs=[
            pl.BlockSpec((1, N, D), lambda b: (b, 0, 0)),
            pl.BlockSpec((1, D), lambda b: (0, 0)),
            pl.BlockSpec((D, 2 * W), lambda b: (0, 0)),
            pl.BlockSpec((1, MEM_HD), lambda b: (0, 0)),
        ],
        out_specs=[pl.BlockSpec((1, N, W), lambda b: (b, 0, 0))] * 2,
        out_shape=[jax.ShapeDtypeStruct((B, N, W), BF16)] * 2,
        compiler_params=_cp(("parallel",)),
        name="mem_prep",
    )(mem, gain, w_kv, kn)


def _final_kernel(x_ref, ya_ref, yb_ref, mq_ref, mg_ref, mr_ref, mk_ref, mv_ref, qn_ref, wb_ref, wo_ref, o_ref):
    mq = _head_rms(mq_ref[...], qn_ref[...], MEM_HD)
    mk = mk_ref[0]
    mv = mv_ref[0]
    heads = []
    for h in range(MEM_HEADS):
        sl = slice(h * MEM_HD, (h + 1) * MEM_HD)
        s = lax.dot_general(mq[:, sl].astype(BF16), mk[:, sl], NT, preferred_element_type=F32) * (MEM_HD ** -0.5)
        m = jnp.max(s, axis=-1, keepdims=True)
        p = jnp.exp(s - m)
        o = jnp.dot(p.astype(BF16), mv[:, sl], preferred_element_type=F32) / jnp.sum(p, axis=-1, keepdims=True)
        heads.append(o)
    ym = jnp.concatenate(heads, axis=1) * _silu(mg_ref[...])

    mixed = None
    for c, y in enumerate((ya_ref[...], yb_ref[...], ym.astype(BF16))):
        z = jnp.dot(y, wb_ref[c], preferred_element_type=F32)
        term = _sigmoid(mr_ref[:, c * D_MODEL:(c + 1) * D_MODEL]) * z
        mixed = term if mixed is None else mixed + term
    o_ref[...] = x_ref[...] + jnp.dot(mixed.astype(BF16), wo_ref[...], preferred_element_type=F32)


def _final(x2, ya2, yb2, proj2, mk, mv, qn, wb, wo, S, tr=256):
    M, D = x2.shape
    N = mk.shape[1]
    nb = S // tr
    row = lambda c: pl.BlockSpec((tr, D), lambda i: (i, c))
    return pl.pallas_call(
        _final_kernel,
        grid=(M // tr,),
        in_specs=[
            row(0), row(0), row(0),
            row(C_MQ // D), row(C_MG // D),
            pl.BlockSpec((tr, 3 * D), lambda i: (i, C_MERGE // (3 * D))),
            pl.BlockSpec((1, N, D), lambda i: (i // nb, 0, 0)),
            pl.BlockSpec((1, N, D), lambda i: (i // nb, 0, 0)),
            pl.BlockSpec((1, MEM_HD), lambda i: (0, 0)),
            pl.BlockSpec((3, D, D), lambda i: (0, 0, 0)),
            pl.BlockSpec((D, D), lambda i: (0, 0)),
        ],
        out_specs=pl.BlockSpec((tr, D), lambda i: (i, 0)),
        out_shape=jax.ShapeDtypeStruct((M, D), F32),
        compiler_params=_cp(("parallel",)),
        name="final",
    )(x2, ya2, yb2, proj2, proj2, proj2, mk, mv, qn, wb, wo)


def _overlap_matrix_t(S):
    n_cmp = (S - CMP_LEN) // CMP_STRIDE + 1
    n_slc = S // SLC_BLOCK
    cs = np.arange(n_cmp)[:, None] * CMP_STRIDE
    ss = np.arange(n_slc)[None, :] * SLC_BLOCK
    ov = np.clip(np.minimum(cs + CMP_LEN, ss + SLC_BLOCK) - np.maximum(cs, ss), 0, None) / CMP_LEN
    out = np.zeros((SLC_LANES, S // CMP_STRIDE), np.float32)
    out[:n_slc, :n_cmp] = ov.T
    return jnp.asarray(out, BF16)


def _pad_cols(w, n):
    return jnp.pad(w, ((0, 0), (0, n - w.shape[1])))


def _layer(x, mem, positions, norm_gain, mem_norm_gain, w_in, w_gla_alpha, b_gla_alpha, gla_out_norm,
           nsa_q_norm, nsa_k_norm, pe_cmp_k, pe_cmp_v, w_cmp_k1, w_cmp_k2, w_cmp_v1, w_cmp_v2,
           w_mem_kv, mem_q_norm, mem_k_norm, w_branch, w_out):
    B, S, D = x.shape
    assert D == D_MODEL and S % (2 * SLC_CHUNK) == 0 and S >= WIN_KEYS and S // SLC_BLOCK <= SLC_LANES

    o = np.cumsum([0, 512, 512, 1024, 16, 1024, 1024, 1536, 48, 1024, 1024, 1024, 3072])
    sec = [w_in[:, o[i]:o[i + 1]] for i in range(12)]
    gq, gk, gv, glr, gg, nq, nkv, nsg, ngate, mq, mg, merge = sec
    w_all = jnp.concatenate(
        [gv, gg, nq, ngate, mq, mg, merge, nkv, gq, gk, _pad_cols(glr, 128), _pad_cols(nsg, 128),
         jnp.zeros((D, NP - C_NSG - 128), w_in.dtype)], axis=1).astype(BF16)

    x2 = x.reshape(B * S, D)
    proj2 = _proj(x2, norm_gain.reshape(1, D), w_all)
    proj3 = proj2.reshape(B, S, NP)

    wa = jnp.pad(w_gla_alpha, ((0, 128 - GLA_RANK), (0, 0)))
    ya = _gla(proj3, wa, b_gla_alpha.reshape(1, -1), gla_out_norm.reshape(1, -1))

    half = NSA_HD // 2
    inv = ROPE_THETA ** (-jnp.arange(half, dtype=F32) / half)
    inv = jnp.tile(inv, 128 // half).reshape(1, 128)
    qn = jnp.tile(nsa_q_norm, NSA_KV_HEADS).reshape(1, -1)
    kn = jnp.tile(nsa_k_norm, (1, NSA_KV_HEADS))
    gid = np.arange(256) // NSA_HD
    gmat = jnp.asarray(gid[:, None] == gid[None, :], BF16)
    qr, kc_tok, vc_tok, ks, vst, kw, vwt = _nsa_prep(proj3, positions.reshape(B, S, 1), inv, qn, kn, gmat)

    def cmp_args(pe, w1, w2):
        pe2 = pe.reshape(2, CMP_STRIDE * NSA_HD)
        pe16 = jnp.concatenate([jnp.broadcast_to(pe2[0:1], (8, pe2.shape[1])), jnp.broadcast_to(pe2[1:2], (8, pe2.shape[1]))], 0)
        w1f = w1.reshape(CMP_LEN * NSA_HD, CMP_HIDDEN).astype(BF16)
        return pe16, w1f[:CMP_STRIDE * NSA_HD], w1f[CMP_STRIDE * NSA_HD:], w2.astype(BF16)

    kc = _compress(kc_tok, *cmp_args(pe_cmp_k, w_cmp_k1, w_cmp_k2), transposed=False)
    vct = _compress(vc_tok, *cmp_args(pe_cmp_v, w_cmp_v1, w_cmp_v2), transposed=True)
    yb = _nsa_attn(qr, kc, vct, ks, vst, kw, vwt, proj3, _overlap_matrix_t(S))

    mk, mv = _mem_prep(mem, mem_norm_gain.reshape(1, D), w_mem_kv.astype(BF16), mem_k_norm.reshape(1, -1))
    out = _final(x2, ya.reshape(B * S, D), yb.reshape(B * S, D), proj2, mk, mv, mem_q_norm.reshape(1, -1),
                 w_branch.astype(BF16), w_out.astype(BF16), S)
    return out.reshape(B, S, D)


def kernel(x, mem, positions, norm_gain, mem_norm_gain, w_in, w_gla_alpha, b_gla_alpha, gla_out_norm, nsa_q_norm, nsa_k_norm, pe_cmp_k, pe_cmp_v, w_cmp_k1, w_cmp_k2, w_cmp_v1, w_cmp_v2, w_mem_kv, mem_q_norm, mem_k_norm, w_branch, w_out):
    h = x
    for l in range(norm_gain.shape[0]):
        h = _layer(h, mem, positions, norm_gain[l], mem_norm_gain[l], w_in[l], w_gla_alpha[l], b_gla_alpha[l],
                   gla_out_norm[l], nsa_q_norm[l], nsa_k_norm[l], pe_cmp_k[l], pe_cmp_v[l], w_cmp_k1[l],
                   w_cmp_k2[l], w_cmp_v1[l], w_cmp_v2[l], w_mem_kv[l], mem_q_norm[l], mem_k_norm[l],
                   w_branch[l], w_out[l])
    return h
```

```python
import functools

import numpy as np
import jax
import jax.numpy as jnp
from jax import lax
from jax.experimental import pallas as pl
from jax.experimental.pallas import tpu as pltpu

F32 = jnp.float32
BF16 = jnp.bfloat16
HIGHEST = lax.Precision.HIGHEST

D_MODEL = 1024
ROPE_THETA = 10000.0
EPS = 1e-6
NEG = -1e30

GLA_HEADS = 4
GLA_DK = 128
GLA_DV = 256
GLA_RANK = 16
GLA_TAU = 16.0
GLA_CHUNK = 64
GLA_SUB = 16

NSA_HEADS = 16
NSA_KV_HEADS = 4
NSA_GROUP = 4
NSA_HD = 64
CMP_LEN = 32
CMP_SHIFT = 4
CMP_STRIDE = 1 << CMP_SHIFT
CMP_HIDDEN = 256
SLC_BLOCK = 64
SLC_TOPN = 16
WINDOW = 512
N_NSA_BRANCH = 3
SLC_LANES = 128
SLC_CHUNK = 512
VT_ROWS = NSA_HD + 16
LOG2E = 1.4426950408889634
NSA_STEP_HEADS = 2
NSA_STEP_BLOCKS = 2
WIN_ALIGN = 128
WIN_KEYS = 640
KAUG = SLC_LANES + NSA_HD

MEM_HEADS = 4
MEM_HD = 256

C_GV, C_GG, C_NQ, C_NGATE, C_MQ, C_MG, C_MERGE = 0, 1024, 2048, 3072, 4096, 5120, 6144
C_NKV, C_GQ, C_GK = 9216, 10752, 11264
NP = 12288
CS_LR, CS_NSG, NP_SMALL = 0, 128, 256

VMEM_LIMIT = 48 * 1024 * 1024

NT = (((1,), (1,)), ((), ()))
TN = (((0,), (0,)), ((), ()))


def _cp(sem):
    return pltpu.CompilerParams(dimension_semantics=sem, vmem_limit_bytes=VMEM_LIMIT)


def _silu(x):
    return x * (1.0 / (1.0 + jnp.exp(-x)))


def _sigmoid(x):
    return 1.0 / (1.0 + jnp.exp(-x))


def _proj_kernel(x_ref, g_ref, w_ref, ws_ref, o_ref, os_ref, xn_ref):
    @pl.when(pl.program_id(1) == 0)
    def _():
        x = x_ref[...]
        ms = jnp.mean(x * x, axis=-1, keepdims=True)
        xn = (x * lax.rsqrt(ms + EPS) * g_ref[...]).astype(BF16)
        xn_ref[...] = xn
        os_ref[...] = jnp.dot(xn, ws_ref[...], preferred_element_type=F32)

    o_ref[...] = jnp.dot(xn_ref[...], w_ref[...], preferred_element_type=F32).astype(o_ref.dtype)


def _proj(x2, gain, w_all, w_small, tm=1024, tn=1024):
    M = x2.shape[0]
    ns = w_small.shape[1]
    return pl.pallas_call(
        _proj_kernel,
        grid=(M // tm, NP // tn),
        in_specs=[
            pl.BlockSpec((tm, D_MODEL), lambda i, j: (i, 0)),
            pl.BlockSpec((1, D_MODEL), lambda i, j: (0, 0)),
            pl.BlockSpec((D_MODEL, tn), lambda i, j: (0, j)),
            pl.BlockSpec((D_MODEL, ns), lambda i, j: (0, 0)),
        ],
        out_specs=[pl.BlockSpec((tm, tn), lambda i, j: (i, j)), pl.BlockSpec((tm, ns), lambda i, j: (i, 0))],
        out_shape=[jax.ShapeDtypeStruct((M, NP), BF16), jax.ShapeDtypeStruct((M, ns), F32)],
        scratch_shapes=[pltpu.VMEM((tm, D_MODEL), BF16)],
        compiler_params=_cp(("parallel", "arbitrary")),
        name="proj",
    )(x2, gain, w_all, w_small)


def _gla_kernel(q_ref, k_ref, v_ref, gate_ref, lr_ref, wa_ref, ba_ref, gn_ref, o_ref, st_ref, g_ref, *, n_chunks):
    C, SB, H, DK, DV = GLA_CHUNK, GLA_SUB, GLA_HEADS, GLA_DK, GLA_DV

    @pl.when(pl.program_id(1) == 0)
    def _():
        st_ref[...] = jnp.zeros_like(st_ref)

    gn = gn_ref[...]
    ri = lax.broadcasted_iota(jnp.int32, (C, C), 0)
    ci = lax.broadcasted_iota(jnp.int32, (C, C), 1)
    tri = (ri >= ci).astype(F32)
    si = lax.broadcasted_iota(jnp.int32, (SB, SB), 0)
    sj = lax.broadcasted_iota(jnp.int32, (SB, SB), 1)
    sub_causal = si >= sj

    z = jnp.dot(lr_ref[0], wa_ref[...], precision=HIGHEST, preferred_element_type=F32) + ba_ref[...]
    la = -(jnp.maximum(-z, 0.0) + jnp.log(1.0 + jnp.exp(-jnp.abs(z)))) * (1.0 / GLA_TAU)
    for c in range(n_chunks):
        g_ref[c * C:(c + 1) * C, :] = jnp.dot(tri, la[c * C:(c + 1) * C], precision=HIGHEST, preferred_element_type=F32)

    def chunk(c, carry):
        rows = pl.ds(pl.multiple_of(c * C, C), C)
        for h in range(H):
            kl = slice(h * DK, (h + 1) * DK)
            vl = slice(h * DV, (h + 1) * DV)
            q = q_ref[0, rows, kl].astype(F32) * (DK ** -0.5)
            k = k_ref[0, rows, kl].astype(F32)
            vb = v_ref[0, rows, vl]
            g = g_ref[rows, kl]

            outs = []
            for i in range(C // SB):
                lo = i * SB
                gi = g[lo:lo + SB]
                qi = q[lo:lo + SB]
                ki = k[lo:lo + SB]
                d = gi[:, None, :] - gi[None, :, :]
                e = jnp.exp(jnp.minimum(d, 0.0))
                a_d = jnp.sum(qi[:, None, :] * ki[None, :, :] * e, axis=-1)
                a_d = jnp.where(sub_causal, a_d, 0.0)
                o_i = jnp.dot(a_d.astype(BF16), vb[lo:lo + SB], preferred_element_type=F32)
                if i > 0:
                    r = g[lo:lo + 1]
                    qt = qi * jnp.exp(gi - r)
                    kt = k[:lo] * jnp.exp(r - g[:lo])
                    a_o = lax.dot_general(qt.astype(BF16), kt.astype(BF16), NT, preferred_element_type=F32)
                    o_i = o_i + jnp.dot(a_o.astype(BF16), vb[:lo], preferred_element_type=F32)
                outs.append(o_i)
            o = jnp.concatenate(outs, axis=0)

            st = st_ref[h]
            qg = q * jnp.exp(g)
            o = o + lax.dot_general(qg.astype(BF16), st.astype(BF16), NT, preferred_element_type=F32)
            gl = g[C - 1:C]
            kd = k * jnp.exp(gl - g)
            st_ref[h] = st * jnp.exp(gl) + lax.dot_general(vb, kd.astype(BF16), TN, preferred_element_type=F32)

            ms = jnp.mean(o * o, axis=-1, keepdims=True)
            y = o * lax.rsqrt(ms + EPS) * gn
            o_ref[0, rows, vl] = (y * _silu(gate_ref[0, rows, vl].astype(F32))).astype(o_ref.dtype)
        return carry

    lax.fori_loop(0, n_chunks, chunk, 0)


def _gla(proj3, small3, wa, ba, gn, tr=512):
    B, S, _ = proj3.shape
    H, DK, DV = GLA_HEADS, GLA_DK, GLA_DV
    kern = functools.partial(_gla_kernel, n_chunks=tr // GLA_CHUNK)
    return pl.pallas_call(
        kern,
        grid=(B, S // tr),
        in_specs=[
            pl.BlockSpec((1, tr, H * DK), lambda b, r: (b, r, C_GQ // (H * DK))),
            pl.BlockSpec((1, tr, H * DK), lambda b, r: (b, r, C_GK // (H * DK))),
            pl.BlockSpec((1, tr, H * DV), lambda b, r: (b, r, C_GV // (H * DV))),
            pl.BlockSpec((1, tr, H * DV), lambda b, r: (b, r, C_GG // (H * DV))),
            pl.BlockSpec((1, tr, 128), lambda b, r: (b, r, CS_LR // 128)),
            pl.BlockSpec((128, H * DK), lambda b, r: (0, 0)),
            pl.BlockSpec((1, H * DK), lambda b, r: (0, 0)),
            pl.BlockSpec((1, DV), lambda b, r: (0, 0)),
        ],
        out_specs=pl.BlockSpec((1, tr, H * DV), lambda b, r: (b, r, 0)),
        out_shape=jax.ShapeDtypeStruct((B, S, H * DV), BF16),
        scratch_shapes=[pltpu.VMEM((H, DV, DK), F32), pltpu.VMEM((tr, H * DK), F32)],
        compiler_params=_cp(("parallel", "arbitrary")),
        name="gla",
    )(proj3, proj3, proj3, proj3, small3, wa, ba, gn)


def _group_meansq(x, gmat):
    sq = x * x
    hi = sq.astype(BF16)
    lo = (sq - hi.astype(F32)).astype(BF16)
    s = jnp.dot(hi, gmat, preferred_element_type=F32) + jnp.dot(lo, gmat, preferred_element_type=F32)
    return s * (1.0 / NSA_HD)


def _rope(x, cos, sin, first_half):
    w = x.shape[-1]
    rot = jnp.where(first_half, -pltpu.roll(x, w - NSA_HD // 2, 1), pltpu.roll(x, NSA_HD // 2, 1))
    return x * cos + rot * sin


def _nsa_prep_kernel(q_ref, kvc_ref, kvs_ref, kvw_ref, pos_ref, inv_ref, qn_ref, kn_ref, gm_ref,
                     qr_ref, kc_ref, vc_ref, ks_ref, vst_ref, kw_ref, vwt_ref):
    tr = q_ref.shape[1]
    W = NSA_KV_HEADS * NSA_HD
    HD = NSA_HD
    gmat = gm_ref[...]
    ang = pos_ref[0].astype(F32) * inv_ref[...]
    cos1, sin1 = jnp.cos(ang), jnp.sin(ang)
    cos = jnp.concatenate([cos1, cos1], axis=1)
    sin = jnp.concatenate([sin1, sin1], axis=1)
    lane = lax.broadcasted_iota(jnp.int32, (tr, W), 1)
    first_half = (lane % HD) < (HD // 2)

    def norm_rope(x, gain):
        y = x * lax.rsqrt(_group_meansq(x, gmat) + EPS) * gain
        return _rope(y, cos, sin, first_half)

    qn = qn_ref[...]
    for s in range(NSA_HEADS * HD // W):
        xq = q_ref[0, :, s * W:(s + 1) * W].astype(F32)
        qr_ref[0, :, s * W:(s + 1) * W] = (norm_rope(xq, qn) * (HD ** -0.5 * LOG2E)).astype(qr_ref.dtype)

    kc = norm_rope(kvc_ref[0, :, :W].astype(F32), kn_ref[0:1, :]).astype(BF16)
    vc = kvc_ref[0, :, W:]
    for h in range(NSA_KV_HEADS):
        kc_ref[0, h] = kc[:, h * HD:(h + 1) * HD]
        vc_ref[0, h] = vc[:, h * HD:(h + 1) * HD]

    ks = norm_rope(kvs_ref[0, :, :W].astype(F32), kn_ref[1:2, :]).astype(BF16)
    row_blk = (pl.program_id(1) * tr + lax.broadcasted_iota(jnp.int32, (tr, SLC_LANES), 0)) // SLC_BLOCK
    onehot = (row_blk == lax.broadcasted_iota(jnp.int32, (tr, SLC_LANES), 1)).astype(BF16)
    vst = kvs_ref[0, :, W:].astype(F32).T
    for h in range(NSA_KV_HEADS):
        ks_ref[0, h] = jnp.concatenate([onehot, ks[:, h * HD:(h + 1) * HD]], axis=1)
        for j in range(tr // SLC_CHUNK):
            vst_ref[0, h, j, :HD] = vst[h * HD:(h + 1) * HD, j * SLC_CHUNK:(j + 1) * SLC_CHUNK].astype(BF16)
            vst_ref[0, h, j, HD:] = jnp.ones((VT_ROWS - HD, SLC_CHUNK), BF16)

    kw = norm_rope(kvw_ref[0, :, :W].astype(F32), kn_ref[2:3, :]).astype(BF16)
    vwt = kvw_ref[0, :, W:].astype(F32).T
    for h in range(NSA_KV_HEADS):
        kw_ref[0, h] = kw[:, h * HD:(h + 1) * HD]
        for j in range(tr // WIN_ALIGN):
            vwt_ref[0, h, j, :HD] = vwt[h * HD:(h + 1) * HD, j * WIN_ALIGN:(j + 1) * WIN_ALIGN].astype(BF16)
            vwt_ref[0, h, j, HD:] = jnp.ones((VT_ROWS - HD, WIN_ALIGN), BF16)


def _nsa_prep(proj3, pos3, inv, qn, kn, gmat, tr=512):
    B, S, _ = proj3.shape
    Hk, HD = NSA_KV_HEADS, NSA_HD
    hm = lambda w, dt: jax.ShapeDtypeStruct((B, Hk, S, w), dt)
    hspec = lambda w: pl.BlockSpec((1, Hk, tr, w), lambda b, r: (b, 0, r, 0))
    tspec = lambda c, rows=HD: pl.BlockSpec((1, Hk, tr // c, rows, c), lambda b, r: (b, 0, r, 0, 0))
    return pl.pallas_call(
        _nsa_prep_kernel,
        grid=(B, S // tr),
        in_specs=[
            pl.BlockSpec((1, tr, 1024), lambda b, r: (b, r, C_NQ // 1024)),
            pl.BlockSpec((1, tr, 512), lambda b, r: (b, r, C_NKV // 512)),
            pl.BlockSpec((1, tr, 512), lambda b, r: (b, r, C_NKV // 512 + 1)),
            pl.BlockSpec((1, tr, 512), lambda b, r: (b, r, C_NKV // 512 + 2)),
            pl.BlockSpec((1, tr, 1), lambda b, r: (b, r, 0)),
            pl.BlockSpec((1, 128), lambda b, r: (0, 0)),
            pl.BlockSpec((1, 256), lambda b, r: (0, 0)),
            pl.BlockSpec((3, 256), lambda b, r: (0, 0)),
            pl.BlockSpec((256, 256), lambda b, r: (0, 0)),
        ],
        out_specs=[pl.BlockSpec((1, tr, 1024), lambda b, r: (b, r, 0)), hspec(HD), hspec(HD), hspec(KAUG),
                   tspec(SLC_CHUNK, VT_ROWS), hspec(HD), tspec(WIN_ALIGN, VT_ROWS)],
        out_shape=[jax.ShapeDtypeStruct((B, S, 1024), BF16), hm(HD, BF16), hm(HD, BF16), hm(KAUG, BF16),
                   jax.ShapeDtypeStruct((B, Hk, S // SLC_CHUNK, VT_ROWS, SLC_CHUNK), BF16), hm(HD, BF16),
                   jax.ShapeDtypeStruct((B, Hk, S // WIN_ALIGN, VT_ROWS, WIN_ALIGN), BF16)],
        compiler_params=_cp(("parallel", "parallel")),
        name="nsa_prep",
    )(proj3, proj3, proj3, proj3, pos3, inv, qn, kn, gmat)


def _compress_kernel(t_ref, pe_ref, w1a_ref, w1b_ref, w2_ref, *rest, transposed):
    tail_ref, o_ref = rest if transposed else (None,) + rest
    t = t_ref[0, 0]
    w1a, w1b = w1a_ref[...], w1b_ref[...]
    u = jnp.dot(t, w1a, preferred_element_type=F32)
    v = jnp.dot(t, w1b, preferred_element_type=F32)
    pe = pe_ref[...].astype(BF16)
    c = (jnp.dot(pe[0:8], w1a, preferred_element_type=F32) + jnp.dot(pe[8:16], w1b, preferred_element_type=F32))[0:1]
    n = v.shape[0]
    h = u + pltpu.roll(v, n - 1, 0) + c
    h = jax.nn.gelu(h).astype(BF16)
    if transposed:
        hd = w2_ref.shape[0]
        o_ref[0, 0, :hd] = lax.dot_general(w2_ref[...], h, NT, preferred_element_type=F32).astype(o_ref.dtype)
        o_ref[0, 0, hd:] = tail_ref[...]
    else:
        o_ref[0, 0] = jnp.dot(h, w2_ref[...], preferred_element_type=F32).astype(o_ref.dtype)


def _compress(tok, pe2, w1a, w1b, w2, tail=None):
    B, Hk, S, HD = tok.shape
    n = S // CMP_STRIDE
    t2 = tok.reshape(B, Hk, n, CMP_STRIDE * HD)
    transposed = tail is not None
    oshape = (HD + tail.shape[0], n) if transposed else (n, HD)
    w2 = w2.T if transposed else w2
    extra = ([tail], [pl.BlockSpec(tail.shape, lambda b, h: (0, 0))]) if transposed else ([], [])
    return pl.pallas_call(
        functools.partial(_compress_kernel, transposed=transposed),
        grid=(B, Hk),
        in_specs=[
            pl.BlockSpec((1, 1, n, CMP_STRIDE * HD), lambda b, h: (b, h, 0, 0)),
            pl.BlockSpec((16, CMP_STRIDE * HD), lambda b, h: (0, 0)),
            pl.BlockSpec((CMP_STRIDE * HD, CMP_HIDDEN), lambda b, h: (0, 0)),
            pl.BlockSpec((CMP_STRIDE * HD, CMP_HIDDEN), lambda b, h: (0, 0)),
            pl.BlockSpec(w2.shape, lambda b, h: (0, 0)),
        ] + extra[1],
        out_specs=pl.BlockSpec((1, 1) + oshape, lambda b, h: (b, h, 0, 0)),
        out_shape=jax.ShapeDtypeStruct((B, Hk) + oshape, BF16),
        compiler_params=_cp(("parallel", "parallel")),
        name="compress_v" if transposed else "compress_k",
    )(t2, pe2, w1a, w1b, w2, *extra[0])


def _nsa_attn_kernel(q_ref, kc_ref, vct_ref, ks_ref, vst_ref, kw_ref, vwt_ref, gt_ref, gate_ref, o_ref,
                     sa_ref, sb_ref, e_ref):
    G, HD, QB, NH, NB = NSA_GROUP, NSA_HD, SLC_BLOCK, NSA_STEP_HEADS, NSA_STEP_BLOCKS
    R = NB * G * QB
    heads = range(NH)
    hk0 = pl.program_id(1) * NH
    step = pl.program_id(2)
    t0 = step * (NB * QB)
    lane = lax.broadcasted_iota(jnp.int32, (1, R), 1)
    t = t0 + (lane // (G * QB)) * QB + lane % QB
    q = [jnp.concatenate([q_ref[0, a * QB:(a + 1) * QB, (h * G + g) * HD:(h * G + g + 1) * HD]
                          for a in range(NB) for g in range(G)], axis=0)
         for h in heads]

    nc = kc_ref.shape[2]
    valid = lax.broadcasted_iota(jnp.int32, (nc, 1), 0) <= ((t - (CMP_LEN - 1)) >> CMP_SHIFT)
    for h in heads:
        s = lax.dot_general(kc_ref[0, h], q[h], NT, preferred_element_type=F32)
        sm = jnp.where(valid, s, NEG)
        m = jnp.maximum(jnp.max(sm, axis=0, keepdims=True), 0.1 * NEG)
        e_ref[h] = jnp.exp2(sm - m).astype(BF16)
    blk = lax.broadcasted_iota(jnp.int32, (SLC_LANES, 2 * QB), 0)
    qlane = lax.broadcasted_iota(jnp.int32, (1, 2 * QB), 1)
    cur = step * NB + (qlane // QB) % NB
    forced = (blk == 0) | (blk == cur) | (blk == cur - 1)
    o_cmp, score = [], []
    for h in heads:
        oc = jnp.dot(vct_ref[0, h], e_ref[h], preferred_element_type=F32)
        den = oc[HD:HD + 1]
        rden = 1.0 / jnp.where(den > 0.0, den, 1.0)
        o_cmp.append(oc[:HD] * rden)
        impf = oc[VT_ROWS:] * rden
        parts = []
        for a in range(NB):
            pa = impf[:, a * G * QB:(a + 1) * G * QB]
            p2 = pa[:, :2 * QB] + pa[:, 2 * QB:]
            parts.append(p2 + pltpu.roll(p2, QB, 1))
        imp = parts[0] if NB == 1 else jnp.where(qlane < QB, parts[0], parts[1])
        score.append(jnp.where(forced, -jnp.inf, jnp.where(blk <= cur, imp, NEG)))

    wc = jnp.maximum(t0 - WINDOW, 0) // WIN_ALIGN
    w0 = pl.multiple_of(wc * WIN_ALIGN, WIN_ALIGN)
    kp = w0 + lax.broadcasted_iota(jnp.int32, (WIN_KEYS, 1), 0)
    win_ok = (kp <= t) & (kp > t - WINDOW)
    o_win = []
    for h in heads:
        sw = lax.dot_general(kw_ref[0, h, pl.ds(w0, WIN_KEYS), :], q[h], NT, preferred_element_type=F32)
        sw = jnp.where(win_ok, sw, NEG)
        pw = jnp.exp2(sw - jnp.max(sw, axis=0, keepdims=True))
        vv = jnp.concatenate([vwt_ref[0, h, wc + j] for j in range(WIN_KEYS // WIN_ALIGN)], axis=1)
        ow = jnp.dot(vv, pw.astype(BF16), preferred_element_type=F32)
        o_win.append(ow[:HD] * (1.0 / ow[HD:HD + 1]))

    few = cur < SLC_TOPN
    sel0 = jnp.where(forced | (few & (blk <= cur)), 1.0, 0.0)
    n_rounds = jnp.where(step * NB >= SLC_TOPN, SLC_TOPN - 3, 0)

    def pick(_, carry):
        out = []
        for sc, sel in carry:
            mx = jnp.max(sc, axis=0, keepdims=True)
            first = jnp.min(jnp.where(sc == mx, blk, SLC_LANES), axis=0, keepdims=True)
            hit = blk == first
            out.append((jnp.where(hit, -jnp.inf, sc), jnp.where(hit, 1.0, sel)))
        return tuple(out)

    picked = lax.fori_loop(0, n_rounds, pick, tuple((score[h], sel0) for h in heads))
    q_aug = []
    for h in heads:
        bias_t = jnp.where(picked[h][1] > 0.5, 0.0, NEG)
        bias = bias_t.T.astype(BF16)
        rows = jnp.concatenate([bias[a * QB:(a + 1) * QB] for a in range(NB) for g in range(G)], axis=0)
        q_aug.append(jnp.concatenate([rows, q[h]], axis=1))

    kpos = lax.broadcasted_iota(jnp.int32, (SLC_CHUNK, 1), 0)

    def scores(c, dst_ref):
        k0 = pl.multiple_of(c * SLC_CHUNK, SLC_CHUNK)
        for h in heads:
            dst_ref[h] = lax.dot_general(ks_ref[0, h, pl.ds(k0, SLC_CHUNK), :], q_aug[h], NT, preferred_element_type=F32)

    def absorb(c, src_ref, carry, masked):
        out = []
        for h in heads:
            m_i, acc = carry[h]
            sc = src_ref[h]
            if masked:
                sc = jnp.where((kpos + c * SLC_CHUNK) <= t, sc, NEG)
            m_new = jnp.maximum(m_i, jnp.max(sc, axis=0, keepdims=True))
            pp = jnp.exp2(sc - m_new).astype(BF16)
            acc = jnp.exp2(m_i - m_new) * acc + jnp.dot(vst_ref[0, h, c], pp, preferred_element_type=F32)
            out.append((m_new, acc))
        return tuple(out)

    diag = t0 // SLC_CHUNK
    n_pairs = diag // 2
    scores(0, sa_ref)

    def pair(j, carry):
        scores(2 * j + 1, sb_ref)
        carry = absorb(2 * j, sa_ref, carry, False)
        scores(2 * j + 2, sa_ref)
        return absorb(2 * j + 1, sb_ref, carry, False)

    init = tuple((jnp.full((1, R), NEG, F32), jnp.zeros((VT_ROWS, R), F32)) for h in heads)
    carry = lax.fori_loop(0, n_pairs, pair, init)
    scores(2 * n_pairs + 1, sb_ref)
    carry = absorb(2 * n_pairs, sa_ref, carry, True)
    carry = absorb(2 * n_pairs + 1, sb_ref, carry, True)

    sig = _sigmoid(gt_ref[0])
    glane = lax.broadcasted_iota(jnp.int32, (QB, 128), 1)
    outs = [[] for a in range(NB)]
    for h in heads:
        acc_s = carry[h][1]
        oc, os_, ow = o_cmp[h].T, (acc_s[:HD] * (1.0 / acc_s[HD:HD + 1])).T, o_win[h].T
        for a in range(NB):
            sig_a = sig[a * QB:(a + 1) * QB]
            for g in range(G):
                base = ((hk0 + h) * G + g) * N_NSA_BRANCH
                gc, gs, gw = [jnp.sum(jnp.where(glane == base + b, sig_a, 0.0), axis=-1, keepdims=True) for b in range(3)]
                r = slice((a * G + g) * QB, (a * G + g + 1) * QB)
                outs[a].append(gc * oc[r] + gs * os_[r] + gw * ow[r])
    o = jnp.concatenate([jnp.concatenate(oa, axis=1) for oa in outs], axis=0)
    o_ref[0] = (o * _silu(gate_ref[0].astype(F32))).astype(o_ref.dtype)


def _nsa_attn(qr, kc, vct, ks, vst, kw, vwt, proj3, small3):
    B, S, _ = qr.shape
    Hk, HD, NH = NSA_KV_HEADS, NSA_HD, NSA_STEP_HEADS
    TQ = NSA_STEP_BLOCKS * SLC_BLOCK
    nc = kc.shape[2]
    W = NH * NSA_GROUP * HD
    full = lambda a: pl.BlockSpec((1, NH) + a.shape[2:], lambda b, h, t: (b, h) + (0,) * (a.ndim - 2),
                                  pipeline_mode=pl.Buffered(1))
    return pl.pallas_call(
        _nsa_attn_kernel,
        grid=(B, Hk // NH, S // TQ),
        in_specs=[
            pl.BlockSpec((1, TQ, W), lambda b, h, t: (b, t, h)),
            full(kc), full(vct), full(ks), full(vst), full(kw), full(vwt),
            pl.BlockSpec((1, TQ, 128), lambda b, h, t: (b, t, CS_NSG // 128)),
            pl.BlockSpec((1, TQ, W), lambda b, h, t: (b, t, C_NGATE // W + h)),
        ],
        out_specs=pl.BlockSpec((1, TQ, W), lambda b, h, t: (b, t, h)),
        out_shape=jax.ShapeDtypeStruct((B, S, NSA_HEADS * HD), BF16),
        scratch_shapes=[pltpu.VMEM((NH, SLC_CHUNK, NSA_GROUP * TQ), F32)] * 2
        + [pltpu.VMEM((NH, nc, NSA_GROUP * TQ), BF16)],
        compiler_params=_cp(("parallel", "parallel", "arbitrary")),
        name="nsa_attn",
    )(qr, kc, vct, ks, vst, kw, vwt, small3, proj3)


def _head_rms(x, gain, hd):
    outs = []
    for h in range(x.shape[-1] // hd):
        xh = x[:, h * hd:(h + 1) * hd]
        ms = jnp.mean(xh * xh, axis=-1, keepdims=True)
        outs.append(xh * lax.rsqrt(ms + EPS) * gain)
    return jnp.concatenate(outs, axis=1)


def _mem_prep_kernel(mem_ref, g_ref, w_ref, kn_ref, mk_ref, mv_ref):
    x = mem_ref[0]
    ms = jnp.mean(x * x, axis=-1, keepdims=True)
    xn = (x * lax.rsqrt(ms + EPS) * g_ref[...]).astype(BF16)
    kv = jnp.dot(xn, w_ref[...], preferred_element_type=F32)
    W = MEM_HEADS * MEM_HD
    mk_ref[0] = _head_rms(kv[:, :W], kn_ref[...], MEM_HD).astype(mk_ref.dtype)
    mv_ref[0] = kv[:, W:].astype(mv_ref.dtype)


def _mem_prep(mem, gain, w_kv, kn):
    B, N, D = mem.shape
    W = MEM_HEADS * MEM_HD
    return pl.pallas_call(
        _mem_prep_kernel,
        grid=(B,),
        in_specs=[
            pl.BlockSpec((1, N, D), lambda b: (b, 0, 0)),
            pl.BlockSpec((1, D), lambda b: (0, 0)),
            pl.BlockSpec((D, 2 * W), lambda b: (0, 0)),
            pl.BlockSpec((1, MEM_HD), lambda b: (0, 0)),
        ],
        out_specs=[pl.BlockSpec((1, N, W), lambda b: (b, 0, 0))] * 2,
        out_shape=[jax.ShapeDtypeStruct((B, N, W), BF16)] * 2,
        compiler_params=_cp(("parallel",)),
        name="mem_prep",
    )(mem, gain, w_kv, kn)


def _final_kernel(x_ref, ya_ref, yb_ref, mq_ref, mg_ref, mr_ref, mk_ref, mv_ref, qn_ref, wb_ref, wo_ref, o_ref):
    mq = _head_rms(mq_ref[...].astype(F32), qn_ref[...], MEM_HD)
    mk = mk_ref[0]
    mv = mv_ref[0]
    heads = []
    for h in range(MEM_HEADS):
        sl = slice(h * MEM_HD, (h + 1) * MEM_HD)
        s = lax.dot_general(mq[:, sl].astype(BF16), mk[:, sl], NT, preferred_element_type=F32) * (MEM_HD ** -0.5)
        m = jnp.max(s, axis=-1, keepdims=True)
        p = jnp.exp(s - m)
        o = jnp.dot(p.astype(BF16), mv[:, sl], preferred_element_type=F32) / jnp.sum(p, axis=-1, keepdims=True)
        heads.append(o)
    ym = jnp.concatenate(heads, axis=1) * _silu(mg_ref[...].astype(F32))

    mixed = None
    for c, y in enumerate((ya_ref[...], yb_ref[...], ym.astype(BF16))):
        z = jnp.dot(y, wb_ref[c], preferred_element_type=F32)
        term = _sigmoid(mr_ref[:, c * D_MODEL:(c + 1) * D_MODEL].astype(F32)) * z
        mixed = term if mixed is None else mixed + term
    o_ref[...] = x_ref[...] + jnp.dot(mixed.astype(BF16), wo_ref[...], preferred_element_type=F32)


def _final(x2, ya2, yb2, proj2, mk, mv, qn, wb, wo, S, tr=512):
    M, D = x2.shape
    N = mk.shape[1]
    nb = S // tr
    row = lambda c: pl.BlockSpec((tr, D), lambda i: (i, c))
    return pl.pallas_call(
        _final_kernel,
        grid=(M // tr,),
        in_specs=[
            row(0), row(0), row(0),
            row(C_MQ // D), row(C_MG // D),
            pl.BlockSpec((tr, 3 * D), lambda i: (i, C_MERGE // (3 * D))),
            pl.BlockSpec((1, N, D), lambda i: (i // nb, 0, 0)),
            pl.BlockSpec((1, N, D), lambda i: (i // nb, 0, 0)),
            pl.BlockSpec((1, MEM_HD), lambda i: (0, 0)),
            pl.BlockSpec((3, D, D), lambda i: (0, 0, 0), pipeline_mode=pl.Buffered(1)),
            pl.BlockSpec((D, D), lambda i: (0, 0), pipeline_mode=pl.Buffered(1)),
        ],
        out_specs=pl.BlockSpec((tr, D), lambda i: (i, 0)),
        out_shape=jax.ShapeDtypeStruct((M, D), F32),
        compiler_params=_cp(("parallel",)),
        name="final",
    )(x2, ya2, yb2, proj2, proj2, proj2, mk, mv, qn, wb, wo)


def _overlap_matrix_t(S):
    n_cmp = (S - CMP_LEN) // CMP_STRIDE + 1
    n_slc = S // SLC_BLOCK
    cs = np.arange(n_cmp)[:, None] * CMP_STRIDE
    ss = np.arange(n_slc)[None, :] * SLC_BLOCK
    ov = np.clip(np.minimum(cs + CMP_LEN, ss + SLC_BLOCK) - np.maximum(cs, ss), 0, None) / CMP_LEN
    out = np.zeros((SLC_LANES, S // CMP_STRIDE), np.float32)
    out[:n_slc, :n_cmp] = ov.T
    return jnp.asarray(out, BF16)


def _pad_cols(w, n):
    return jnp.pad(w, ((0, 0), (0, n - w.shape[1])))


def _layer(x, mem, positions, norm_gain, mem_norm_gain, w_in, w_gla_alpha, b_gla_alpha, gla_out_norm,
           nsa_q_norm, nsa_k_norm, pe_cmp_k, pe_cmp_v, w_cmp_k1, w_cmp_k2, w_cmp_v1, w_cmp_v2,
           w_mem_kv, mem_q_norm, mem_k_norm, w_branch, w_out):
    B, S, D = x.shape
    assert D == D_MODEL and S % (2 * SLC_CHUNK) == 0 and S >= WIN_KEYS and S // SLC_BLOCK <= SLC_LANES

    o = np.cumsum([0, 512, 512, 1024, 16, 1024, 1024, 1536, 48, 1024, 1024, 1024, 3072])
    sec = [w_in[:, o[i]:o[i + 1]] for i in range(12)]
    gq, gk, gv, glr, gg, nq, nkv, nsg, ngate, mq, mg, merge = sec
    w_all = jnp.concatenate([gv, gg, nq, ngate, mq, mg, merge, nkv, gq, gk], axis=1)
    w_all = _pad_cols(w_all, NP).astype(BF16)
    w_small = jnp.concatenate([_pad_cols(glr, 128), _pad_cols(nsg, 128)], axis=1).astype(BF16)

    x2 = x.reshape(B * S, D)
    proj2, small2 = _proj(x2, norm_gain.reshape(1, D), w_all, w_small)
    proj3 = proj2.reshape(B, S, NP)
    small3 = small2.reshape(B, S, NP_SMALL)

    wa = jnp.pad(w_gla_alpha, ((0, 128 - GLA_RANK), (0, 0)))
    ya = _gla(proj3, small3, wa, b_gla_alpha.reshape(1, -1), gla_out_norm.reshape(1, -1))

    half = NSA_HD // 2
    inv = ROPE_THETA ** (-jnp.arange(half, dtype=F32) / half)
    inv = jnp.tile(inv, 128 // half).reshape(1, 128)
    qn = jnp.tile(nsa_q_norm, NSA_KV_HEADS).reshape(1, -1)
    kn = jnp.tile(nsa_k_norm, (1, NSA_KV_HEADS))
    gid = np.arange(256) // NSA_HD
    gmat = jnp.asarray(gid[:, None] == gid[None, :], BF16)
    qr, kc_tok, vc_tok, ks, vst, kw, vwt = _nsa_prep(proj3, positions.reshape(B, S, 1), inv, qn, kn, gmat)

    def cmp_args(pe, w1, w2):
        pe2 = pe.reshape(2, CMP_STRIDE * NSA_HD)
        pe16 = jnp.concatenate([jnp.broadcast_to(pe2[0:1], (8, pe2.shape[1])), jnp.broadcast_to(pe2[1:2], (8, pe2.shape[1]))], 0)
        w1f = w1.reshape(CMP_LEN * NSA_HD, CMP_HIDDEN).astype(BF16)
        return pe16, w1f[:CMP_STRIDE * NSA_HD], w1f[CMP_STRIDE * NSA_HD:], w2.astype(BF16)

    kc = _compress(kc_tok, *cmp_args(pe_cmp_k, w_cmp_k1, w_cmp_k2))
    tail = jnp.concatenate([jnp.ones((VT_ROWS - NSA_HD, S // CMP_STRIDE), BF16), _overlap_matrix_t(S)], axis=0)
    vct = _compress(vc_tok, *cmp_args(pe_cmp_v, w_cmp_v1, w_cmp_v2), tail=tail)
    yb = _nsa_attn(qr, kc, vct, ks, vst, kw, vwt, proj3, small3)

    mk, mv = _mem_prep(mem, mem_norm_gain.reshape(1, D), w_mem_kv.astype(BF16), mem_k_norm.reshape(1, -1))
    out = _final(x2, ya.reshape(B * S, D), yb.reshape(B * S, D), proj2, mk, mv, mem_q_norm.reshape(1, -1),
                 w_branch.astype(BF16), w_out.astype(BF16), S)
    return out.reshape(B, S, D)


def kernel(x, mem, positions, norm_gain, mem_norm_gain, w_in, w_gla_alpha, b_gla_alpha, gla_out_norm, nsa_q_norm, nsa_k_norm, pe_cmp_k, pe_cmp_v, w_cmp_k1, w_cmp_k2, w_cmp_v1, w_cmp_v2, w_mem_kv, mem_q_norm, mem_k_norm, w_branch, w_out):
    h = x
    for l in range(norm_gain.shape[0]):
        h = _layer(h, mem, positions, norm_gain[l], mem_norm_gain[l], w_in[l], w_gla_alpha[l], b_gla_alpha[l],
                   gla_out_norm[l], nsa_q_norm[l], nsa_k_norm[l], pe_cmp_k[l], pe_cmp_v[l], w_cmp_k1[l],
                   w_cmp_k2[l], w_cmp_v1[l], w_cmp_v2[l], w_mem_kv[l], mem_q_norm[l], mem_k_norm[l],
                   w_branch[l], w_out[l])
    return h
```

```python
import functools

import numpy as np
import jax
import jax.numpy as jnp
from jax import lax
from jax.experimental import pallas as pl
from jax.experimental.pallas import tpu as pltpu

F32 = jnp.float32
BF16 = jnp.bfloat16
HIGHEST = lax.Precision.HIGHEST

D_MODEL = 1024
ROPE_THETA = 10000.0
EPS = 1e-6
NEG = -1e30

GLA_HEADS = 4
GLA_DK = 128
GLA_DV = 256
GLA_RANK = 16
GLA_TAU = 16.0
GLA_CHUNK = 64
GLA_SUB = 16

NSA_HEADS = 16
NSA_KV_HEADS = 4
NSA_GROUP = 4
NSA_HD = 64
CMP_LEN = 32
CMP_SHIFT = 4
CMP_STRIDE = 1 << CMP_SHIFT
CMP_HIDDEN = 256
SLC_BLOCK = 64
SLC_TOPN = 16
WINDOW = 512
N_NSA_BRANCH = 3
SLC_LANES = 128
SLC_CHUNK = 512
VT_ROWS = NSA_HD + 16
LOG2E = 1.4426950408889634
NSA_STEP_HEADS = 2
NSA_STEP_BLOCKS = 2
WIN_ALIGN = 128
WIN_KEYS = 640
KAUG = SLC_LANES + 2 * NSA_HD

MEM_HEADS = 4
MEM_HD = 256

C_GV, C_GG, C_NQ, C_NGATE, C_MQ, C_MG, C_MERGE = 0, 1024, 2048, 3072, 4096, 5120, 6144
C_NKV, C_GQ, C_GK = 9216, 10752, 11264
NP = 12288
CS_LR, CS_NSG, NP_SMALL = 0, 128, 256

VMEM_LIMIT = 48 * 1024 * 1024

NT = (((1,), (1,)), ((), ()))
TN = (((0,), (0,)), ((), ()))


def _cp(sem):
    return pltpu.CompilerParams(dimension_semantics=sem, vmem_limit_bytes=VMEM_LIMIT)


def _silu(x):
    return x * (1.0 / (1.0 + jnp.exp(-x)))


def _sigmoid(x):
    return 1.0 / (1.0 + jnp.exp(-x))


def _proj_kernel(x_ref, g_ref, w_ref, ws_ref, o_ref, os_ref, xn_ref):
    @pl.when(pl.program_id(1) == 0)
    def _():
        x = x_ref[...]
        ms = jnp.mean(x * x, axis=-1, keepdims=True)
        xn = (x * lax.rsqrt(ms + EPS) * g_ref[...]).astype(BF16)
        xn_ref[...] = xn
        os_ref[...] = jnp.dot(xn, ws_ref[...], preferred_element_type=F32)

    o_ref[...] = jnp.dot(xn_ref[...], w_ref[...], preferred_element_type=F32).astype(o_ref.dtype)


def _proj(x2, gain, w_all, w_small, tm=1024, tn=1024):
    M = x2.shape[0]
    ns = w_small.shape[1]
    return pl.pallas_call(
        _proj_kernel,
        grid=(M // tm, NP // tn),
        in_specs=[
            pl.BlockSpec((tm, D_MODEL), lambda i, j: (i, 0)),
            pl.BlockSpec((1, D_MODEL), lambda i, j: (0, 0)),
            pl.BlockSpec((D_MODEL, tn), lambda i, j: (0, j)),
            pl.BlockSpec((D_MODEL, ns), lambda i, j: (0, 0)),
        ],
        out_specs=[pl.BlockSpec((tm, tn), lambda i, j: (i, j)), pl.BlockSpec((tm, ns), lambda i, j: (i, 0))],
        out_shape=[jax.ShapeDtypeStruct((M, NP), BF16), jax.ShapeDtypeStruct((M, ns), F32)],
        scratch_shapes=[pltpu.VMEM((tm, D_MODEL), BF16)],
        compiler_params=_cp(("parallel", "arbitrary")),
        name="proj",
    )(x2, gain, w_all, w_small)


def _gla_kernel(q_ref, k_ref, v_ref, gate_ref, lr_ref, wa_ref, ba_ref, gn_ref, o_ref, st_ref, g_ref, *, n_chunks):
    C, SB, H, DK, DV = GLA_CHUNK, GLA_SUB, GLA_HEADS, GLA_DK, GLA_DV

    @pl.when(pl.program_id(1) == 0)
    def _():
        st_ref[...] = jnp.zeros_like(st_ref)

    gn = gn_ref[...]
    ri = lax.broadcasted_iota(jnp.int32, (C, C), 0)
    ci = lax.broadcasted_iota(jnp.int32, (C, C), 1)
    tri = (ri >= ci).astype(F32)
    si = lax.broadcasted_iota(jnp.int32, (SB, SB), 0)
    sj = lax.broadcasted_iota(jnp.int32, (SB, SB), 1)
    sub_causal = si >= sj

    z = jnp.dot(lr_ref[0], wa_ref[...], precision=HIGHEST, preferred_element_type=F32) + ba_ref[...]
    la = -(jnp.maximum(-z, 0.0) + jnp.log(1.0 + jnp.exp(-jnp.abs(z)))) * (1.0 / GLA_TAU)
    for c in range(n_chunks):
        g_ref[c * C:(c + 1) * C, :] = jnp.dot(tri, la[c * C:(c + 1) * C], precision=HIGHEST, preferred_element_type=F32)

    def chunk(c, carry):
        rows = pl.ds(pl.multiple_of(c * C, C), C)
        for h in range(H):
            kl = slice(h * DK, (h + 1) * DK)
            vl = slice(h * DV, (h + 1) * DV)
            q = q_ref[0, rows, kl].astype(F32) * (DK ** -0.5)
            k = k_ref[0, rows, kl].astype(F32)
            vb = v_ref[0, rows, vl]
            g = g_ref[rows, kl]

            outs = []
            for i in range(C // SB):
                lo = i * SB
                gi = g[lo:lo + SB]
                qi = q[lo:lo + SB]
                ki = k[lo:lo + SB]
                d = gi[:, None, :] - gi[None, :, :]
                e = jnp.exp(jnp.minimum(d, 0.0))
                a_d = jnp.sum(qi[:, None, :] * ki[None, :, :] * e, axis=-1)
                a_d = jnp.where(sub_causal, a_d, 0.0)
                o_i = jnp.dot(a_d.astype(BF16), vb[lo:lo + SB], preferred_element_type=F32)
                if i > 0:
                    r = g[lo:lo + 1]
                    qt = qi * jnp.exp(gi - r)
                    kt = k[:lo] * jnp.exp(r - g[:lo])
                    a_o = lax.dot_general(qt.astype(BF16), kt.astype(BF16), NT, preferred_element_type=F32)
                    o_i = o_i + jnp.dot(a_o.astype(BF16), vb[:lo], preferred_element_type=F32)
                outs.append(o_i)
            o = jnp.concatenate(outs, axis=0)

            st = st_ref[h]
            qg = q * jnp.exp(g)
            o = o + lax.dot_general(qg.astype(BF16), st.astype(BF16), NT, preferred_element_type=F32)
            gl = g[C - 1:C]
            kd = k * jnp.exp(gl - g)
            st_ref[h] = st * jnp.exp(gl) + lax.dot_general(vb, kd.astype(BF16), TN, preferred_element_type=F32)

            ms = jnp.mean(o * o, axis=-1, keepdims=True)
            y = o * lax.rsqrt(ms + EPS) * gn
            o_ref[0, rows, vl] = (y * _silu(gate_ref[0, rows, vl].astype(F32))).astype(o_ref.dtype)
        return carry

    lax.fori_loop(0, n_chunks, chunk, 0)


def _gla(proj3, small3, wa, ba, gn, tr=512):
    B, S, _ = proj3.shape
    H, DK, DV = GLA_HEADS, GLA_DK, GLA_DV
    kern = functools.partial(_gla_kernel, n_chunks=tr // GLA_CHUNK)
    return pl.pallas_call(
        kern,
        grid=(B, S // tr),
        in_specs=[
            pl.BlockSpec((1, tr, H * DK), lambda b, r: (b, r, C_GQ // (H * DK))),
            pl.BlockSpec((1, tr, H * DK), lambda b, r: (b, r, C_GK // (H * DK))),
            pl.BlockSpec((1, tr, H * DV), lambda b, r: (b, r, C_GV // (H * DV))),
            pl.BlockSpec((1, tr, H * DV), lambda b, r: (b, r, C_GG // (H * DV))),
            pl.BlockSpec((1, tr, 128), lambda b, r: (b, r, CS_LR // 128)),
            pl.BlockSpec((128, H * DK), lambda b, r: (0, 0)),
            pl.BlockSpec((1, H * DK), lambda b, r: (0, 0)),
            pl.BlockSpec((1, DV), lambda b, r: (0, 0)),
        ],
        out_specs=pl.BlockSpec((1, tr, H * DV), lambda b, r: (b, r, 0)),
        out_shape=jax.ShapeDtypeStruct((B, S, H * DV), BF16),
        scratch_shapes=[pltpu.VMEM((H, DV, DK), F32), pltpu.VMEM((tr, H * DK), F32)],
        compiler_params=_cp(("parallel", "arbitrary")),
        name="gla",
    )(proj3, proj3, proj3, proj3, small3, wa, ba, gn)


def _group_meansq(x, gmat):
    sq = x * x
    hi = sq.astype(BF16)
    lo = (sq - hi.astype(F32)).astype(BF16)
    s = jnp.dot(hi, gmat, preferred_element_type=F32) + jnp.dot(lo, gmat, preferred_element_type=F32)
    return s * (1.0 / NSA_HD)


def _rope(x, cos, sin, first_half):
    w = x.shape[-1]
    rot = jnp.where(first_half, -pltpu.roll(x, w - NSA_HD // 2, 1), pltpu.roll(x, NSA_HD // 2, 1))
    return x * cos + rot * sin


def _nsa_prep_kernel(q_ref, kvc_ref, kvs_ref, kvw_ref, pos_ref, inv_ref, qn_ref, kn_ref, gm_ref,
                     qr_ref, kc_ref, vc_ref, ks_ref, vst_ref, vwt_ref):
    tr = q_ref.shape[1]
    W = NSA_KV_HEADS * NSA_HD
    HD = NSA_HD
    gmat = gm_ref[...]
    ang = pos_ref[0].astype(F32) * inv_ref[...]
    cos1, sin1 = jnp.cos(ang), jnp.sin(ang)
    cos = jnp.concatenate([cos1, cos1], axis=1)
    sin = jnp.concatenate([sin1, sin1], axis=1)
    lane = lax.broadcasted_iota(jnp.int32, (tr, W), 1)
    first_half = (lane % HD) < (HD // 2)

    def norm_rope(x, gain):
        y = x * lax.rsqrt(_group_meansq(x, gmat) + EPS) * gain
        return _rope(y, cos, sin, first_half)

    qn = qn_ref[...]
    for s in range(NSA_HEADS * HD // W):
        xq = q_ref[0, :, s * W:(s + 1) * W].astype(F32)
        qr_ref[0, :, s * W:(s + 1) * W] = (norm_rope(xq, qn) * (HD ** -0.5 * LOG2E)).astype(qr_ref.dtype)

    kc = norm_rope(kvc_ref[0, :, :W].astype(F32), kn_ref[0:1, :]).astype(BF16)
    vc = kvc_ref[0, :, W:]
    for h in range(NSA_KV_HEADS):
        kc_ref[0, h] = kc[:, h * HD:(h + 1) * HD]
        vc_ref[0, h] = vc[:, h * HD:(h + 1) * HD]

    ks = norm_rope(kvs_ref[0, :, :W].astype(F32), kn_ref[1:2, :]).astype(BF16)
    kw = norm_rope(kvw_ref[0, :, :W].astype(F32), kn_ref[2:3, :]).astype(BF16)
    row_blk = (pl.program_id(1) * tr + lax.broadcasted_iota(jnp.int32, (tr, SLC_LANES), 0)) // SLC_BLOCK
    onehot = (row_blk == lax.broadcasted_iota(jnp.int32, (tr, SLC_LANES), 1)).astype(BF16)
    vst = kvs_ref[0, :, W:].astype(F32).T
    for h in range(NSA_KV_HEADS):
        ks_ref[0, h] = jnp.concatenate([onehot, ks[:, h * HD:(h + 1) * HD], kw[:, h * HD:(h + 1) * HD]], axis=1)
        for j in range(tr // SLC_CHUNK):
            vst_ref[0, h, j, :HD] = vst[h * HD:(h + 1) * HD, j * SLC_CHUNK:(j + 1) * SLC_CHUNK].astype(BF16)
            vst_ref[0, h, j, HD:] = jnp.ones((VT_ROWS - HD, SLC_CHUNK), BF16)

    vwt = kvw_ref[0, :, W:].astype(F32).T
    for h in range(NSA_KV_HEADS):
        for j in range(tr // WIN_ALIGN):
            vwt_ref[0, h, j, :HD] = vwt[h * HD:(h + 1) * HD, j * WIN_ALIGN:(j + 1) * WIN_ALIGN].astype(BF16)
            vwt_ref[0, h, j, HD:] = jnp.ones((VT_ROWS - HD, WIN_ALIGN), BF16)


def _nsa_prep(proj3, pos3, inv, qn, kn, gmat, tr=512):
    B, S, _ = proj3.shape
    Hk, HD = NSA_KV_HEADS, NSA_HD
    hm = lambda w, dt: jax.ShapeDtypeStruct((B, Hk, S, w), dt)
    hspec = lambda w: pl.BlockSpec((1, Hk, tr, w), lambda b, r: (b, 0, r, 0))
    tspec = lambda c, rows=HD: pl.BlockSpec((1, Hk, tr // c, rows, c), lambda b, r: (b, 0, r, 0, 0))
    return pl.pallas_call(
        _nsa_prep_kernel,
        grid=(B, S // tr),
        in_specs=[
            pl.BlockSpec((1, tr, 1024), lambda b, r: (b, r, C_NQ // 1024)),
            pl.BlockSpec((1, tr, 512), lambda b, r: (b, r, C_NKV // 512)),
            pl.BlockSpec((1, tr, 512), lambda b, r: (b, r, C_NKV // 512 + 1)),
            pl.BlockSpec((1, tr, 512), lambda b, r: (b, r, C_NKV // 512 + 2)),
            pl.BlockSpec((1, tr, 1), lambda b, r: (b, r, 0)),
            pl.BlockSpec((1, 128), lambda b, r: (0, 0)),
            pl.BlockSpec((1, 256), lambda b, r: (0, 0)),
            pl.BlockSpec((3, 256), lambda b, r: (0, 0)),
            pl.BlockSpec((256, 256), lambda b, r: (0, 0)),
        ],
        out_specs=[pl.BlockSpec((1, tr, 1024), lambda b, r: (b, r, 0)), hspec(HD), hspec(HD), hspec(KAUG),
                   tspec(SLC_CHUNK, VT_ROWS), tspec(WIN_ALIGN, VT_ROWS)],
        out_shape=[jax.ShapeDtypeStruct((B, S, 1024), BF16), hm(HD, BF16), hm(HD, BF16), hm(KAUG, BF16),
                   jax.ShapeDtypeStruct((B, Hk, S // SLC_CHUNK, VT_ROWS, SLC_CHUNK), BF16),
                   jax.ShapeDtypeStruct((B, Hk, S // WIN_ALIGN, VT_ROWS, WIN_ALIGN), BF16)],
        compiler_params=_cp(("parallel", "parallel")),
        name="nsa_prep",
    )(proj3, proj3, proj3, proj3, pos3, inv, qn, kn, gmat)


def _compress_kernel(t_ref, pe_ref, w1a_ref, w1b_ref, w2_ref, *rest, transposed):
    tail_ref, o_ref = rest if transposed else (None,) + rest
    t = t_ref[0, 0]
    w1a, w1b = w1a_ref[...], w1b_ref[...]
    u = jnp.dot(t, w1a, preferred_element_type=F32)
    v = jnp.dot(t, w1b, preferred_element_type=F32)
    pe = pe_ref[...].astype(BF16)
    c = (jnp.dot(pe[0:8], w1a, preferred_element_type=F32) + jnp.dot(pe[8:16], w1b, preferred_element_type=F32))[0:1]
    n = v.shape[0]
    h = u + pltpu.roll(v, n - 1, 0) + c
    h = jax.nn.gelu(h).astype(BF16)
    if transposed:
        hd = w2_ref.shape[0]
        o_ref[0, 0, :hd] = lax.dot_general(w2_ref[...], h, NT, preferred_element_type=F32).astype(o_ref.dtype)
        o_ref[0, 0, hd:] = tail_ref[...]
    else:
        o_ref[0, 0] = jnp.dot(h, w2_ref[...], preferred_element_type=F32).astype(o_ref.dtype)


def _compress(tok, pe2, w1a, w1b, w2, tail=None):
    B, Hk, S, HD = tok.shape
    n = S // CMP_STRIDE
    t2 = tok.reshape(B, Hk, n, CMP_STRIDE * HD)
    transposed = tail is not None
    oshape = (HD + tail.shape[0], n) if transposed else (n, HD)
    w2 = w2.T if transposed else w2
    extra = ([tail], [pl.BlockSpec(tail.shape, lambda b, h: (0, 0))]) if transposed else ([], [])
    return pl.pallas_call(
        functools.partial(_compress_kernel, transposed=transposed),
        grid=(B, Hk),
        in_specs=[
            pl.BlockSpec((1, 1, n, CMP_STRIDE * HD), lambda b, h: (b, h, 0, 0)),
            pl.BlockSpec((16, CMP_STRIDE * HD), lambda b, h: (0, 0)),
            pl.BlockSpec((CMP_STRIDE * HD, CMP_HIDDEN), lambda b, h: (0, 0)),
            pl.BlockSpec((CMP_STRIDE * HD, CMP_HIDDEN), lambda b, h: (0, 0)),
            pl.BlockSpec(w2.shape, lambda b, h: (0, 0)),
        ] + extra[1],
        out_specs=pl.BlockSpec((1, 1) + oshape, lambda b, h: (b, h, 0, 0)),
        out_shape=jax.ShapeDtypeStruct((B, Hk) + oshape, BF16),
        compiler_params=_cp(("parallel", "parallel")),
        name="compress_v" if transposed else "compress_k",
    )(t2, pe2, w1a, w1b, w2, *extra[0])


def _nsa_attn_kernel(q_ref, kc_ref, vct_ref, ks_ref, vst_ref, vwt_ref, gt_ref, gate_ref, tri_ref, o_ref,
                     s_ref, e_ref):
    G, HD, QB, NH, NB = NSA_GROUP, NSA_HD, SLC_BLOCK, NSA_STEP_HEADS, NSA_STEP_BLOCKS
    R = NB * G * QB
    heads = range(NH)
    hk0 = pl.program_id(1) * NH
    step = pl.program_id(2)
    t0 = step * (NB * QB)
    lane = lax.broadcasted_iota(jnp.int32, (1, R), 1)
    t = t0 + (lane // (G * QB)) * QB + lane % QB
    q = [jnp.concatenate([q_ref[0, a * QB:(a + 1) * QB, (h * G + g) * HD:(h * G + g + 1) * HD]
                          for a in range(NB) for g in range(G)], axis=0)
         for h in heads]

    nc = kc_ref.shape[2]
    valid = lax.broadcasted_iota(jnp.int32, (nc, 1), 0) <= ((t - (CMP_LEN - 1)) >> CMP_SHIFT)
    for h in heads:
        s = lax.dot_general(kc_ref[0, h], q[h], NT, preferred_element_type=F32)
        sm = jnp.where(valid, s, NEG)
        m = jnp.maximum(jnp.max(sm, axis=0, keepdims=True), 0.1 * NEG)
        e_ref[h] = jnp.exp2(sm - m).astype(BF16)
    blk = lax.broadcasted_iota(jnp.int32, (SLC_LANES, 2 * QB), 0)
    qlane = lax.broadcasted_iota(jnp.int32, (1, 2 * QB), 1)
    cur = step * NB + (qlane // QB) % NB
    forced = (blk == 0) | (blk == cur) | (blk == cur - 1)
    o_cmp, score = [], []
    for h in heads:
        oc = jnp.dot(vct_ref[0, h], e_ref[h], preferred_element_type=F32)
        den = oc[HD:HD + 1]
        rden = 1.0 / jnp.where(den > 0.0, den, 1.0)
        o_cmp.append(oc[:HD] * rden)
        impf = oc[VT_ROWS:] * rden
        parts = []
        for a in range(NB):
            pa = impf[:, a * G * QB:(a + 1) * G * QB]
            p2 = pa[:, :2 * QB] + pa[:, 2 * QB:]
            parts.append(p2 + pltpu.roll(p2, QB, 1))
        imp = parts[0] if NB == 1 else jnp.where(qlane < QB, parts[0], parts[1])
        score.append(jnp.where(forced, -jnp.inf, jnp.where(blk <= cur, imp, NEG)))

    wc = jnp.maximum(t0 - WINDOW, 0) // WIN_ALIGN
    w0 = pl.multiple_of(wc * WIN_ALIGN, WIN_ALIGN)
    kp = w0 + lax.broadcasted_iota(jnp.int32, (WIN_KEYS, 1), 0)
    win_ok = (kp <= t) & (kp > t - WINDOW)
    o_win = []
    for h in heads:
        q_win = jnp.concatenate([jnp.zeros((R, SLC_LANES + HD), BF16), q[h]], axis=1)
        sw = lax.dot_general(ks_ref[0, h, pl.ds(w0, WIN_KEYS), :], q_win, NT, preferred_element_type=F32)
        sw = jnp.where(win_ok, sw, NEG)
        pw = jnp.exp2(sw - jnp.max(sw, axis=0, keepdims=True))
        vv = jnp.concatenate([vwt_ref[0, h, wc + j] for j in range(WIN_KEYS // WIN_ALIGN)], axis=1)
        ow = jnp.dot(vv, pw.astype(BF16), preferred_element_type=F32)
        o_win.append(ow[:HD] * (1.0 / ow[HD:HD + 1]))

    few = cur < SLC_TOPN
    causal = blk <= cur
    blkf = blk.astype(F32)

    def pick(sc):
        mx = jnp.max(sc, axis=0, keepdims=True)
        first = jnp.min(jnp.where(sc == mx, blkf, float(SLC_LANES)), axis=0, keepdims=True)
        return jnp.where(blkf == first, -jnp.inf, sc)

    for _ in range(SLC_TOPN - 3):
        score = [pick(sc) for sc in score]
    q_aug = []
    for h in heads:
        chosen = forced | ((few | (score[h] == -jnp.inf)) & causal)
        bias_t = jnp.where(chosen, 0.0, NEG)
        bias = bias_t.T.astype(BF16)
        rows = jnp.concatenate([bias[a * QB:(a + 1) * QB] for a in range(NB) for g in range(G)], axis=0)
        q_aug.append(jnp.concatenate([rows, q[h], jnp.zeros((R, HD), BF16)], axis=1))

    def scores(c, slot):
        k0 = pl.multiple_of(c * SLC_CHUNK, SLC_CHUNK)
        for h in heads:
            s_ref[slot, h] = lax.dot_general(ks_ref[0, h, pl.ds(k0, SLC_CHUNK), :], q_aug[h], NT, preferred_element_type=F32)

    def absorb(c, slot, carry):
        out = []
        for h in heads:
            m_i, acc = carry[h]
            sc = s_ref[slot, h]
            m_new = jnp.maximum(m_i, jnp.max(sc, axis=0, keepdims=True))
            pp = jnp.exp2(sc - m_new).astype(BF16)
            acc = jnp.exp2(m_i - m_new) * acc + jnp.dot(vst_ref[0, h, c], pp, preferred_element_type=F32)
            out.append((m_new, acc))
        return tuple(out)

    diag = t0 // SLC_CHUNK
    n_pairs = diag // 2
    scores(0, 0)

    def pair(j, carry):
        scores(2 * j + 1, 1)
        carry = absorb(2 * j, 0, carry)
        scores(2 * j + 2, 0)
        return absorb(2 * j + 1, 1, carry)

    init = tuple((jnp.full((1, R), NEG, F32), jnp.zeros((VT_ROWS, R), F32)) for h in heads)
    carry = lax.fori_loop(0, n_pairs, pair, init)
    scores(2 * n_pairs + 1, 1)
    r0 = pl.multiple_of(t0 - diag * SLC_CHUNK, NB * QB)
    tri = tri_ref[...]
    for h in heads:
        s_ref[diag % 2, h, pl.ds(r0, NB * QB), :] += tri
    carry = absorb(2 * n_pairs, 0, carry)
    carry = absorb(2 * n_pairs + 1, 1, carry)

    sig = _sigmoid(gt_ref[0])
    glane = lax.broadcasted_iota(jnp.int32, (QB, 128), 1)
    outs = [[] for a in range(NB)]
    for h in heads:
        acc_s = carry[h][1]
        oc, os_, ow = o_cmp[h].T, (acc_s[:HD] * (1.0 / acc_s[HD:HD + 1])).T, o_win[h].T
        for a in range(NB):
            sig_a = sig[a * QB:(a + 1) * QB]
            for g in range(G):
                base = ((hk0 + h) * G + g) * N_NSA_BRANCH
                gc, gs, gw = [jnp.sum(jnp.where(glane == base + b, sig_a, 0.0), axis=-1, keepdims=True) for b in range(3)]
                r = slice((a * G + g) * QB, (a * G + g + 1) * QB)
                outs[a].append(gc * oc[r] + gs * os_[r] + gw * ow[r])
    o = jnp.concatenate([jnp.concatenate(oa, axis=1) for oa in outs], axis=0)
    o_ref[0] = (o * _silu(gate_ref[0].astype(F32))).astype(o_ref.dtype)


def _nsa_attn(qr, kc, vct, ks, vst, vwt, proj3, small3):
    B, S, _ = qr.shape
    Hk, HD, NH = NSA_KV_HEADS, NSA_HD, NSA_STEP_HEADS
    TQ = NSA_STEP_BLOCKS * SLC_BLOCK
    r = np.arange(TQ)[:, None]
    ln = np.arange(NSA_GROUP * TQ)[None, :]
    a, qi = ln // (NSA_GROUP * SLC_BLOCK), ln % SLC_BLOCK
    tri = jnp.asarray(np.where((r // SLC_BLOCK != a) | (r % SLC_BLOCK <= qi), 0.0, NEG), F32)
    nc = kc.shape[2]
    W = NH * NSA_GROUP * HD
    full = lambda a: pl.BlockSpec((1, NH) + a.shape[2:], lambda b, h, t: (b, h) + (0,) * (a.ndim - 2),
                                  pipeline_mode=pl.Buffered(1))
    return pl.pallas_call(
        _nsa_attn_kernel,
        grid=(B, Hk // NH, S // TQ),
        in_specs=[
            pl.BlockSpec((1, TQ, W), lambda b, h, t: (b, t, h)),
            full(kc), full(vct), full(ks), full(vst), full(vwt),
            pl.BlockSpec((1, TQ, 128), lambda b, h, t: (b, t, CS_NSG // 128)),
            pl.BlockSpec((1, TQ, W), lambda b, h, t: (b, t, C_NGATE // W + h)),
            pl.BlockSpec(tri.shape, lambda b, h, t: (0, 0), pipeline_mode=pl.Buffered(1)),
        ],
        out_specs=pl.BlockSpec((1, TQ, W), lambda b, h, t: (b, t, h)),
        out_shape=jax.ShapeDtypeStruct((B, S, NSA_HEADS * HD), BF16),
        scratch_shapes=[pltpu.VMEM((2, NH, SLC_CHUNK, NSA_GROUP * TQ), F32), pltpu.VMEM((NH, nc, NSA_GROUP * TQ), BF16)],
        compiler_params=_cp(("parallel", "parallel", "arbitrary")),
        name="nsa_attn",
    )(qr, kc, vct, ks, vst, vwt, small3, proj3, tri)


def _head_rms(x, gain, hd):
    outs = []
    for h in range(x.shape[-1] // hd):
        xh = x[:, h * hd:(h + 1) * hd]
        ms = jnp.mean(xh * xh, axis=-1, keepdims=True)
        outs.append(xh * lax.rsqrt(ms + EPS) * gain)
    return jnp.concatenate(outs, axis=1)


def _mem_prep_kernel(mem_ref, g_ref, w_ref, kn_ref, mk_ref, mv_ref):
    x = mem_ref[0]
    ms = jnp.mean(x * x, axis=-1, keepdims=True)
    xn = (x * lax.rsqrt(ms + EPS) * g_ref[...]).astype(BF16)
    kv = jnp.dot(xn, w_ref[...], preferred_element_type=F32)
    W = MEM_HEADS * MEM_HD
    mk_ref[0] = _head_rms(kv[:, :W], kn_ref[...], MEM_HD).astype(mk_ref.dtype)
    mv_ref[0] = kv[:, W:].astype(mv_ref.dtype)


def _mem_prep(mem, gain, w_kv, kn):
    B, N, D = mem.shape
    W = MEM_HEADS * MEM_HD
    return pl.pallas_call(
        _mem_prep_kernel,
        grid=(B,),
        in_specs=[
            pl.BlockSpec((1, N, D), lambda b: (b, 0, 0)),
            pl.BlockSpec((1, D), lambda b: (0, 0)),
            pl.BlockSpec((D, 2 * W), lambda b: (0, 0)),
            pl.BlockSpec((1, MEM_HD), lambda b: (0, 0)),
        ],
        out_specs=[pl.BlockSpec((1, N, W), lambda b: (b, 0, 0))] * 2,
        out_shape=[jax.ShapeDtypeStruct((B, N, W), BF16)] * 2,
        compiler_params=_cp(("parallel",)),
        name="mem_prep",
    )(mem, gain, w_kv, kn)


def _final_kernel(x_ref, ya_ref, yb_ref, mq_ref, mg_ref, mr_ref, mk_ref, mv_ref, qn_ref, wb_ref, wo_ref, o_ref):
    mq = _head_rms(mq_ref[...].astype(F32), qn_ref[...], MEM_HD)
    mk = mk_ref[0]
    mv = mv_ref[0]
    heads = []
    for h in range(MEM_HEADS):
        sl = slice(h * MEM_HD, (h + 1) * MEM_HD)
        s = lax.dot_general(mq[:, sl].astype(BF16), mk[:, sl], NT, preferred_element_type=F32) * (MEM_HD ** -0.5)
        m = jnp.max(s, axis=-1, keepdims=True)
        p = jnp.exp(s - m)
        o = jnp.dot(p.astype(BF16), mv[:, sl], preferred_element_type=F32) / jnp.sum(p, axis=-1, keepdims=True)
        heads.append(o)
    ym = jnp.concatenate(heads, axis=1) * _silu(mg_ref[...].astype(F32))

    mixed = None
    for c, y in enumerate((ya_ref[...], yb_ref[...], ym.astype(BF16))):
        z = jnp.dot(y, wb_ref[c], preferred_element_type=F32)
        term = _sigmoid(mr_ref[:, c * D_MODEL:(c + 1) * D_MODEL].astype(F32)) * z
        mixed = term if mixed is None else mixed + term
    o_ref[...] = x_ref[...] + jnp.dot(mixed.astype(BF16), wo_ref[...], preferred_element_type=F32)


def _final(x2, ya2, yb2, proj2, mk, mv, qn, wb, wo, S, tr=512):
    M, D = x2.shape
    N = mk.shape[1]
    nb = S // tr
    row = lambda c: pl.BlockSpec((tr, D), lambda i: (i, c))
    return pl.pallas_call(
        _final_kernel,
        grid=(M // tr,),
        in_specs=[
            row(0), row(0), row(0),
            row(C_MQ // D), row(C_MG // D),
            pl.BlockSpec((tr, 3 * D), lambda i: (i, C_MERGE // (3 * D))),
            pl.BlockSpec((1, N, D), lambda i: (i // nb, 0, 0)),
            pl.BlockSpec((1, N, D), lambda i: (i // nb, 0, 0)),
            pl.BlockSpec((1, MEM_HD), lambda i: (0, 0)),
            pl.BlockSpec((3, D, D), lambda i: (0, 0, 0), pipeline_mode=pl.Buffered(1)),
            pl.BlockSpec((D, D), lambda i: (0, 0), pipeline_mode=pl.Buffered(1)),
        ],
        out_specs=pl.BlockSpec((tr, D), lambda i: (i, 0)),
        out_shape=jax.ShapeDtypeStruct((M, D), F32),
        compiler_params=_cp(("parallel",)),
        name="final",
    )(x2, ya2, yb2, proj2, proj2, proj2, mk, mv, qn, wb, wo)


def _overlap_matrix_t(S):
    n_cmp = (S - CMP_LEN) // CMP_STRIDE + 1
    n_slc = S // SLC_BLOCK
    cs = np.arange(n_cmp)[:, None] * CMP_STRIDE
    ss = np.arange(n_slc)[None, :] * SLC_BLOCK
    ov = np.clip(np.minimum(cs + CMP_LEN, ss + SLC_BLOCK) - np.maximum(cs, ss), 0, None) / CMP_LEN
    out = np.zeros((SLC_LANES, S // CMP_STRIDE), np.float32)
    out[:n_slc, :n_cmp] = ov.T
    return jnp.asarray(out, BF16)


def _pad_cols(w, n):
    return jnp.pad(w, ((0, 0), (0, n - w.shape[1])))


def _layer(x, mem, positions, norm_gain, mem_norm_gain, w_in, w_gla_alpha, b_gla_alpha, gla_out_norm,
           nsa_q_norm, nsa_k_norm, pe_cmp_k, pe_cmp_v, w_cmp_k1, w_cmp_k2, w_cmp_v1, w_cmp_v2,
           w_mem_kv, mem_q_norm, mem_k_norm, w_branch, w_out):
    B, S, D = x.shape
    assert D == D_MODEL and S % (2 * SLC_CHUNK) == 0 and S >= WIN_KEYS and S // SLC_BLOCK <= SLC_LANES

    o = np.cumsum([0, 512, 512, 1024, 16, 1024, 1024, 1536, 48, 1024, 1024, 1024, 3072])
    sec = [w_in[:, o[i]:o[i + 1]] for i in range(12)]
    gq, gk, gv, glr, gg, nq, nkv, nsg, ngate, mq, mg, merge = sec
    w_all = jnp.concatenate([gv, gg, nq, ngate, mq, mg, merge, nkv, gq, gk], axis=1)
    w_all = _pad_cols(w_all, NP).astype(BF16)
    w_small = jnp.concatenate([_pad_cols(glr, 128), _pad_cols(nsg, 128)], axis=1).astype(BF16)

    x2 = x.reshape(B * S, D)
    proj2, small2 = _proj(x2, norm_gain.reshape(1, D), w_all, w_small)
    proj3 = proj2.reshape(B, S, NP)
    small3 = small2.reshape(B, S, NP_SMALL)

    wa = jnp.pad(w_gla_alpha, ((0, 128 - GLA_RANK), (0, 0)))
    ya = _gla(proj3, small3, wa, b_gla_alpha.reshape(1, -1), gla_out_norm.reshape(1, -1))

    half = NSA_HD // 2
    inv = ROPE_THETA ** (-jnp.arange(half, dtype=F32) / half)
    inv = jnp.tile(inv, 128 // half).reshape(1, 128)
    qn = jnp.tile(nsa_q_norm, NSA_KV_HEADS).reshape(1, -1)
    kn = jnp.tile(nsa_k_norm, (1, NSA_KV_HEADS))
    gid = np.arange(256) // NSA_HD
    gmat = jnp.asarray(gid[:, None] == gid[None, :], BF16)
    qr, kc_tok, vc_tok, ks, vst, vwt = _nsa_prep(proj3, positions.reshape(B, S, 1), inv, qn, kn, gmat)

    def cmp_args(pe, w1, w2):
        pe2 = pe.reshape(2, CMP_STRIDE * NSA_HD)
        pe16 = jnp.concatenate([jnp.broadcast_to(pe2[0:1], (8, pe2.shape[1])), jnp.broadcast_to(pe2[1:2], (8, pe2.shape[1]))], 0)
        w1f = w1.reshape(CMP_LEN * NSA_HD, CMP_HIDDEN).astype(BF16)
        return pe16, w1f[:CMP_STRIDE * NSA_HD], w1f[CMP_STRIDE * NSA_HD:], w2.astype(BF16)

    kc = _compress(kc_tok, *cmp_args(pe_cmp_k, w_cmp_k1, w_cmp_k2))
    tail = jnp.concatenate([jnp.ones((VT_ROWS - NSA_HD, S // CMP_STRIDE), BF16), _overlap_matrix_t(S)], axis=0)
    vct = _compress(vc_tok, *cmp_args(pe_cmp_v, w_cmp_v1, w_cmp_v2), tail=tail)
    yb = _nsa_attn(qr, kc, vct, ks, vst, vwt, proj3, small3)

    mk, mv = _mem_prep(mem, mem_norm_gain.reshape(1, D), w_mem_kv.astype(BF16), mem_k_norm.reshape(1, -1))
    out = _final(x2, ya.reshape(B * S, D), yb.reshape(B * S, D), proj2, mk, mv, mem_q_norm.reshape(1, -1),
                 w_branch.astype(BF16), w_out.astype(BF16), S)
    return out.reshape(B, S, D)


def kernel(x, mem, positions, norm_gain, mem_norm_gain, w_in, w_gla_alpha, b_gla_alpha, gla_out_norm, nsa_q_norm, nsa_k_norm, pe_cmp_k, pe_cmp_v, w_cmp_k1, w_cmp_k2, w_cmp_v1, w_cmp_v2, w_mem_kv, mem_q_norm, mem_k_norm, w_branch, w_out):
    h = x
    for l in range(norm_gain.shape[0]):
        h = _layer(h, mem, positions, norm_gain[l], mem_norm_gain[l], w_in[l], w_gla_alpha[l], b_gla_alpha[l],
                   gla_out_norm[l], nsa_q_norm[l], nsa_k_norm[l], pe_cmp_k[l], pe_cmp_v[l], w_cmp_k1[l],
                   w_cmp_k2[l], w_cmp_v1[l], w_cmp_v2[l], w_mem_kv[l], mem_q_norm[l], mem_k_norm[l],
                   w_branch[l], w_out[l])
    return h
```

```python
import functools

import numpy as np
import jax
import jax.numpy as jnp
from jax import lax
from jax.experimental import pallas as pl
from jax.experimental.pallas import tpu as pltpu

F32 = jnp.float32
BF16 = jnp.bfloat16
HIGHEST = lax.Precision.HIGHEST

D_MODEL = 1024
ROPE_THETA = 10000.0
EPS = 1e-6
NEG = -1e30

GLA_HEADS = 4
GLA_DK = 128
GLA_DV = 256
GLA_RANK = 16
GLA_TAU = 16.0
GLA_CHUNK = 64
GLA_SUB = 16

NSA_HEADS = 16
NSA_KV_HEADS = 4
NSA_GROUP = 4
NSA_HD = 64
CMP_LEN = 32
CMP_SHIFT = 4
CMP_STRIDE = 1 << CMP_SHIFT
CMP_HIDDEN = 256
SLC_BLOCK = 64
SLC_TOPN = 16
WINDOW = 512
N_NSA_BRANCH = 3
SLC_LANES = 128
SLC_CHUNK = 512
VT_ROWS = NSA_HD + 16
LOG2E = 1.4426950408889634
WIN_ALIGN = 128
NSA_STEP_HEADS = 2
NSA_STEP_BLOCKS = 4
WIN_KEYS = WINDOW + max(NSA_STEP_BLOCKS * SLC_BLOCK, WIN_ALIGN)
KAUG = SLC_LANES + 2 * NSA_HD

MEM_HEADS = 4
MEM_HD = 256

C_GV, C_GG, C_NQ, C_NGATE, C_MQ, C_MG, C_MERGE = 0, 1024, 2048, 3072, 4096, 5120, 6144
C_NKV, C_GQ, C_GK = 9216, 10752, 11264
NP = 12288
CS_LR, CS_NSG, NP_SMALL = 0, 128, 256

VMEM_LIMIT = 48 * 1024 * 1024

NT = (((1,), (1,)), ((), ()))
TN = (((0,), (0,)), ((), ()))


def _cp(sem):
    return pltpu.CompilerParams(dimension_semantics=sem, vmem_limit_bytes=VMEM_LIMIT)


def _silu(x):
    return x * (1.0 / (1.0 + jnp.exp(-x)))


def _sigmoid(x):
    return 1.0 / (1.0 + jnp.exp(-x))


def _proj_kernel(x_ref, g_ref, w_ref, ws_ref, o_ref, os_ref, xn_ref):
    @pl.when(pl.program_id(1) == 0)
    def _():
        x = x_ref[...]
        ms = jnp.mean(x * x, axis=-1, keepdims=True)
        xn = (x * lax.rsqrt(ms + EPS) * g_ref[...]).astype(BF16)
        xn_ref[...] = xn
        os_ref[...] = jnp.dot(xn, ws_ref[...], preferred_element_type=F32)

    o_ref[...] = jnp.dot(xn_ref[...], w_ref[...], preferred_element_type=F32).astype(o_ref.dtype)


def _proj(x2, gain, w_all, w_small, tm=1024, tn=2048):
    M = x2.shape[0]
    ns = w_small.shape[1]
    return pl.pallas_call(
        _proj_kernel,
        grid=(M // tm, NP // tn),
        in_specs=[
            pl.BlockSpec((tm, D_MODEL), lambda i, j: (i, 0)),
            pl.BlockSpec((1, D_MODEL), lambda i, j: (0, 0)),
            pl.BlockSpec((D_MODEL, tn), lambda i, j: (0, j)),
            pl.BlockSpec((D_MODEL, ns), lambda i, j: (0, 0)),
        ],
        out_specs=[pl.BlockSpec((tm, tn), lambda i, j: (i, j)), pl.BlockSpec((tm, ns), lambda i, j: (i, 0))],
        out_shape=[jax.ShapeDtypeStruct((M, NP), BF16), jax.ShapeDtypeStruct((M, ns), F32)],
        scratch_shapes=[pltpu.VMEM((tm, D_MODEL), BF16)],
        compiler_params=_cp(("parallel", "arbitrary")),
        name="proj",
    )(x2, gain, w_all, w_small)


def _gla_kernel(q_ref, k_ref, v_ref, gate_ref, lr_ref, wa_ref, ba_ref, gn_ref, o_ref, st_ref, g_ref, *, n_chunks):
    C, SB, H, DK, DV = GLA_CHUNK, GLA_SUB, GLA_HEADS, GLA_DK, GLA_DV

    @pl.when(pl.program_id(1) == 0)
    def _():
        st_ref[...] = jnp.zeros_like(st_ref)

    gn = gn_ref[...]
    ri = lax.broadcasted_iota(jnp.int32, (C, C), 0)
    ci = lax.broadcasted_iota(jnp.int32, (C, C), 1)
    tri = (ri >= ci).astype(F32)
    si = lax.broadcasted_iota(jnp.int32, (SB, SB), 0)
    sj = lax.broadcasted_iota(jnp.int32, (SB, SB), 1)
    sub_causal = si >= sj

    z = jnp.dot(lr_ref[0], wa_ref[...], precision=HIGHEST, preferred_element_type=F32) + ba_ref[...]
    la = -(jnp.maximum(-z, 0.0) + jnp.log(1.0 + jnp.exp(-jnp.abs(z)))) * (1.0 / GLA_TAU)
    for c in range(n_chunks):
        g_ref[c * C:(c + 1) * C, :] = jnp.dot(tri, la[c * C:(c + 1) * C], precision=HIGHEST, preferred_element_type=F32)

    def chunk(c, carry):
        rows = pl.ds(pl.multiple_of(c * C, C), C)
        for h in range(H):
            kl = slice(h * DK, (h + 1) * DK)
            vl = slice(h * DV, (h + 1) * DV)
            q = q_ref[0, rows, kl].astype(F32) * (DK ** -0.5)
            k = k_ref[0, rows, kl].astype(F32)
            vb = v_ref[0, rows, vl]
            g = g_ref[rows, kl]

            outs = []
            for i in range(C // SB):
                lo = i * SB
                gi = g[lo:lo + SB]
                qi = q[lo:lo + SB]
                ki = k[lo:lo + SB]
                d = gi[:, None, :] - gi[None, :, :]
                e = jnp.exp(jnp.minimum(d, 0.0))
                a_d = jnp.sum(qi[:, None, :] * ki[None, :, :] * e, axis=-1)
                a_d = jnp.where(sub_causal, a_d, 0.0)
                o_i = jnp.dot(a_d.astype(BF16), vb[lo:lo + SB], preferred_element_type=F32)
                if i > 0:
                    r = g[lo:lo + 1]
                    qt = qi * jnp.exp(gi - r)
                    kt = k[:lo] * jnp.exp(r - g[:lo])
                    a_o = lax.dot_general(qt.astype(BF16), kt.astype(BF16), NT, preferred_element_type=F32)
                    o_i = o_i + jnp.dot(a_o.astype(BF16), vb[:lo], preferred_element_type=F32)
                outs.append(o_i)
            o = jnp.concatenate(outs, axis=0)

            st = st_ref[h]
            qg = q * jnp.exp(g)
            o = o + lax.dot_general(qg.astype(BF16), st.astype(BF16), NT, preferred_element_type=F32)
            gl = g[C - 1:C]
            kd = k * jnp.exp(gl - g)
            st_ref[h] = st * jnp.exp(gl) + lax.dot_general(vb, kd.astype(BF16), TN, preferred_element_type=F32)

            ms = jnp.mean(o * o, axis=-1, keepdims=True)
            y = o * lax.rsqrt(ms + EPS) * gn
            o_ref[0, rows, vl] = (y * _silu(gate_ref[0, rows, vl].astype(F32))).astype(o_ref.dtype)
        return carry

    lax.fori_loop(0, n_chunks, chunk, 0)


def _gla(proj3, small3, wa, ba, gn, tr=512):
    B, S, _ = proj3.shape
    H, DK, DV = GLA_HEADS, GLA_DK, GLA_DV
    kern = functools.partial(_gla_kernel, n_chunks=tr // GLA_CHUNK)
    return pl.pallas_call(
        kern,
        grid=(B, S // tr),
        in_specs=[
            pl.BlockSpec((1, tr, H * DK), lambda b, r: (b, r, C_GQ // (H * DK))),
            pl.BlockSpec((1, tr, H * DK), lambda b, r: (b, r, C_GK // (H * DK))),
            pl.BlockSpec((1, tr, H * DV), lambda b, r: (b, r, C_GV // (H * DV))),
            pl.BlockSpec((1, tr, H * DV), lambda b, r: (b, r, C_GG // (H * DV))),
            pl.BlockSpec((1, tr, 128), lambda b, r: (b, r, CS_LR // 128)),
            pl.BlockSpec((128, H * DK), lambda b, r: (0, 0)),
            pl.BlockSpec((1, H * DK), lambda b, r: (0, 0)),
            pl.BlockSpec((1, DV), lambda b, r: (0, 0)),
        ],
        out_specs=pl.BlockSpec((1, tr, H * DV), lambda b, r: (b, r, 0)),
        out_shape=jax.ShapeDtypeStruct((B, S, H * DV), BF16),
        scratch_shapes=[pltpu.VMEM((H, DV, DK), F32), pltpu.VMEM((tr, H * DK), F32)],
        compiler_params=_cp(("parallel", "arbitrary")),
        name="gla",
    )(proj3, proj3, proj3, proj3, small3, wa, ba, gn)


def _group_meansq(x, gmat):
    sq = x * x
    hi = sq.astype(BF16)
    lo = (sq - hi.astype(F32)).astype(BF16)
    s = jnp.dot(hi, gmat, preferred_element_type=F32) + jnp.dot(lo, gmat, preferred_element_type=F32)
    return s * (1.0 / NSA_HD)


def _rope(x, cos, sin, first_half):
    w = x.shape[-1]
    rot = jnp.where(first_half, -pltpu.roll(x, w - NSA_HD // 2, 1), pltpu.roll(x, NSA_HD // 2, 1))
    return x * cos + rot * sin


def _nsa_prep_kernel(q_ref, kvc_ref, kvs_ref, kvw_ref, pos_ref, inv_ref, qn_ref, kn_ref, gm_ref,
                     qr_ref, kc_ref, vc_ref, ks_ref, vst_ref, vwt_ref):
    tr = q_ref.shape[1]
    W = NSA_KV_HEADS * NSA_HD
    HD = NSA_HD
    gmat = gm_ref[...]
    ang = pos_ref[0].astype(F32) * inv_ref[...]
    cos1, sin1 = jnp.cos(ang), jnp.sin(ang)
    cos = jnp.concatenate([cos1, cos1], axis=1)
    sin = jnp.concatenate([sin1, sin1], axis=1)
    lane = lax.broadcasted_iota(jnp.int32, (tr, W), 1)
    first_half = (lane % HD) < (HD // 2)

    def norm_rope(x, gain):
        y = x * lax.rsqrt(_group_meansq(x, gmat) + EPS) * gain
        return _rope(y, cos, sin, first_half)

    qn = qn_ref[...]
    for s in range(NSA_HEADS * HD // W):
        xq = q_ref[0, :, s * W:(s + 1) * W].astype(F32)
        qr_ref[0, :, s * W:(s + 1) * W] = (norm_rope(xq, qn) * (HD ** -0.5 * LOG2E)).astype(qr_ref.dtype)

    kc = norm_rope(kvc_ref[0, :, :W].astype(F32), kn_ref[0:1, :]).astype(BF16)
    vc = kvc_ref[0, :, W:]
    for h in range(NSA_KV_HEADS):
        kc_ref[0, h] = kc[:, h * HD:(h + 1) * HD]
        vc_ref[0, h] = vc[:, h * HD:(h + 1) * HD]

    ks = norm_rope(kvs_ref[0, :, :W].astype(F32), kn_ref[1:2, :]).astype(BF16)
    kw = norm_rope(kvw_ref[0, :, :W].astype(F32), kn_ref[2:3, :]).astype(BF16)
    row_blk = (pl.program_id(1) * tr + lax.broadcasted_iota(jnp.int32, (tr, SLC_LANES), 0)) // SLC_BLOCK
    onehot = (row_blk == lax.broadcasted_iota(jnp.int32, (tr, SLC_LANES), 1)).astype(BF16)
    vst = kvs_ref[0, :, W:].astype(F32).T
    for h in range(NSA_KV_HEADS):
        ks_ref[0, h] = jnp.concatenate([onehot, ks[:, h * HD:(h + 1) * HD], kw[:, h * HD:(h + 1) * HD]], axis=1)
        for j in range(tr // SLC_CHUNK):
            vst_ref[0, h, j, :HD] = vst[h * HD:(h + 1) * HD, j * SLC_CHUNK:(j + 1) * SLC_CHUNK].astype(BF16)
            vst_ref[0, h, j, HD:] = jnp.ones((VT_ROWS - HD, SLC_CHUNK), BF16)

    vwt = kvw_ref[0, :, W:].astype(F32).T
    for h in range(NSA_KV_HEADS):
        for j in range(tr // WIN_ALIGN):
            vwt_ref[0, h, j, :HD] = vwt[h * HD:(h + 1) * HD, j * WIN_ALIGN:(j + 1) * WIN_ALIGN].astype(BF16)
            vwt_ref[0, h, j, HD:] = jnp.ones((VT_ROWS - HD, WIN_ALIGN), BF16)


def _nsa_prep(proj3, pos3, inv, qn, kn, gmat, tr=512):
    B, S, _ = proj3.shape
    Hk, HD = NSA_KV_HEADS, NSA_HD
    hm = lambda w, dt: jax.ShapeDtypeStruct((B, Hk, S, w), dt)
    hspec = lambda w: pl.BlockSpec((1, Hk, tr, w), lambda b, r: (b, 0, r, 0))
    tspec = lambda c, rows=HD: pl.BlockSpec((1, Hk, tr // c, rows, c), lambda b, r: (b, 0, r, 0, 0))
    return pl.pallas_call(
        _nsa_prep_kernel,
        grid=(B, S // tr),
        in_specs=[
            pl.BlockSpec((1, tr, 1024), lambda b, r: (b, r, C_NQ // 1024)),
            pl.BlockSpec((1, tr, 512), lambda b, r: (b, r, C_NKV // 512)),
            pl.BlockSpec((1, tr, 512), lambda b, r: (b, r, C_NKV // 512 + 1)),
            pl.BlockSpec((1, tr, 512), lambda b, r: (b, r, C_NKV // 512 + 2)),
            pl.BlockSpec((1, tr, 1), lambda b, r: (b, r, 0)),
            pl.BlockSpec((1, 128), lambda b, r: (0, 0)),
            pl.BlockSpec((1, 256), lambda b, r: (0, 0)),
            pl.BlockSpec((3, 256), lambda b, r: (0, 0)),
            pl.BlockSpec((256, 256), lambda b, r: (0, 0)),
        ],
        out_specs=[pl.BlockSpec((1, tr, 1024), lambda b, r: (b, r, 0)), hspec(HD), hspec(HD), hspec(KAUG),
                   tspec(SLC_CHUNK, VT_ROWS), tspec(WIN_ALIGN, VT_ROWS)],
        out_shape=[jax.ShapeDtypeStruct((B, S, 1024), BF16), hm(HD, BF16), hm(HD, BF16), hm(KAUG, BF16),
                   jax.ShapeDtypeStruct((B, Hk, S // SLC_CHUNK, VT_ROWS, SLC_CHUNK), BF16),
                   jax.ShapeDtypeStruct((B, Hk, S // WIN_ALIGN, VT_ROWS, WIN_ALIGN), BF16)],
        compiler_params=_cp(("parallel", "parallel")),
        name="nsa_prep",
    )(proj3, proj3, proj3, proj3, pos3, inv, qn, kn, gmat)


def _compress_kernel(t_ref, pe_ref, w1a_ref, w1b_ref, w2_ref, *rest, transposed):
    tail_ref, o_ref = rest if transposed else (None,) + rest
    t = t_ref[0, 0]
    w1a, w1b = w1a_ref[...], w1b_ref[...]
    u = jnp.dot(t, w1a, preferred_element_type=F32)
    v = jnp.dot(t, w1b, preferred_element_type=F32)
    pe = pe_ref[...].astype(BF16)
    c = (jnp.dot(pe[0:8], w1a, preferred_element_type=F32) + jnp.dot(pe[8:16], w1b, preferred_element_type=F32))[0:1]
    n = v.shape[0]
    h = u + pltpu.roll(v, n - 1, 0) + c
    h = jax.nn.gelu(h).astype(BF16)
    if transposed:
        hd = w2_ref.shape[0]
        o_ref[0, 0, :hd] = lax.dot_general(w2_ref[...], h, NT, preferred_element_type=F32).astype(o_ref.dtype)
        o_ref[0, 0, hd:] = tail_ref[...]
    else:
        o_ref[0, 0] = jnp.dot(h, w2_ref[...], preferred_element_type=F32).astype(o_ref.dtype)


def _compress(tok, pe2, w1a, w1b, w2, tail=None):
    B, Hk, S, HD = tok.shape
    n = S // CMP_STRIDE
    t2 = tok.reshape(B, Hk, n, CMP_STRIDE * HD)
    transposed = tail is not None
    oshape = (HD + tail.shape[0], n) if transposed else (n, HD)
    w2 = w2.T if transposed else w2
    extra = ([tail], [pl.BlockSpec(tail.shape, lambda b, h: (0, 0))]) if transposed else ([], [])
    return pl.pallas_call(
        functools.partial(_compress_kernel, transposed=transposed),
        grid=(B, Hk),
        in_specs=[
            pl.BlockSpec((1, 1, n, CMP_STRIDE * HD), lambda b, h: (b, h, 0, 0)),
            pl.BlockSpec((16, CMP_STRIDE * HD), lambda b, h: (0, 0)),
            pl.BlockSpec((CMP_STRIDE * HD, CMP_HIDDEN), lambda b, h: (0, 0)),
            pl.BlockSpec((CMP_STRIDE * HD, CMP_HIDDEN), lambda b, h: (0, 0)),
            pl.BlockSpec(w2.shape, lambda b, h: (0, 0)),
        ] + extra[1],
        out_specs=pl.BlockSpec((1, 1) + oshape, lambda b, h: (b, h, 0, 0)),
        out_shape=jax.ShapeDtypeStruct((B, Hk) + oshape, BF16),
        compiler_params=_cp(("parallel", "parallel")),
        name="compress_v" if transposed else "compress_k",
    )(t2, pe2, w1a, w1b, w2, *extra[0])


def _nsa_attn_kernel(q_ref, kc_ref, vct_ref, ks_ref, vst_ref, vwt_ref, gt_ref, gate_ref, tri_ref, o_ref,
                     s_ref, e_ref):
    G, HD, QB, NH, NB = NSA_GROUP, NSA_HD, SLC_BLOCK, NSA_STEP_HEADS, NSA_STEP_BLOCKS
    R = NB * G * QB
    heads = range(NH)
    hk0 = pl.program_id(1) * NH
    step = pl.program_id(2)
    t0 = step * (NB * QB)
    lane = lax.broadcasted_iota(jnp.int32, (1, R), 1)
    t = t0 + (lane // (G * QB)) * QB + lane % QB
    q = [jnp.concatenate([q_ref[0, a * QB:(a + 1) * QB, (h * G + g) * HD:(h * G + g + 1) * HD]
                          for a in range(NB) for g in range(G)], axis=0)
         for h in heads]

    nc = kc_ref.shape[2]
    valid = lax.broadcasted_iota(jnp.int32, (nc, 1), 0) <= ((t - (CMP_LEN - 1)) >> CMP_SHIFT)
    for h in heads:
        s = lax.dot_general(kc_ref[0, h], q[h], NT, preferred_element_type=F32)
        sm = jnp.where(valid, s, NEG)
        m = jnp.maximum(jnp.max(sm, axis=0, keepdims=True), 0.1 * NEG)
        e_ref[h] = jnp.exp2(sm - m).astype(BF16)
    QL = max(NB * QB, 2 * QB)
    blk = lax.broadcasted_iota(jnp.int32, (SLC_LANES, QL), 0)
    qlane = lax.broadcasted_iota(jnp.int32, (1, QL), 1)
    cur = step * NB + (qlane // QB) % NB
    forced = (blk == 0) | (blk == cur) | (blk == cur - 1)
    o_cmp, score = [], []
    for h in heads:
        oc = jnp.dot(vct_ref[0, h], e_ref[h], preferred_element_type=F32)
        den = oc[HD:HD + 1]
        rden = 1.0 / jnp.where(den > 0.0, den, 1.0)
        o_cmp.append(oc[:HD] * rden)
        impf = oc[VT_ROWS:] * rden
        parts = []
        for a in range(NB):
            pa = impf[:, a * G * QB:(a + 1) * G * QB]
            p2 = pa[:, :2 * QB] + pa[:, 2 * QB:]
            parts.append(p2 + pltpu.roll(p2, QB, 1))
        first_half = qlane[:, :2 * QB] < QB
        imp = parts[0] if NB == 1 else jnp.concatenate(
            [jnp.where(first_half, parts[a], parts[a + 1]) for a in range(0, NB, 2)], axis=1)
        score.append(jnp.where(forced, -jnp.inf, jnp.where(blk <= cur, imp, NEG)))

    wc = jnp.maximum(t0 - WINDOW, 0) // WIN_ALIGN
    w0 = pl.multiple_of(wc * WIN_ALIGN, WIN_ALIGN)
    kp = w0 + lax.broadcasted_iota(jnp.int32, (WIN_KEYS, 1), 0)
    win_ok = (kp <= t) & (kp > t - WINDOW)
    o_win = []
    for h in heads:
        q_win = jnp.concatenate([jnp.zeros((R, SLC_LANES + HD), BF16), q[h]], axis=1)
        sw = lax.dot_general(ks_ref[0, h, pl.ds(w0, WIN_KEYS), :], q_win, NT, preferred_element_type=F32)
        sw = jnp.where(win_ok, sw, NEG)
        pw = jnp.exp2(sw - jnp.max(sw, axis=0, keepdims=True))
        vv = jnp.concatenate([vwt_ref[0, h, wc + j] for j in range(WIN_KEYS // WIN_ALIGN)], axis=1)
        ow = jnp.dot(vv, pw.astype(BF16), preferred_element_type=F32)
        o_win.append(ow[:HD] * (1.0 / ow[HD:HD + 1]))

    few = cur < SLC_TOPN
    causal = blk <= cur
    blkf = blk.astype(F32)

    def pick(sc):
        mx = jnp.max(sc, axis=0, keepdims=True)
        first = jnp.min(jnp.where(sc == mx, blkf, float(SLC_LANES)), axis=0, keepdims=True)
        return jnp.where(blkf == first, -jnp.inf, sc)

    for _ in range(SLC_TOPN - 3):
        score = [pick(sc) for sc in score]
    q_aug = []
    for h in heads:
        chosen = forced | ((few | (score[h] == -jnp.inf)) & causal)
        bias_t = jnp.where(chosen, 0.0, NEG)
        bias = bias_t.T.astype(BF16)
        rows = jnp.concatenate([bias[a * QB:(a + 1) * QB] for a in range(NB) for g in range(G)], axis=0)
        q_aug.append(jnp.concatenate([rows, q[h], jnp.zeros((R, HD), BF16)], axis=1))

    def scores(c, slot):
        k0 = pl.multiple_of(c * SLC_CHUNK, SLC_CHUNK)
        for h in heads:
            s_ref[slot, h] = lax.dot_general(ks_ref[0, h, pl.ds(k0, SLC_CHUNK), :], q_aug[h], NT, preferred_element_type=F32)

    def absorb(c, slot, carry):
        out = []
        for h in heads:
            m_i, acc = carry[h]
            sc = s_ref[slot, h]
            m_new = jnp.maximum(m_i, jnp.max(sc, axis=0, keepdims=True))
            pp = jnp.exp2(sc - m_new).astype(BF16)
            acc = jnp.exp2(m_i - m_new) * acc + jnp.dot(vst_ref[0, h, c], pp, preferred_element_type=F32)
            out.append((m_new, acc))
        return tuple(out)

    diag = t0 // SLC_CHUNK
    n_pairs = diag // 2
    scores(0, 0)

    def pair(j, carry):
        scores(2 * j + 1, 1)
        carry = absorb(2 * j, 0, carry)
        scores(2 * j + 2, 0)
        return absorb(2 * j + 1, 1, carry)

    init = tuple((jnp.full((1, R), NEG, F32), jnp.zeros((VT_ROWS, R), F32)) for h in heads)
    carry = lax.fori_loop(0, n_pairs, pair, init)
    scores(2 * n_pairs + 1, 1)
    r0 = pl.multiple_of(t0 - diag * SLC_CHUNK, NB * QB)
    tri = tri_ref[...]
    for h in heads:
        s_ref[diag % 2, h, pl.ds(r0, NB * QB), :] += tri
    carry = absorb(2 * n_pairs, 0, carry)
    carry = absorb(2 * n_pairs + 1, 1, carry)

    sig = _sigmoid(gt_ref[0])
    glane = lax.broadcasted_iota(jnp.int32, (QB, 128), 1)
    outs = [[] for a in range(NB)]
    for h in heads:
        acc_s = carry[h][1]
        oc, os_, ow = o_cmp[h].T, (acc_s[:HD] * (1.0 / acc_s[HD:HD + 1])).T, o_win[h].T
        for a in range(NB):
            sig_a = sig[a * QB:(a + 1) * QB]
            for g in range(G):
                base = ((hk0 + h) * G + g) * N_NSA_BRANCH
                gc, gs, gw = [jnp.sum(jnp.where(glane == base + b, sig_a, 0.0), axis=-1, keepdims=True) for b in range(3)]
                r = slice((a * G + g) * QB, (a * G + g + 1) * QB)
                outs[a].append(gc * oc[r] + gs * os_[r] + gw * ow[r])
    o = jnp.concatenate([jnp.concatenate(oa, axis=1) for oa in outs], axis=0)
    o_ref[0] = (o * _silu(gate_ref[0].astype(F32))).astype(o_ref.dtype)


def _nsa_attn(qr, kc, vct, ks, vst, vwt, proj3, small3):
    B, S, _ = qr.shape
    Hk, HD, NH = NSA_KV_HEADS, NSA_HD, NSA_STEP_HEADS
    TQ = NSA_STEP_BLOCKS * SLC_BLOCK
    r = np.arange(TQ)[:, None]
    ln = np.arange(NSA_GROUP * TQ)[None, :]
    a, qi = ln // (NSA_GROUP * SLC_BLOCK), ln % SLC_BLOCK
    tri = jnp.asarray(np.where((r // SLC_BLOCK != a) | (r % SLC_BLOCK <= qi), 0.0, NEG), F32)
    nc = kc.shape[2]
    W = NH * NSA_GROUP * HD
    full = lambda a: pl.BlockSpec((1, NH) + a.shape[2:], lambda b, h, t: (b, h) + (0,) * (a.ndim - 2),
                                  pipeline_mode=pl.Buffered(1))
    return pl.pallas_call(
        _nsa_attn_kernel,
        grid=(B, Hk // NH, S // TQ),
        in_specs=[
            pl.BlockSpec((1, TQ, W), lambda b, h, t: (b, t, h)),
            full(kc), full(vct), full(ks), full(vst), full(vwt),
            pl.BlockSpec((1, TQ, 128), lambda b, h, t: (b, t, CS_NSG // 128)),
            pl.BlockSpec((1, TQ, W), lambda b, h, t: (b, t, C_NGATE // W + h)),
            pl.BlockSpec(tri.shape, lambda b, h, t: (0, 0), pipeline_mode=pl.Buffered(1)),
        ],
        out_specs=pl.BlockSpec((1, TQ, W), lambda b, h, t: (b, t, h)),
        out_shape=jax.ShapeDtypeStruct((B, S, NSA_HEADS * HD), BF16),
        scratch_shapes=[pltpu.VMEM((2, NH, SLC_CHUNK, NSA_GROUP * TQ), F32), pltpu.VMEM((NH, nc, NSA_GROUP * TQ), BF16)],
        compiler_params=_cp(("parallel", "parallel", "arbitrary")),
        name="nsa_attn",
    )(qr, kc, vct, ks, vst, vwt, small3, proj3, tri)


def _head_rms(x, gain, hd):
    outs = []
    for h in range(x.shape[-1] // hd):
        xh = x[:, h * hd:(h + 1) * hd]
        ms = jnp.mean(xh * xh, axis=-1, keepdims=True)
        outs.append(xh * lax.rsqrt(ms + EPS) * gain)
    return jnp.concatenate(outs, axis=1)


def _mem_prep_kernel(mem_ref, g_ref, w_ref, kn_ref, mk_ref, mv_ref):
    x = mem_ref[0]
    ms = jnp.mean(x * x, axis=-1, keepdims=True)
    xn = (x * lax.rsqrt(ms + EPS) * g_ref[...]).astype(BF16)
    kv = jnp.dot(xn, w_ref[...], preferred_element_type=F32)
    W = MEM_HEADS * MEM_HD
    mk_ref[0] = _head_rms(kv[:, :W], kn_ref[...], MEM_HD).astype(mk_ref.dtype)
    mv_ref[0] = kv[:, W:].astype(mv_ref.dtype)


def _mem_prep(mem, gain, w_kv, kn):
    B, N, D = mem.shape
    W = MEM_HEADS * MEM_HD
    return pl.pallas_call(
        _mem_prep_kernel,
        grid=(B,),
        in_specs=[
            pl.BlockSpec((1, N, D), lambda b: (b, 0, 0)),
            pl.BlockSpec((1, D), lambda b: (0, 0)),
            pl.BlockSpec((D, 2 * W), lambda b: (0, 0)),
            pl.BlockSpec((1, MEM_HD), lambda b: (0, 0)),
        ],
        out_specs=[pl.BlockSpec((1, N, W), lambda b: (b, 0, 0))] * 2,
        out_shape=[jax.ShapeDtypeStruct((B, N, W), BF16)] * 2,
        compiler_params=_cp(("parallel",)),
        name="mem_prep",
    )(mem, gain, w_kv, kn)


def _final_kernel(x_ref, ya_ref, yb_ref, mq_ref, mg_ref, mr_ref, mk_ref, mv_ref, qn_ref, wb_ref, wo_ref, o_ref):
    mq = _head_rms(mq_ref[...].astype(F32), qn_ref[...], MEM_HD)
    mk = mk_ref[0]
    mv = mv_ref[0]
    heads = []
    for h in range(MEM_HEADS):
        sl = slice(h * MEM_HD, (h + 1) * MEM_HD)
        s = lax.dot_general(mq[:, sl].astype(BF16), mk[:, sl], NT, preferred_element_type=F32) * (MEM_HD ** -0.5)
        m = jnp.max(s, axis=-1, keepdims=True)
        p = jnp.exp(s - m)
        o = jnp.dot(p.astype(BF16), mv[:, sl], preferred_element_type=F32) / jnp.sum(p, axis=-1, keepdims=True)
        heads.append(o)
    ym = jnp.concatenate(heads, axis=1) * _silu(mg_ref[...].astype(F32))

    mixed = None
    for c, y in enumerate((ya_ref[...], yb_ref[...], ym.astype(BF16))):
        z = jnp.dot(y, wb_ref[c], preferred_element_type=F32)
        term = _sigmoid(mr_ref[:, c * D_MODEL:(c + 1) * D_MODEL].astype(F32)) * z
        mixed = term if mixed is None else mixed + term
    o_ref[...] = x_ref[...] + jnp.dot(mixed.astype(BF16), wo_ref[...], preferred_element_type=F32)


def _final(x2, ya2, yb2, proj2, mk, mv, qn, wb, wo, S, tr=512):
    M, D = x2.shape
    N = mk.shape[1]
    nb = S // tr
    row = lambda c: pl.BlockSpec((tr, D), lambda i: (i, c))
    return pl.pallas_call(
        _final_kernel,
        grid=(M // tr,),
        in_specs=[
            row(0), row(0), row(0),
            row(C_MQ // D), row(C_MG // D),
            pl.BlockSpec((tr, 3 * D), lambda i: (i, C_MERGE // (3 * D))),
            pl.BlockSpec((1, N, D), lambda i: (i // nb, 0, 0)),
            pl.BlockSpec((1, N, D), lambda i: (i // nb, 0, 0)),
            pl.BlockSpec((1, MEM_HD), lambda i: (0, 0)),
            pl.BlockSpec((3, D, D), lambda i: (0, 0, 0), pipeline_mode=pl.Buffered(1)),
            pl.BlockSpec((D, D), lambda i: (0, 0), pipeline_mode=pl.Buffered(1)),
        ],
        out_specs=pl.BlockSpec((tr, D), lambda i: (i, 0)),
        out_shape=jax.ShapeDtypeStruct((M, D), F32),
        compiler_params=_cp(("parallel",)),
        name="final",
    )(x2, ya2, yb2, proj2, proj2, proj2, mk, mv, qn, wb, wo)


def _overlap_matrix_t(S):
    n_cmp = (S - CMP_LEN) // CMP_STRIDE + 1
    n_slc = S // SLC_BLOCK
    cs = np.arange(n_cmp)[:, None] * CMP_STRIDE
    ss = np.arange(n_slc)[None, :] * SLC_BLOCK
    ov = np.clip(np.minimum(cs + CMP_LEN, ss + SLC_BLOCK) - np.maximum(cs, ss), 0, None) / CMP_LEN
    out = np.zeros((SLC_LANES, S // CMP_STRIDE), np.float32)
    out[:n_slc, :n_cmp] = ov.T
    return jnp.asarray(out, BF16)


def _pad_cols(w, n):
    return jnp.pad(w, ((0, 0), (0, n - w.shape[1])))


def _layer(x, mem, positions, norm_gain, mem_norm_gain, w_in, w_gla_alpha, b_gla_alpha, gla_out_norm,
           nsa_q_norm, nsa_k_norm, pe_cmp_k, pe_cmp_v, w_cmp_k1, w_cmp_k2, w_cmp_v1, w_cmp_v2,
           w_mem_kv, mem_q_norm, mem_k_norm, w_branch, w_out):
    B, S, D = x.shape
    assert D == D_MODEL and S % (2 * SLC_CHUNK) == 0 and S >= WIN_KEYS and S // SLC_BLOCK <= SLC_LANES
    assert SLC_TOPN % NSA_STEP_BLOCKS == 0 and SLC_CHUNK % (NSA_STEP_BLOCKS * SLC_BLOCK) == 0

    o = np.cumsum([0, 512, 512, 1024, 16, 1024, 1024, 1536, 48, 1024, 1024, 1024, 3072])
    sec = [w_in[:, o[i]:o[i + 1]] for i in range(12)]
    gq, gk, gv, glr, gg, nq, nkv, nsg, ngate, mq, mg, merge = sec
    wide = [w.astype(BF16) for w in (gv, gg, nq, ngate, mq, mg, merge, nkv, gq, gk)]
    w_all = jnp.concatenate(wide + [jnp.zeros((D, NP - sum(w.shape[1] for w in wide)), BF16)], axis=1)
    w_small = jnp.concatenate([_pad_cols(glr, 128), _pad_cols(nsg, 128)], axis=1).astype(BF16)

    x2 = x.reshape(B * S, D)
    proj2, small2 = _proj(x2, norm_gain.reshape(1, D), w_all, w_small)
    proj3 = proj2.reshape(B, S, NP)
    small3 = small2.reshape(B, S, NP_SMALL)

    wa = jnp.pad(w_gla_alpha, ((0, 128 - GLA_RANK), (0, 0)))
    ya = _gla(proj3, small3, wa, b_gla_alpha.reshape(1, -1), gla_out_norm.reshape(1, -1))

    half = NSA_HD // 2
    inv = ROPE_THETA ** (-jnp.arange(half, dtype=F32) / half)
    inv = jnp.tile(inv, 128 // half).reshape(1, 128)
    qn = jnp.tile(nsa_q_norm, NSA_KV_HEADS).reshape(1, -1)
    kn = jnp.tile(nsa_k_norm, (1, NSA_KV_HEADS))
    gid = np.arange(256) // NSA_HD
    gmat = jnp.asarray(gid[:, None] == gid[None, :], BF16)
    qr, kc_tok, vc_tok, ks, vst, vwt = _nsa_prep(proj3, positions.reshape(B, S, 1), inv, qn, kn, gmat)

    def cmp_args(pe, w1, w2):
        pe2 = pe.reshape(2, CMP_STRIDE * NSA_HD)
        pe16 = jnp.concatenate([jnp.broadcast_to(pe2[0:1], (8, pe2.shape[1])), jnp.broadcast_to(pe2[1:2], (8, pe2.shape[1]))], 0)
        w1f = w1.reshape(CMP_LEN * NSA_HD, CMP_HIDDEN).astype(BF16)
        return pe16, w1f[:CMP_STRIDE * NSA_HD], w1f[CMP_STRIDE * NSA_HD:], w2.astype(BF16)

    kc = _compress(kc_tok, *cmp_args(pe_cmp_k, w_cmp_k1, w_cmp_k2))
    tail = jnp.concatenate([jnp.ones((VT_ROWS - NSA_HD, S // CMP_STRIDE), BF16), _overlap_matrix_t(S)], axis=0)
    vct = _compress(vc_tok, *cmp_args(pe_cmp_v, w_cmp_v1, w_cmp_v2), tail=tail)
    yb = _nsa_attn(qr, kc, vct, ks, vst, vwt, proj3, small3)

    mk, mv = _mem_prep(mem, mem_norm_gain.reshape(1, D), w_mem_kv.astype(BF16), mem_k_norm.reshape(1, -1))
    out = _final(x2, ya.reshape(B * S, D), yb.reshape(B * S, D), proj2, mk, mv, mem_q_norm.reshape(1, -1),
                 w_branch.astype(BF16), w_out.astype(BF16), S)
    return out.reshape(B, S, D)


def kernel(x, mem, positions, norm_gain, mem_norm_gain, w_in, w_gla_alpha, b_gla_alpha, gla_out_norm, nsa_q_norm, nsa_k_norm, pe_cmp_k, pe_cmp_v, w_cmp_k1, w_cmp_k2, w_cmp_v1, w_cmp_v2, w_mem_kv, mem_q_norm, mem_k_norm, w_branch, w_out):
    h = x
    for l in range(norm_gain.shape[0]):
        h = _layer(h, mem, positions, norm_gain[l], mem_norm_gain[l], w_in[l], w_gla_alpha[l], b_gla_alpha[l],
                   gla_out_norm[l], nsa_q_norm[l], nsa_k_norm[l], pe_cmp_k[l], pe_cmp_v[l], w_cmp_k1[l],
                   w_cmp_k2[l], w_cmp_v1[l], w_cmp_v2[l], w_mem_kv[l], mem_q_norm[l], mem_k_norm[l],
                   w_branch[l], w_out[l])
    return h
```

```python
import functools

import numpy as np
import jax
import jax.numpy as jnp
from jax import lax
from jax.experimental import pallas as pl
from jax.experimental.pallas import tpu as pltpu

F32 = jnp.float32
BF16 = jnp.bfloat16

D_MODEL = 1024
ROPE_THETA = 10000.0
EPS = 1e-6
NEG = -1e30

GLA_HEADS = 4
GLA_DK = 128
GLA_DV = 256
GLA_RANK = 16
GLA_TAU = 16.0
GLA_CHUNK = 64
GLA_SUB = 16

NSA_HEADS = 16
NSA_KV_HEADS = 4
NSA_GROUP = 4
NSA_HD = 64
CMP_LEN = 32
CMP_SHIFT = 4
CMP_STRIDE = 1 << CMP_SHIFT
CMP_HIDDEN = 256
SLC_BLOCK = 64
SLC_TOPN = 16
WINDOW = 512
N_NSA_BRANCH = 3
SLC_LANES = 128
SLC_CHUNK = 512
VT_ROWS = NSA_HD + 16
LOG2E = 1.4426950408889634
WIN_ALIGN = 128
NSA_STEP_HEADS = 2
NSA_STEP_BLOCKS = 4
WIN_KEYS = WINDOW + max(NSA_STEP_BLOCKS * SLC_BLOCK, WIN_ALIGN)
KAUG = SLC_LANES + 2 * NSA_HD

MEM_HEADS = 4
MEM_HD = 256

C_GV, C_GG, C_NQ, C_NGATE, C_MQ, C_MG, C_MERGE = 0, 1024, 2048, 3072, 4096, 5120, 6144
C_NKV, C_GQ, C_GK = 9216, 10752, 11264
NP = 12288
CS_LR, CS_NSG, NP_SMALL = 0, 128, 256

VMEM_LIMIT = 48 * 1024 * 1024

NT = (((1,), (1,)), ((), ()))
TN = (((0,), (0,)), ((), ()))


def _cp(sem):
    return pltpu.CompilerParams(dimension_semantics=sem, vmem_limit_bytes=VMEM_LIMIT)


def _silu(x):
    return x * (1.0 / (1.0 + jnp.exp(-x)))


def _sigmoid(x):
    return 1.0 / (1.0 + jnp.exp(-x))


def _proj_kernel(x_ref, g_ref, w_ref, ws_ref, o_ref, os_ref, xn_ref):
    @pl.when(pl.program_id(1) == 0)
    def _():
        x = x_ref[...]
        ms = jnp.mean(x * x, axis=-1, keepdims=True)
        xn = (x * lax.rsqrt(ms + EPS) * g_ref[...]).astype(BF16)
        xn_ref[...] = xn
        os_ref[...] = jnp.dot(xn, ws_ref[...], preferred_element_type=F32)

    o_ref[...] = jnp.dot(xn_ref[...], w_ref[...], preferred_element_type=F32).astype(o_ref.dtype)


def _proj(x2, gain, w_all, w_small, tm=1024, tn=2048):
    M = x2.shape[0]
    ns = w_small.shape[1]
    return pl.pallas_call(
        _proj_kernel,
        grid=(M // tm, NP // tn),
        in_specs=[
            pl.BlockSpec((tm, D_MODEL), lambda i, j: (i, 0)),
            pl.BlockSpec((1, D_MODEL), lambda i, j: (0, 0)),
            pl.BlockSpec((D_MODEL, tn), lambda i, j: (0, j)),
            pl.BlockSpec((D_MODEL, ns), lambda i, j: (0, 0)),
        ],
        out_specs=[pl.BlockSpec((tm, tn), lambda i, j: (i, j)), pl.BlockSpec((tm, ns), lambda i, j: (i, 0))],
        out_shape=[jax.ShapeDtypeStruct((M, NP), BF16), jax.ShapeDtypeStruct((M, ns), F32)],
        scratch_shapes=[pltpu.VMEM((tm, D_MODEL), BF16)],
        compiler_params=_cp(("parallel", "arbitrary")),
        name="proj",
    )(x2, gain, w_all, w_small)


def _split2(x):
    hi = x.astype(BF16)
    return hi, (x - hi.astype(F32)).astype(BF16)


def _gla_kernel(q_ref, k_ref, v_ref, gate_ref, lr_ref, wa_ref, ba_ref, gn_ref, o_ref, st_ref, g_ref, kf_ref, *, n_chunks):
    C, SB, H, DK, DV = GLA_CHUNK, GLA_SUB, GLA_HEADS, GLA_DK, GLA_DV

    @pl.when(pl.program_id(1) == 0)
    def _():
        st_ref[...] = jnp.zeros_like(st_ref)

    gn = gn_ref[...]
    ri = lax.broadcasted_iota(jnp.int32, (C, C), 0)
    ci = lax.broadcasted_iota(jnp.int32, (C, C), 1)
    tri = (ri >= ci).astype(BF16)
    si = lax.broadcasted_iota(jnp.int32, (SB, SB), 0)
    sj = lax.broadcasted_iota(jnp.int32, (SB, SB), 1)
    sub_causal = si >= sj

    lr_hi, lr_lo = _split2(lr_ref[0])
    wa_hi, wa_lo = _split2(wa_ref[...])
    z = (jnp.dot(lr_hi, wa_hi, preferred_element_type=F32) + jnp.dot(lr_hi, wa_lo, preferred_element_type=F32)
         + jnp.dot(lr_lo, wa_hi, preferred_element_type=F32)) + ba_ref[...]
    la = -(jnp.maximum(-z, 0.0) + jnp.log(1.0 + jnp.exp(-jnp.abs(z)))) * (1.0 / GLA_TAU)
    la1 = la.astype(BF16)
    la2, la3 = _split2(la - la1.astype(F32))
    for c in range(n_chunks):
        r = slice(c * C, (c + 1) * C)
        g_ref[r, :] = (jnp.dot(tri, la1[r], preferred_element_type=F32) + jnp.dot(tri, la2[r], preferred_element_type=F32)
                       + jnp.dot(tri, la3[r], preferred_element_type=F32))

    kf_ref[...] = k_ref[0].astype(F32)

    def chunk(c, carry):
        rows = pl.ds(pl.multiple_of(c * C, C), C)
        for h in range(H):
            kl = slice(h * DK, (h + 1) * DK)
            vl = slice(h * DV, (h + 1) * DV)
            q = q_ref[0, rows, kl].astype(F32) * (DK ** -0.5)
            k = kf_ref[rows, kl]
            vb = v_ref[0, rows, vl]
            g = g_ref[rows, kl]

            outs = []
            for i in range(C // SB):
                lo = i * SB
                gi = g[lo:lo + SB]
                qi = q[lo:lo + SB]
                ki = k[lo:lo + SB]
                a_d = jnp.zeros((SB, SB), F32)
                for jj in range(SB):
                    row = slice(lo + jj, lo + jj + 1)
                    w = (qi * kf_ref.at[rows, kl][row, :]) * jnp.exp(jnp.minimum(gi - g_ref.at[rows, kl][row, :], 0.0))
                    a_d = jnp.where(sj == jj, jnp.sum(w, axis=-1, keepdims=True), a_d)
                a_d = jnp.where(sub_causal, a_d, 0.0)
                o_i = jnp.dot(a_d.astype(BF16), vb[lo:lo + SB], preferred_element_type=F32)
                if i > 0:
                    r = g[lo:lo + 1]
                    qt = qi * jnp.exp(gi - r)
                    kt = k[:lo] * jnp.exp(r - g[:lo])
                    a_o = lax.dot_general(qt.astype(BF16), kt.astype(BF16), NT, preferred_element_type=F32)
                    o_i = o_i + jnp.dot(a_o.astype(BF16), vb[:lo], preferred_element_type=F32)
                outs.append(o_i)
            o = jnp.concatenate(outs, axis=0)

            st = st_ref[h]
            qg = q * jnp.exp(g)
            o = o + lax.dot_general(qg.astype(BF16), st.astype(BF16), NT, preferred_element_type=F32)
            gl = g[C - 1:C]
            kd = k * jnp.exp(gl - g)
            st_ref[h] = st * jnp.exp(gl) + lax.dot_general(vb, kd.astype(BF16), TN, preferred_element_type=F32)

            ms = jnp.mean(o * o, axis=-1, keepdims=True)
            y = o * lax.rsqrt(ms + EPS) * gn
            o_ref[0, rows, vl] = (y * _silu(gate_ref[0, rows, vl].astype(F32))).astype(o_ref.dtype)
        return carry

    lax.fori_loop(0, n_chunks, chunk, 0)


def _gla(proj3, small3, wa, ba, gn, tr=512):
    B, S, _ = proj3.shape
    H, DK, DV = GLA_HEADS, GLA_DK, GLA_DV
    kern = functools.partial(_gla_kernel, n_chunks=tr // GLA_CHUNK)
    return pl.pallas_call(
        kern,
        grid=(B, S // tr),
        in_specs=[
            pl.BlockSpec((1, tr, H * DK), lambda b, r: (b, r, C_GQ // (H * DK))),
            pl.BlockSpec((1, tr, H * DK), lambda b, r: (b, r, C_GK // (H * DK))),
            pl.BlockSpec((1, tr, H * DV), lambda b, r: (b, r, C_GV // (H * DV))),
            pl.BlockSpec((1, tr, H * DV), lambda b, r: (b, r, C_GG // (H * DV))),
            pl.BlockSpec((1, tr, 128), lambda b, r: (b, r, CS_LR // 128)),
            pl.BlockSpec((128, H * DK), lambda b, r: (0, 0)),
            pl.BlockSpec((1, H * DK), lambda b, r: (0, 0)),
            pl.BlockSpec((1, DV), lambda b, r: (0, 0)),
        ],
        out_specs=pl.BlockSpec((1, tr, H * DV), lambda b, r: (b, r, 0)),
        out_shape=jax.ShapeDtypeStruct((B, S, H * DV), BF16),
        scratch_shapes=[pltpu.VMEM((H, DV, DK), F32), pltpu.VMEM((tr, H * DK), F32), pltpu.VMEM((tr, H * DK), F32)],
        compiler_params=_cp(("parallel", "arbitrary")),
        name="gla",
    )(proj3, proj3, proj3, proj3, small3, wa, ba, gn)


def _group_meansq(x, gmat):
    sq = x * x
    hi = sq.astype(BF16)
    lo = (sq - hi.astype(F32)).astype(BF16)
    s = jnp.dot(hi, gmat, preferred_element_type=F32) + jnp.dot(lo, gmat, preferred_element_type=F32)
    return s * (1.0 / NSA_HD)


def _rope(x, cos, sin, first_half):
    w = x.shape[-1]
    rot = jnp.where(first_half, -pltpu.roll(x, w - NSA_HD // 2, 1), pltpu.roll(x, NSA_HD // 2, 1))
    return x * cos + rot * sin


def _nsa_prep_kernel(q_ref, kvc_ref, kvs_ref, kvw_ref, pos_ref, inv_ref, qn_ref, kn_ref, gm_ref,
                     qr_ref, kc_ref, vc_ref, ks_ref, vst_ref, vwt_ref):
    tr = q_ref.shape[1]
    W = NSA_KV_HEADS * NSA_HD
    HD = NSA_HD
    gmat = gm_ref[...]
    ang = pos_ref[0].astype(F32) * inv_ref[...]
    cos1, sin1 = jnp.cos(ang), jnp.sin(ang)
    cos = jnp.concatenate([cos1, cos1], axis=1)
    sin = jnp.concatenate([sin1, sin1], axis=1)
    lane = lax.broadcasted_iota(jnp.int32, (tr, W), 1)
    first_half = (lane % HD) < (HD // 2)

    def norm_rope(x, gain):
        y = x * lax.rsqrt(_group_meansq(x, gmat) + EPS) * gain
        return _rope(y, cos, sin, first_half)

    qn = qn_ref[...]
    for s in range(NSA_HEADS * HD // W):
        xq = q_ref[0, :, s * W:(s + 1) * W].astype(F32)
        qr_ref[0, :, s * W:(s + 1) * W] = (norm_rope(xq, qn) * (HD ** -0.5 * LOG2E)).astype(qr_ref.dtype)

    kc = norm_rope(kvc_ref[0, :, :W].astype(F32), kn_ref[0:1, :]).astype(BF16)
    vc = kvc_ref[0, :, W:]
    for h in range(NSA_KV_HEADS):
        kc_ref[0, h] = kc[:, h * HD:(h + 1) * HD]
        vc_ref[0, h] = vc[:, h * HD:(h + 1) * HD]

    ks = norm_rope(kvs_ref[0, :, :W].astype(F32), kn_ref[1:2, :]).astype(BF16)
    kw = norm_rope(kvw_ref[0, :, :W].astype(F32), kn_ref[2:3, :]).astype(BF16)
    row_blk = (pl.program_id(1) * tr + lax.broadcasted_iota(jnp.int32, (tr, SLC_LANES), 0)) // SLC_BLOCK
    onehot = (row_blk == lax.broadcasted_iota(jnp.int32, (tr, SLC_LANES), 1)).astype(BF16)
    vst = kvs_ref[0, :, W:].astype(F32).T
    for h in range(NSA_KV_HEADS):
        ks_ref[0, h] = jnp.concatenate([onehot, ks[:, h * HD:(h + 1) * HD], kw[:, h * HD:(h + 1) * HD]], axis=1)
        for j in range(tr // SLC_CHUNK):
            vst_ref[0, h, j, :HD] = vst[h * HD:(h + 1) * HD, j * SLC_CHUNK:(j + 1) * SLC_CHUNK].astype(BF16)
            vst_ref[0, h, j, HD:] = jnp.ones((VT_ROWS - HD, SLC_CHUNK), BF16)

    vwt = kvw_ref[0, :, W:].astype(F32).T
    for h in range(NSA_KV_HEADS):
        for j in range(tr // WIN_ALIGN):
            vwt_ref[0, h, j, :HD] = vwt[h * HD:(h + 1) * HD, j * WIN_ALIGN:(j + 1) * WIN_ALIGN].astype(BF16)
            vwt_ref[0, h, j, HD:] = jnp.ones((VT_ROWS - HD, WIN_ALIGN), BF16)


def _nsa_prep(proj3, pos3, inv, qn, kn, gmat, tr=512):
    B, S, _ = proj3.shape
    Hk, HD = NSA_KV_HEADS, NSA_HD
    hm = lambda w, dt: jax.ShapeDtypeStruct((B, Hk, S, w), dt)
    hspec = lambda w: pl.BlockSpec((1, Hk, tr, w), lambda b, r: (b, 0, r, 0))
    tspec = lambda c, rows=HD: pl.BlockSpec((1, Hk, tr // c, rows, c), lambda b, r: (b, 0, r, 0, 0))
    return pl.pallas_call(
        _nsa_prep_kernel,
        grid=(B, S // tr),
        in_specs=[
            pl.BlockSpec((1, tr, 1024), lambda b, r: (b, r, C_NQ // 1024)),
            pl.BlockSpec((1, tr, 512), lambda b, r: (b, r, C_NKV // 512)),
            pl.BlockSpec((1, tr, 512), lambda b, r: (b, r, C_NKV // 512 + 1)),
            pl.BlockSpec((1, tr, 512), lambda b, r: (b, r, C_NKV // 512 + 2)),
            pl.BlockSpec((1, tr, 1), lambda b, r: (b, r, 0)),
            pl.BlockSpec((1, 128), lambda b, r: (0, 0)),
            pl.BlockSpec((1, 256), lambda b, r: (0, 0)),
            pl.BlockSpec((3, 256), lambda b, r: (0, 0)),
            pl.BlockSpec((256, 256), lambda b, r: (0, 0)),
        ],
        out_specs=[pl.BlockSpec((1, tr, 1024), lambda b, r: (b, r, 0)), hspec(HD), hspec(HD), hspec(KAUG),
                   tspec(SLC_CHUNK, VT_ROWS), tspec(WIN_ALIGN, VT_ROWS)],
        out_shape=[jax.ShapeDtypeStruct((B, S, 1024), BF16), hm(HD, BF16), hm(HD, BF16), hm(KAUG, BF16),
                   jax.ShapeDtypeStruct((B, Hk, S // SLC_CHUNK, VT_ROWS, SLC_CHUNK), BF16),
                   jax.ShapeDtypeStruct((B, Hk, S // WIN_ALIGN, VT_ROWS, WIN_ALIGN), BF16)],
        compiler_params=_cp(("parallel", "parallel")),
        name="nsa_prep",
    )(proj3, proj3, proj3, proj3, pos3, inv, qn, kn, gmat)


def _compress_kernel(t_ref, pe_ref, w1a_ref, w1b_ref, w2_ref, *rest, transposed):
    tail_ref, o_ref = rest if transposed else (None,) + rest
    t = t_ref[0, 0]
    w1a, w1b = w1a_ref[...], w1b_ref[...]
    u = jnp.dot(t, w1a, preferred_element_type=F32)
    v = jnp.dot(t, w1b, preferred_element_type=F32)
    pe = pe_ref[...].astype(BF16)
    c = (jnp.dot(pe[0:8], w1a, preferred_element_type=F32) + jnp.dot(pe[8:16], w1b, preferred_element_type=F32))[0:1]
    n = v.shape[0]
    h = u + pltpu.roll(v, n - 1, 0) + c
    h = jax.nn.gelu(h).astype(BF16)
    if transposed:
        hd = w2_ref.shape[0]
        o_ref[0, 0, :hd] = lax.dot_general(w2_ref[...], h, NT, preferred_element_type=F32).astype(o_ref.dtype)
        o_ref[0, 0, hd:] = tail_ref[...]
    else:
        o_ref[0, 0] = jnp.dot(h, w2_ref[...], preferred_element_type=F32).astype(o_ref.dtype)


def _compress(tok, pe2, w1a, w1b, w2, tail=None):
    B, Hk, S, HD = tok.shape
    n = S // CMP_STRIDE
    t2 = tok.reshape(B, Hk, n, CMP_STRIDE * HD)
    transposed = tail is not None
    oshape = (HD + tail.shape[0], n) if transposed else (n, HD)
    w2 = w2.T if transposed else w2
    extra = ([tail], [pl.BlockSpec(tail.shape, lambda b, h: (0, 0))]) if transposed else ([], [])
    return pl.pallas_call(
        functools.partial(_compress_kernel, transposed=transposed),
        grid=(B, Hk),
        in_specs=[
            pl.BlockSpec((1, 1, n, CMP_STRIDE * HD), lambda b, h: (b, h, 0, 0)),
            pl.BlockSpec((16, CMP_STRIDE * HD), lambda b, h: (0, 0)),
            pl.BlockSpec((CMP_STRIDE * HD, CMP_HIDDEN), lambda b, h: (0, 0)),
            pl.BlockSpec((CMP_STRIDE * HD, CMP_HIDDEN), lambda b, h: (0, 0)),
            pl.BlockSpec(w2.shape, lambda b, h: (0, 0)),
        ] + extra[1],
        out_specs=pl.BlockSpec((1, 1) + oshape, lambda b, h: (b, h, 0, 0)),
        out_shape=jax.ShapeDtypeStruct((B, Hk) + oshape, BF16),
        compiler_params=_cp(("parallel", "parallel")),
        name="compress_v" if transposed else "compress_k",
    )(t2, pe2, w1a, w1b, w2, *extra[0])


def _nsa_attn_kernel(q_ref, kc_ref, vct_ref, ks_ref, vst_ref, vwt_ref, gt_ref, gate_ref, tri_ref, o_ref,
                     s_ref, e_ref):
    G, HD, QB, NH, NB = NSA_GROUP, NSA_HD, SLC_BLOCK, NSA_STEP_HEADS, NSA_STEP_BLOCKS
    R = NB * G * QB
    heads = range(NH)
    hk0 = pl.program_id(1) * NH
    step = pl.program_id(2)
    t0 = step * (NB * QB)
    lane = lax.broadcasted_iota(jnp.int32, (1, R), 1)
    t = t0 + (lane // (G * QB)) * QB + lane % QB
    q = [jnp.concatenate([q_ref[0, a * QB:(a + 1) * QB, (h * G + g) * HD:(h * G + g + 1) * HD]
                          for a in range(NB) for g in range(G)], axis=0)
         for h in heads]

    nc = kc_ref.shape[2]
    valid = lax.broadcasted_iota(jnp.int32, (nc, 1), 0) <= ((t - (CMP_LEN - 1)) >> CMP_SHIFT)
    for h in heads:
        s = lax.dot_general(kc_ref[0, h], q[h], NT, preferred_element_type=F32)
        sm = jnp.where(valid, s, NEG)
        m = jnp.maximum(jnp.max(sm, axis=0, keepdims=True), 0.1 * NEG)
        e_ref[h] = jnp.exp2(sm - m).astype(BF16)
    QL = max(NB * QB, 2 * QB)
    blk = lax.broadcasted_iota(jnp.int32, (SLC_LANES, QL), 0)
    qlane = lax.broadcasted_iota(jnp.int32, (1, QL), 1)
    cur = step * NB + (qlane // QB) % NB
    forced = (blk == 0) | (blk == cur) | (blk == cur - 1)
    o_cmp, score = [], []
    for h in heads:
        oc = jnp.dot(vct_ref[0, h], e_ref[h], preferred_element_type=F32)
        den = oc[HD:HD + 1]
        rden = 1.0 / jnp.where(den > 0.0, den, 1.0)
        o_cmp.append(oc[:HD] * rden)
        impf = oc[VT_ROWS:] * rden
        parts = []
        for a in range(NB):
            pa = impf[:, a * G * QB:(a + 1) * G * QB]
            p2 = pa[:, :2 * QB] + pa[:, 2 * QB:]
            parts.append(p2 + pltpu.roll(p2, QB, 1))
        first_half = qlane[:, :2 * QB] < QB
        imp = parts[0] if NB == 1 else jnp.concatenate(
            [jnp.where(first_half, parts[a], parts[a + 1]) for a in range(0, NB, 2)], axis=1)
        score.append(jnp.where(forced, -jnp.inf, jnp.where(blk <= cur, imp, NEG)))

    wc = jnp.maximum(t0 - WINDOW, 0) // WIN_ALIGN
    w0 = pl.multiple_of(wc * WIN_ALIGN, WIN_ALIGN)
    kp = w0 + lax.broadcasted_iota(jnp.int32, (WIN_KEYS, 1), 0)
    win_ok = (kp <= t) & (kp > t - WINDOW)
    o_win = []
    for h in heads:
        q_win = jnp.concatenate([jnp.zeros((R, SLC_LANES + HD), BF16), q[h]], axis=1)
        sw = lax.dot_general(ks_ref[0, h, pl.ds(w0, WIN_KEYS), :], q_win, NT, preferred_element_type=F32)
        sw = jnp.where(win_ok, sw, NEG)
        pw = jnp.exp2(sw - jnp.max(sw, axis=0, keepdims=True))
        vv = jnp.concatenate([vwt_ref[0, h, wc + j] for j in range(WIN_KEYS // WIN_ALIGN)], axis=1)
        ow = jnp.dot(vv, pw.astype(BF16), preferred_element_type=F32)
        o_win.append(ow[:HD] * (1.0 / ow[HD:HD + 1]))

    few = cur < SLC_TOPN
    causal = blk <= cur
    blkf = blk.astype(F32)

    def pick(sc):
        mx = jnp.max(sc, axis=0, keepdims=True)
        first = jnp.min(jnp.where(sc == mx, blkf, float(SLC_LANES)), axis=0, keepdims=True)
        return jnp.where(blkf == first, -jnp.inf, sc)

    for _ in range(SLC_TOPN - 3):
        score = [pick(sc) for sc in score]
    q_aug = []
    for h in heads:
        chosen = forced | ((few | (score[h] == -jnp.inf)) & causal)
        bias_t = jnp.where(chosen, 0.0, NEG)
        bias = bias_t.T.astype(BF16)
        rows = jnp.concatenate([bias[a * QB:(a + 1) * QB] for a in range(NB) for g in range(G)], axis=0)
        q_aug.append(jnp.concatenate([rows, q[h], jnp.zeros((R, HD), BF16)], axis=1))

    def scores(c, slot):
        k0 = pl.multiple_of(c * SLC_CHUNK, SLC_CHUNK)
        for h in heads:
            s_ref[slot, h] = lax.dot_general(ks_ref[0, h, pl.ds(k0, SLC_CHUNK), :], q_aug[h], NT, preferred_element_type=F32)

    def absorb(c, slot, carry):
        out = []
        for h in heads:
            m_i, acc = carry[h]
            sc = s_ref[slot, h]
            m_new = jnp.maximum(m_i, jnp.max(sc, axis=0, keepdims=True))
            pp = jnp.exp2(sc - m_new).astype(BF16)
            acc = jnp.exp2(m_i - m_new) * acc + jnp.dot(vst_ref[0, h, c], pp, preferred_element_type=F32)
            out.append((m_new, acc))
        return tuple(out)

    diag = t0 // SLC_CHUNK
    n_pairs = diag // 2
    scores(0, 0)

    def pair(j, carry):
        scores(2 * j + 1, 1)
        carry = absorb(2 * j, 0, carry)
        scores(2 * j + 2, 0)
        return absorb(2 * j + 1, 1, carry)

    init = tuple((jnp.full((1, R), NEG, F32), jnp.zeros((VT_ROWS, R), F32)) for h in heads)
    carry = lax.fori_loop(0, n_pairs, pair, init)
    scores(2 * n_pairs + 1, 1)
    r0 = pl.multiple_of(t0 - diag * SLC_CHUNK, NB * QB)
    tri = tri_ref[...]
    for h in heads:
        s_ref[diag % 2, h, pl.ds(r0, NB * QB), :] += tri
    carry = absorb(2 * n_pairs, 0, carry)
    carry = absorb(2 * n_pairs + 1, 1, carry)

    sig = _sigmoid(gt_ref[0])
    glane = lax.broadcasted_iota(jnp.int32, (QB, 128), 1)
    outs = [[] for a in range(NB)]
    for h in heads:
        acc_s = carry[h][1]
        oc, os_, ow = o_cmp[h].T, (acc_s[:HD] * (1.0 / acc_s[HD:HD + 1])).T, o_win[h].T
        for a in range(NB):
            sig_a = sig[a * QB:(a + 1) * QB]
            for g in range(G):
                base = ((hk0 + h) * G + g) * N_NSA_BRANCH
                gc, gs, gw = [jnp.sum(jnp.where(glane == base + b, sig_a, 0.0), axis=-1, keepdims=True) for b in range(3)]
                r = slice((a * G + g) * QB, (a * G + g + 1) * QB)
                outs[a].append(gc * oc[r] + gs * os_[r] + gw * ow[r])
    o = jnp.concatenate([jnp.concatenate(oa, axis=1) for oa in outs], axis=0)
    o_ref[0] = (o * _silu(gate_ref[0].astype(F32))).astype(o_ref.dtype)


def _nsa_attn(qr, kc, vct, ks, vst, vwt, proj3, small3):
    B, S, _ = qr.shape
    Hk, HD, NH = NSA_KV_HEADS, NSA_HD, NSA_STEP_HEADS
    TQ = NSA_STEP_BLOCKS * SLC_BLOCK
    r = np.arange(TQ)[:, None]
    ln = np.arange(NSA_GROUP * TQ)[None, :]
    a, qi = ln // (NSA_GROUP * SLC_BLOCK), ln % SLC_BLOCK
    tri = jnp.asarray(np.where((r // SLC_BLOCK != a) | (r % SLC_BLOCK <= qi), 0.0, NEG), F32)
    nc = kc.shape[2]
    W = NH * NSA_GROUP * HD
    full = lambda a: pl.BlockSpec((1, NH) + a.shape[2:], lambda b, h, t: (b, h) + (0,) * (a.ndim - 2),
                                  pipeline_mode=pl.Buffered(1))
    return pl.pallas_call(
        _nsa_attn_kernel,
        grid=(B, Hk // NH, S // TQ),
        in_specs=[
            pl.BlockSpec((1, TQ, W), lambda b, h, t: (b, t, h)),
            full(kc), full(vct), full(ks), full(vst), full(vwt),
            pl.BlockSpec((1, TQ, 128), lambda b, h, t: (b, t, CS_NSG // 128)),
            pl.BlockSpec((1, TQ, W), lambda b, h, t: (b, t, C_NGATE // W + h)),
            pl.BlockSpec(tri.shape, lambda b, h, t: (0, 0), pipeline_mode=pl.Buffered(1)),
        ],
        out_specs=pl.BlockSpec((1, TQ, W), lambda b, h, t: (b, t, h)),
        out_shape=jax.ShapeDtypeStruct((B, S, NSA_HEADS * HD), BF16),
        scratch_shapes=[pltpu.VMEM((2, NH, SLC_CHUNK, NSA_GROUP * TQ), F32), pltpu.VMEM((NH, nc, NSA_GROUP * TQ), BF16)],
        compiler_params=_cp(("parallel", "parallel", "arbitrary")),
        name="nsa_attn",
    )(qr, kc, vct, ks, vst, vwt, small3, proj3, tri)


def _head_rms(x, gain, hd):
    outs = []
    for h in range(x.shape[-1] // hd):
        xh = x[:, h * hd:(h + 1) * hd]
        ms = jnp.mean(xh * xh, axis=-1, keepdims=True)
        outs.append(xh * lax.rsqrt(ms + EPS) * gain)
    return jnp.concatenate(outs, axis=1)


def _mem_prep_kernel(mem_ref, g_ref, w_ref, kn_ref, mk_ref, mv_ref):
    x = mem_ref[0]
    ms = jnp.mean(x * x, axis=-1, keepdims=True)
    xn = (x * lax.rsqrt(ms + EPS) * g_ref[...]).astype(BF16)
    kv = jnp.dot(xn, w_ref[...], preferred_element_type=F32)
    W = MEM_HEADS * MEM_HD
    mk_ref[0] = _head_rms(kv[:, :W], kn_ref[...], MEM_HD).astype(mk_ref.dtype)
    mv_ref[0] = kv[:, W:].astype(mv_ref.dtype)


def _mem_prep(mem, gain, w_kv, kn):
    B, N, D = mem.shape
    W = MEM_HEADS * MEM_HD
    return pl.pallas_call(
        _mem_prep_kernel,
        grid=(B,),
        in_specs=[
            pl.BlockSpec((1, N, D), lambda b: (b, 0, 0)),
            pl.BlockSpec((1, D), lambda b: (0, 0)),
            pl.BlockSpec((D, 2 * W), lambda b: (0, 0)),
            pl.BlockSpec((1, MEM_HD), lambda b: (0, 0)),
        ],
        out_specs=[pl.BlockSpec((1, N, W), lambda b: (b, 0, 0))] * 2,
        out_shape=[jax.ShapeDtypeStruct((B, N, W), BF16)] * 2,
        compiler_params=_cp(("parallel",)),
        name="mem_prep",
    )(mem, gain, w_kv, kn)


def _final_kernel(x_ref, ya_ref, yb_ref, mq_ref, mg_ref, mr_ref, mk_ref, mv_ref, qn_ref, wb_ref, wo_ref, o_ref):
    mq = _head_rms(mq_ref[...].astype(F32), qn_ref[...], MEM_HD)
    mk = mk_ref[0]
    mv = mv_ref[0]
    heads = []
    for h in range(MEM_HEADS):
        sl = slice(h * MEM_HD, (h + 1) * MEM_HD)
        s = lax.dot_general(mq[:, sl].astype(BF16), mk[:, sl], NT, preferred_element_type=F32) * (MEM_HD ** -0.5)
        m = jnp.max(s, axis=-1, keepdims=True)
        p = jnp.exp(s - m)
        o = jnp.dot(p.astype(BF16), mv[:, sl], preferred_element_type=F32) / jnp.sum(p, axis=-1, keepdims=True)
        heads.append(o)
    ym = jnp.concatenate(heads, axis=1) * _silu(mg_ref[...].astype(F32))

    mixed = None
    for c, y in enumerate((ya_ref[...], yb_ref[...], ym.astype(BF16))):
        z = jnp.dot(y, wb_ref[c], preferred_element_type=F32)
        term = _sigmoid(mr_ref[:, c * D_MODEL:(c + 1) * D_MODEL].astype(F32)) * z
        mixed = term if mixed is None else mixed + term
    o_ref[...] = x_ref[...] + jnp.dot(mixed.astype(BF16), wo_ref[...], preferred_element_type=F32)


def _final(x2, ya2, yb2, proj2, mk, mv, qn, wb, wo, S, tr=512):
    M, D = x2.shape
    N = mk.shape[1]
    nb = S // tr
    row = lambda c: pl.BlockSpec((tr, D), lambda i: (i, c))
    return pl.pallas_call(
        _final_kernel,
        grid=(M // tr,),
        in_specs=[
            row(0), row(0), row(0),
            row(C_MQ // D), row(C_MG // D),
            pl.BlockSpec((tr, 3 * D), lambda i: (i, C_MERGE // (3 * D))),
            pl.BlockSpec((1, N, D), lambda i: (i // nb, 0, 0)),
            pl.BlockSpec((1, N, D), lambda i: (i // nb, 0, 0)),
            pl.BlockSpec((1, MEM_HD), lambda i: (0, 0)),
            pl.BlockSpec((3, D, D), lambda i: (0, 0, 0), pipeline_mode=pl.Buffered(1)),
            pl.BlockSpec((D, D), lambda i: (0, 0), pipeline_mode=pl.Buffered(1)),
        ],
        out_specs=pl.BlockSpec((tr, D), lambda i: (i, 0)),
        out_shape=jax.ShapeDtypeStruct((M, D), F32),
        compiler_params=_cp(("parallel",)),
        name="final",
    )(x2, ya2, yb2, proj2, proj2, proj2, mk, mv, qn, wb, wo)


def _overlap_matrix_t(S):
    n_cmp = (S - CMP_LEN) // CMP_STRIDE + 1
    n_slc = S // SLC_BLOCK
    cs = np.arange(n_cmp)[:, None] * CMP_STRIDE
    ss = np.arange(n_slc)[None, :] * SLC_BLOCK
    ov = np.clip(np.minimum(cs + CMP_LEN, ss + SLC_BLOCK) - np.maximum(cs, ss), 0, None) / CMP_LEN
    out = np.zeros((SLC_LANES, S // CMP_STRIDE), np.float32)
    out[:n_slc, :n_cmp] = ov.T
    return jnp.asarray(out, BF16)


def _pad_cols(w, n):
    return jnp.pad(w, ((0, 0), (0, n - w.shape[1])))


def _layer(x, mem, positions, norm_gain, mem_norm_gain, w_in, w_gla_alpha, b_gla_alpha, gla_out_norm,
           nsa_q_norm, nsa_k_norm, pe_cmp_k, pe_cmp_v, w_cmp_k1, w_cmp_k2, w_cmp_v1, w_cmp_v2,
           w_mem_kv, mem_q_norm, mem_k_norm, w_branch, w_out):
    B, S, D = x.shape
    assert D == D_MODEL and S % (2 * SLC_CHUNK) == 0 and S >= WIN_KEYS and S // SLC_BLOCK <= SLC_LANES
    assert SLC_TOPN % NSA_STEP_BLOCKS == 0 and SLC_CHUNK % (NSA_STEP_BLOCKS * SLC_BLOCK) == 0

    o = np.cumsum([0, 512, 512, 1024, 16, 1024, 1024, 1536, 48, 1024, 1024, 1024, 3072])
    runs = [(2, 3), (4, 6), (8, 12), (6, 7), (0, 2)]
    wide = [w_in[:, o[a]:o[b]].astype(BF16) for a, b in runs]
    w_all = jnp.concatenate(wide + [jnp.zeros((D, NP - sum(w.shape[1] for w in wide)), BF16)], axis=1)
    glr, nsg = w_in[:, o[3]:o[4]], w_in[:, o[7]:o[8]]
    w_small = jnp.concatenate([_pad_cols(glr, 128), _pad_cols(nsg, 128)], axis=1).astype(BF16)

    x2 = x.reshape(B * S, D)
    proj2, small2 = _proj(x2, norm_gain.reshape(1, D), w_all, w_small)
    proj3 = proj2.reshape(B, S, NP)
    small3 = small2.reshape(B, S, NP_SMALL)

    wa = jnp.pad(w_gla_alpha, ((0, 128 - GLA_RANK), (0, 0)))
    ya = _gla(proj3, small3, wa, b_gla_alpha.reshape(1, -1), gla_out_norm.reshape(1, -1))

    half = NSA_HD // 2
    inv = ROPE_THETA ** (-jnp.arange(half, dtype=F32) / half)
    inv = jnp.tile(inv, 128 // half).reshape(1, 128)
    qn = jnp.tile(nsa_q_norm, NSA_KV_HEADS).reshape(1, -1)
    kn = jnp.tile(nsa_k_norm, (1, NSA_KV_HEADS))
    gid = np.arange(256) // NSA_HD
    gmat = jnp.asarray(gid[:, None] == gid[None, :], BF16)
    qr, kc_tok, vc_tok, ks, vst, vwt = _nsa_prep(proj3, positions.reshape(B, S, 1), inv, qn, kn, gmat)

    def cmp_args(pe, w1, w2):
        pe2 = pe.reshape(2, CMP_STRIDE * NSA_HD)
        pe16 = jnp.concatenate([jnp.broadcast_to(pe2[0:1], (8, pe2.shape[1])), jnp.broadcast_to(pe2[1:2], (8, pe2.shape[1]))], 0)
        w1f = w1.reshape(CMP_LEN * NSA_HD, CMP_HIDDEN).astype(BF16)
        return pe16, w1f[:CMP_STRIDE * NSA_HD], w1f[CMP_STRIDE * NSA_HD:], w2.astype(BF16)

    kc = _compress(kc_tok, *cmp_args(pe_cmp_k, w_cmp_k1, w_cmp_k2))
    tail = jnp.concatenate([jnp.ones((VT_ROWS - NSA_HD, S // CMP_STRIDE), BF16), _overlap_matrix_t(S)], axis=0)
    vct = _compress(vc_tok, *cmp_args(pe_cmp_v, w_cmp_v1, w_cmp_v2), tail=tail)
    yb = _nsa_attn(qr, kc, vct, ks, vst, vwt, proj3, small3)

    mk, mv = _mem_prep(mem, mem_norm_gain.reshape(1, D), w_mem_kv.astype(BF16), mem_k_norm.reshape(1, -1))
    out = _final(x2, ya.reshape(B * S, D), yb.reshape(B * S, D), proj2, mk, mv, mem_q_norm.reshape(1, -1),
                 w_branch.astype(BF16), w_out.astype(BF16), S)
    return out.reshape(B, S, D)


def kernel(x, mem, positions, norm_gain, mem_norm_gain, w_in, w_gla_alpha, b_gla_alpha, gla_out_norm, nsa_q_norm, nsa_k_norm, pe_cmp_k, pe_cmp_v, w_cmp_k1, w_cmp_k2, w_cmp_v1, w_cmp_v2, w_mem_kv, mem_q_norm, mem_k_norm, w_branch, w_out):
    h = x
    for l in range(norm_gain.shape[0]):
        h = _layer(h, mem, positions, norm_gain[l], mem_norm_gain[l], w_in[l], w_gla_alpha[l], b_gla_alpha[l],
                   gla_out_norm[l], nsa_q_norm[l], nsa_k_norm[l], pe_cmp_k[l], pe_cmp_v[l], w_cmp_k1[l],
                   w_cmp_k2[l], w_cmp_v1[l], w_cmp_v2[l], w_mem_kv[l], mem_q_norm[l], mem_k_norm[l],
                   w_branch[l], w_out[l])
    return h
```

```python
import functools

import numpy as np
import jax
import jax.numpy as jnp
from jax import lax
from jax.experimental import pallas as pl
from jax.experimental.pallas import tpu as pltpu

F32 = jnp.float32
BF16 = jnp.bfloat16

D_MODEL = 1024
ROPE_THETA = 10000.0
EPS = 1e-6
NEG = -1e30

GLA_HEADS = 4
GLA_DK = 128
GLA_DV = 256
GLA_RANK = 16
GLA_TAU = 16.0
GLA_CHUNK = 64
GLA_SUB = 16

NSA_HEADS = 16
NSA_KV_HEADS = 4
NSA_GROUP = 4
NSA_HD = 64
CMP_LEN = 32
CMP_SHIFT = 4
CMP_STRIDE = 1 << CMP_SHIFT
CMP_HIDDEN = 256
SLC_BLOCK = 64
SLC_TOPN = 16
WINDOW = 512
N_NSA_BRANCH = 3
SLC_LANES = 128
SLC_CHUNK = 512
VT_ROWS = NSA_HD + 16
LOG2E = 1.4426950408889634
WIN_ALIGN = 128
NSA_STEP_HEADS = 2
NSA_STEP_BLOCKS = 4
WIN_KEYS = WINDOW + max(NSA_STEP_BLOCKS * SLC_BLOCK, WIN_ALIGN)
KAUG = SLC_LANES + 2 * NSA_HD

MEM_HEADS = 4
MEM_HD = 256

C_GV, C_GG, C_NQ, C_NGATE, C_MQ, C_MG, C_MERGE = 0, 1024, 2048, 3072, 4096, 5120, 6144
C_NKV, C_GQ, C_GK = 9216, 10752, 11264
NP = 12288
CS_LR, CS_NSG, NP_SMALL = 0, 128, 256

VMEM_LIMIT = 48 * 1024 * 1024

NT = (((1,), (1,)), ((), ()))
TN = (((0,), (0,)), ((), ()))


def _cp(sem):
    return pltpu.CompilerParams(dimension_semantics=sem, vmem_limit_bytes=VMEM_LIMIT)


def _silu(x):
    return x * (1.0 / (1.0 + jnp.exp(-x)))


def _sigmoid(x):
    return 1.0 / (1.0 + jnp.exp(-x))


def _proj_kernel(x_ref, g_ref, w_ref, ws_ref, o_ref, os_ref, xn_ref):
    @pl.when(pl.program_id(1) == 0)
    def _():
        x = x_ref[...]
        ms = jnp.mean(x * x, axis=-1, keepdims=True)
        xn = (x * lax.rsqrt(ms + EPS) * g_ref[...]).astype(BF16)
        xn_ref[...] = xn
        os_ref[...] = jnp.dot(xn, ws_ref[...], preferred_element_type=F32)

    o_ref[...] = jnp.dot(xn_ref[...], w_ref[...], preferred_element_type=F32).astype(o_ref.dtype)


def _proj(x2, gain, w_all, w_small, tm=1024, tn=2048):
    M = x2.shape[0]
    ns = w_small.shape[1]
    return pl.pallas_call(
        _proj_kernel,
        grid=(M // tm, NP // tn),
        in_specs=[
            pl.BlockSpec((tm, D_MODEL), lambda i, j: (i, 0)),
            pl.BlockSpec((1, D_MODEL), lambda i, j: (0, 0)),
            pl.BlockSpec((D_MODEL, tn), lambda i, j: (0, j)),
            pl.BlockSpec((D_MODEL, ns), lambda i, j: (0, 0)),
        ],
        out_specs=[pl.BlockSpec((tm, tn), lambda i, j: (i, j)), pl.BlockSpec((tm, ns), lambda i, j: (i, 0))],
        out_shape=[jax.ShapeDtypeStruct((M, NP), BF16), jax.ShapeDtypeStruct((M, ns), F32)],
        scratch_shapes=[pltpu.VMEM((tm, D_MODEL), BF16)],
        compiler_params=_cp(("parallel", "arbitrary")),
        name="proj",
    )(x2, gain, w_all, w_small)


def _split2(x):
    hi = x.astype(BF16)
    return hi, (x - hi.astype(F32)).astype(BF16)


def _gla_kernel(q_ref, k_ref, v_ref, gate_ref, lr_ref, wa_ref, ba_ref, gn_ref, o_ref, st_ref, g_ref, kf_ref, *, n_chunks):
    C, SB, H, DK, DV = GLA_CHUNK, GLA_SUB, GLA_HEADS, GLA_DK, GLA_DV

    @pl.when(pl.program_id(1) == 0)
    def _():
        st_ref[...] = jnp.zeros_like(st_ref)

    gn = gn_ref[...]
    ri = lax.broadcasted_iota(jnp.int32, (C, C), 0)
    ci = lax.broadcasted_iota(jnp.int32, (C, C), 1)
    tri = (ri >= ci).astype(BF16)
    si = lax.broadcasted_iota(jnp.int32, (SB, SB), 0)
    sj = lax.broadcasted_iota(jnp.int32, (SB, SB), 1)
    sub_causal = si >= sj

    lr_hi, lr_lo = _split2(lr_ref[0])
    wa_hi, wa_lo = _split2(wa_ref[...])
    z = (jnp.dot(lr_hi, wa_hi, preferred_element_type=F32) + jnp.dot(lr_hi, wa_lo, preferred_element_type=F32)
         + jnp.dot(lr_lo, wa_hi, preferred_element_type=F32)) + ba_ref[...]
    la = -(jnp.maximum(-z, 0.0) + jnp.log(1.0 + jnp.exp(-jnp.abs(z)))) * (1.0 / GLA_TAU)
    la1 = la.astype(BF16)
    la2, la3 = _split2(la - la1.astype(F32))
    for c in range(n_chunks):
        r = slice(c * C, (c + 1) * C)
        g_ref[r, :] = (jnp.dot(tri, la1[r], preferred_element_type=F32) + jnp.dot(tri, la2[r], preferred_element_type=F32)
                       + jnp.dot(tri, la3[r], preferred_element_type=F32))

    kf_ref[...] = k_ref[0].astype(F32)

    def chunk(c, carry):
        rows = pl.ds(pl.multiple_of(c * C, C), C)
        for h in range(H):
            kl = slice(h * DK, (h + 1) * DK)
            vl = slice(h * DV, (h + 1) * DV)
            q = q_ref[0, rows, kl].astype(F32) * (DK ** -0.5)
            k = kf_ref[rows, kl]
            vb = v_ref[0, rows, vl]
            g = g_ref[rows, kl]

            outs = []
            for i in range(C // SB):
                lo = i * SB
                gi = g[lo:lo + SB]
                qi = q[lo:lo + SB]
                ki = k[lo:lo + SB]
                a_d = jnp.zeros((SB, SB), F32)
                for jj in range(SB):
                    row = slice(lo + jj, lo + jj + 1)
                    w = (qi * kf_ref.at[rows, kl][row, :]) * jnp.exp(jnp.minimum(gi - g_ref.at[rows, kl][row, :], 0.0))
                    a_d = jnp.where(sj == jj, jnp.sum(w, axis=-1, keepdims=True), a_d)
                a_d = jnp.where(sub_causal, a_d, 0.0)
                o_i = jnp.dot(a_d.astype(BF16), vb[lo:lo + SB], preferred_element_type=F32)
                if i > 0:
                    r = g[lo:lo + 1]
                    qt = qi * jnp.exp(gi - r)
                    kt = k[:lo] * jnp.exp(r - g[:lo])
                    a_o = lax.dot_general(qt.astype(BF16), kt.astype(BF16), NT, preferred_element_type=F32)
                    o_i = o_i + jnp.dot(a_o.astype(BF16), vb[:lo], preferred_element_type=F32)
                outs.append(o_i)
            o = jnp.concatenate(outs, axis=0)

            st = st_ref[h]
            qg = q * jnp.exp(g)
            o = o + lax.dot_general(qg.astype(BF16), st.astype(BF16), NT, preferred_element_type=F32)
            gl = g[C - 1:C]
            kd = k * jnp.exp(gl - g)
            st_ref[h] = st * jnp.exp(gl) + lax.dot_general(vb, kd.astype(BF16), TN, preferred_element_type=F32)

            ms = jnp.mean(o * o, axis=-1, keepdims=True)
            y = o * lax.rsqrt(ms + EPS) * gn
            o_ref[0, rows, vl] = (y * _silu(gate_ref[0, rows, vl].astype(F32))).astype(o_ref.dtype)
        return carry

    lax.fori_loop(0, n_chunks, chunk, 0)


def _gla(proj3, small3, wa, ba, gn, tr=512):
    B, S, _ = proj3.shape
    H, DK, DV = GLA_HEADS, GLA_DK, GLA_DV
    kern = functools.partial(_gla_kernel, n_chunks=tr // GLA_CHUNK)
    return pl.pallas_call(
        kern,
        grid=(B, S // tr),
        in_specs=[
            pl.BlockSpec((1, tr, H * DK), lambda b, r: (b, r, C_GQ // (H * DK))),
            pl.BlockSpec((1, tr, H * DK), lambda b, r: (b, r, C_GK // (H * DK))),
            pl.BlockSpec((1, tr, H * DV), lambda b, r: (b, r, C_GV // (H * DV))),
            pl.BlockSpec((1, tr, H * DV), lambda b, r: (b, r, C_GG // (H * DV))),
            pl.BlockSpec((1, tr, 128), lambda b, r: (b, r, CS_LR // 128)),
            pl.BlockSpec((128, H * DK), lambda b, r: (0, 0)),
            pl.BlockSpec((1, H * DK), lambda b, r: (0, 0)),
            pl.BlockSpec((1, DV), lambda b, r: (0, 0)),
        ],
        out_specs=pl.BlockSpec((1, tr, H * DV), lambda b, r: (b, r, 0)),
        out_shape=jax.ShapeDtypeStruct((B, S, H * DV), BF16),
        scratch_shapes=[pltpu.VMEM((H, DV, DK), F32), pltpu.VMEM((tr, H * DK), F32), pltpu.VMEM((tr, H * DK), F32)],
        compiler_params=_cp(("parallel", "arbitrary")),
        name="gla",
    )(proj3, proj3, proj3, proj3, small3, wa, ba, gn)


def _group_meansq(x, gmat):
    sq = x * x
    hi = sq.astype(BF16)
    lo = (sq - hi.astype(F32)).astype(BF16)
    s = jnp.dot(hi, gmat, preferred_element_type=F32) + jnp.dot(lo, gmat, preferred_element_type=F32)
    return s * (1.0 / NSA_HD)


def _rope(x, cos, sin, first_half):
    w = x.shape[-1]
    rot = jnp.where(first_half, -pltpu.roll(x, w - NSA_HD // 2, 1), pltpu.roll(x, NSA_HD // 2, 1))
    return x * cos + rot * sin


def _nsa_prep_kernel(q_ref, kvc_ref, kvs_ref, kvw_ref, pos_ref, inv_ref, qn_ref, kn_ref, gm_ref,
                     qr_ref, kc_ref, vc_ref, ks_ref, vst_ref, vwt_ref):
    tr = q_ref.shape[1]
    W = NSA_KV_HEADS * NSA_HD
    HD = NSA_HD
    gmat = gm_ref[...]
    ang = pos_ref[0].astype(F32) * inv_ref[...]
    cos1, sin1 = jnp.cos(ang), jnp.sin(ang)
    cos = jnp.concatenate([cos1, cos1], axis=1)
    sin = jnp.concatenate([sin1, sin1], axis=1)
    lane = lax.broadcasted_iota(jnp.int32, (tr, W), 1)
    first_half = (lane % HD) < (HD // 2)

    def norm_rope(x, gain):
        y = x * lax.rsqrt(_group_meansq(x, gmat) + EPS) * gain
        return _rope(y, cos, sin, first_half)

    qn = qn_ref[...]
    for s in range(NSA_HEADS * HD // W):
        xq = q_ref[0, :, s * W:(s + 1) * W].astype(F32)
        qr_ref[0, :, s * W:(s + 1) * W] = (norm_rope(xq, qn) * (HD ** -0.5 * LOG2E)).astype(qr_ref.dtype)

    kc = norm_rope(kvc_ref[0, :, :W].astype(F32), kn_ref[0:1, :]).astype(BF16)
    vc = kvc_ref[0, :, W:]
    for h in range(NSA_KV_HEADS):
        kc_ref[0, h] = kc[:, h * HD:(h + 1) * HD]
        vc_ref[0, h] = vc[:, h * HD:(h + 1) * HD]

    ks = norm_rope(kvs_ref[0, :, :W].astype(F32), kn_ref[1:2, :]).astype(BF16)
    kw = norm_rope(kvw_ref[0, :, :W].astype(F32), kn_ref[2:3, :]).astype(BF16)
    row_blk = (pl.program_id(1) * tr + lax.broadcasted_iota(jnp.int32, (tr, SLC_LANES), 0)) // SLC_BLOCK
    onehot = (row_blk == lax.broadcasted_iota(jnp.int32, (tr, SLC_LANES), 1)).astype(BF16)
    vst = kvs_ref[0, :, W:].astype(F32).T
    for h in range(NSA_KV_HEADS):
        ks_ref[0, h] = jnp.concatenate([onehot, ks[:, h * HD:(h + 1) * HD], kw[:, h * HD:(h + 1) * HD]], axis=1)
        for j in range(tr // SLC_CHUNK):
            vst_ref[0, h, j, :HD] = vst[h * HD:(h + 1) * HD, j * SLC_CHUNK:(j + 1) * SLC_CHUNK].astype(BF16)
            vst_ref[0, h, j, HD:] = jnp.ones((VT_ROWS - HD, SLC_CHUNK), BF16)

    vwt = kvw_ref[0, :, W:].astype(F32).T
    for h in range(NSA_KV_HEADS):
        for j in range(tr // WIN_ALIGN):
            vwt_ref[0, h, j, :HD] = vwt[h * HD:(h + 1) * HD, j * WIN_ALIGN:(j + 1) * WIN_ALIGN].astype(BF16)
            vwt_ref[0, h, j, HD:] = jnp.ones((VT_ROWS - HD, WIN_ALIGN), BF16)


def _nsa_prep(proj3, pos3, inv, qn, kn, gmat, tr=512):
    B, S, _ = proj3.shape
    Hk, HD = NSA_KV_HEADS, NSA_HD
    hm = lambda w, dt: jax.ShapeDtypeStruct((B, Hk, S, w), dt)
    hspec = lambda w: pl.BlockSpec((1, Hk, tr, w), lambda b, r: (b, 0, r, 0))
    tspec = lambda c, rows=HD: pl.BlockSpec((1, Hk, tr // c, rows, c), lambda b, r: (b, 0, r, 0, 0))
    return pl.pallas_call(
        _nsa_prep_kernel,
        grid=(B, S // tr),
        in_specs=[
            pl.BlockSpec((1, tr, 1024), lambda b, r: (b, r, C_NQ // 1024)),
            pl.BlockSpec((1, tr, 512), lambda b, r: (b, r, C_NKV // 512)),
            pl.BlockSpec((1, tr, 512), lambda b, r: (b, r, C_NKV // 512 + 1)),
            pl.BlockSpec((1, tr, 512), lambda b, r: (b, r, C_NKV // 512 + 2)),
            pl.BlockSpec((1, tr, 1), lambda b, r: (b, r, 0)),
            pl.BlockSpec((1, 128), lambda b, r: (0, 0)),
            pl.BlockSpec((1, 256), lambda b, r: (0, 0)),
            pl.BlockSpec((3, 256), lambda b, r: (0, 0)),
            pl.BlockSpec((256, 256), lambda b, r: (0, 0)),
        ],
        out_specs=[pl.BlockSpec((1, tr, 1024), lambda b, r: (b, r, 0)), hspec(HD), hspec(HD), hspec(KAUG),
                   tspec(SLC_CHUNK, VT_ROWS), tspec(WIN_ALIGN, VT_ROWS)],
        out_shape=[jax.ShapeDtypeStruct((B, S, 1024), BF16), hm(HD, BF16), hm(HD, BF16), hm(KAUG, BF16),
                   jax.ShapeDtypeStruct((B, Hk, S // SLC_CHUNK, VT_ROWS, SLC_CHUNK), BF16),
                   jax.ShapeDtypeStruct((B, Hk, S // WIN_ALIGN, VT_ROWS, WIN_ALIGN), BF16)],
        compiler_params=_cp(("parallel", "parallel")),
        name="nsa_prep",
    )(proj3, proj3, proj3, proj3, pos3, inv, qn, kn, gmat)


def _compress_kernel(t_ref, pe_ref, w1a_ref, w1b_ref, w2_ref, *rest, transposed):
    tail_ref, o_ref = rest if transposed else (None,) + rest
    t = t_ref[0, 0]
    w1a, w1b = w1a_ref[...], w1b_ref[...]
    u = jnp.dot(t, w1a, preferred_element_type=F32)
    v = jnp.dot(t, w1b, preferred_element_type=F32)
    pe = pe_ref[...].astype(BF16)
    c = (jnp.dot(pe[0:8], w1a, preferred_element_type=F32) + jnp.dot(pe[8:16], w1b, preferred_element_type=F32))[0:1]
    n = v.shape[0]
    h = u + pltpu.roll(v, n - 1, 0) + c
    h = jax.nn.gelu(h).astype(BF16)
    if transposed:
        hd = w2_ref.shape[0]
        o_ref[0, 0, :hd] = lax.dot_general(w2_ref[...], h, NT, preferred_element_type=F32).astype(o_ref.dtype)
        o_ref[0, 0, hd:] = tail_ref[...]
    else:
        o_ref[0, 0] = jnp.dot(h, w2_ref[...], preferred_element_type=F32).astype(o_ref.dtype)


def _compress(tok, pe2, w1a, w1b, w2, tail=None):
    B, Hk, S, HD = tok.shape
    n = S // CMP_STRIDE
    t2 = tok.reshape(B, Hk, n, CMP_STRIDE * HD)
    transposed = tail is not None
    oshape = (HD + tail.shape[0], n) if transposed else (n, HD)
    w2 = w2.T if transposed else w2
    extra = ([tail], [pl.BlockSpec(tail.shape, lambda b, h: (0, 0))]) if transposed else ([], [])
    return pl.pallas_call(
        functools.partial(_compress_kernel, transposed=transposed),
        grid=(B, Hk),
        in_specs=[
            pl.BlockSpec((1, 1, n, CMP_STRIDE * HD), lambda b, h: (b, h, 0, 0)),
            pl.BlockSpec((16, CMP_STRIDE * HD), lambda b, h: (0, 0)),
            pl.BlockSpec((CMP_STRIDE * HD, CMP_HIDDEN), lambda b, h: (0, 0)),
            pl.BlockSpec((CMP_STRIDE * HD, CMP_HIDDEN), lambda b, h: (0, 0)),
            pl.BlockSpec(w2.shape, lambda b, h: (0, 0)),
        ] + extra[1],
        out_specs=pl.BlockSpec((1, 1) + oshape, lambda b, h: (b, h, 0, 0)),
        out_shape=jax.ShapeDtypeStruct((B, Hk) + oshape, BF16),
        compiler_params=_cp(("parallel", "parallel")),
        name="compress_v" if transposed else "compress_k",
    )(t2, pe2, w1a, w1b, w2, *extra[0])


def _nsa_attn_kernel(q_ref, kc_ref, vct_ref, ks_ref, vst_ref, vwt_ref, gt_ref, gate_ref, tri_ref, cmask_ref, wmask_ref,
                     o_ref,
                     s_ref, e_ref):
    G, HD, QB, NH, NB = NSA_GROUP, NSA_HD, SLC_BLOCK, NSA_STEP_HEADS, NSA_STEP_BLOCKS
    R = NB * G * QB
    heads = range(NH)
    hk0 = pl.program_id(1) * NH
    step = pl.program_id(2)
    t0 = step * (NB * QB)
    q = [jnp.concatenate([q_ref[0, a * QB:(a + 1) * QB, (h * G + g) * HD:(h * G + g + 1) * HD]
                          for a in range(NB) for g in range(G)], axis=0)
         for h in heads]

    nc = kc_ref.shape[2]
    cmask = cmask_ref[pl.ds(pl.multiple_of(nc - t0 // CMP_STRIDE, CMP_STRIDE), nc), :]
    for h in heads:
        sm = lax.dot_general(kc_ref[0, h], q[h], NT, preferred_element_type=F32) + cmask
        m = jnp.maximum(jnp.max(sm, axis=0, keepdims=True), 0.1 * NEG)
        e_ref[h] = jnp.exp2(sm - m).astype(BF16)
    QL = max(NB * QB, 2 * QB)
    blk = lax.broadcasted_iota(jnp.int32, (SLC_LANES, QL), 0)
    qlane = lax.broadcasted_iota(jnp.int32, (1, QL), 1)
    cur = step * NB + (qlane // QB) % NB
    forced = (blk == 0) | (blk == cur) | (blk == cur - 1)
    o_cmp, score = [], []
    for h in heads:
        oc = jnp.dot(vct_ref[0, h], e_ref[h], preferred_element_type=F32)
        den = oc[HD:HD + 1]
        rden = 1.0 / jnp.where(den > 0.0, den, 1.0)
        o_cmp.append(oc[:HD] * rden)
        impf = oc[VT_ROWS:] * rden
        parts = []
        for a in range(NB):
            pa = impf[:, a * G * QB:(a + 1) * G * QB]
            p2 = pa[:, :2 * QB] + pa[:, 2 * QB:]
            parts.append(p2 + pltpu.roll(p2, QB, 1))
        first_half = qlane[:, :2 * QB] < QB
        imp = parts[0] if NB == 1 else jnp.concatenate(
            [jnp.where(first_half, parts[a], parts[a + 1]) for a in range(0, NB, 2)], axis=1)
        score.append(jnp.where(forced, -jnp.inf, jnp.where(blk <= cur, imp, NEG)))

    wc = jnp.maximum(t0 - WINDOW, 0) // WIN_ALIGN
    w0 = pl.multiple_of(wc * WIN_ALIGN, WIN_ALIGN)
    wmask = wmask_ref[pl.ds(pl.multiple_of(jnp.maximum(WINDOW - t0, 0), NB * QB), WIN_KEYS), :]
    o_win = []
    for h in heads:
        q_win = jnp.concatenate([jnp.zeros((R, SLC_LANES + HD), BF16), q[h]], axis=1)
        sw = lax.dot_general(ks_ref[0, h, pl.ds(w0, WIN_KEYS), :], q_win, NT, preferred_element_type=F32)
        sw = sw + wmask
        pw = jnp.exp2(sw - jnp.max(sw, axis=0, keepdims=True))
        vv = jnp.concatenate([vwt_ref[0, h, wc + j] for j in range(WIN_KEYS // WIN_ALIGN)], axis=1)
        ow = jnp.dot(vv, pw.astype(BF16), preferred_element_type=F32)
        o_win.append(ow[:HD] * (1.0 / ow[HD:HD + 1]))

    few = cur < SLC_TOPN
    causal = blk <= cur
    blkf = blk.astype(F32)

    def pick(sc):
        mx = jnp.max(sc, axis=0, keepdims=True)
        first = jnp.min(jnp.where(sc == mx, blkf, float(SLC_LANES)), axis=0, keepdims=True)
        return jnp.where(blkf == first, -jnp.inf, sc)

    for _ in range(SLC_TOPN - 3):
        score = [pick(sc) for sc in score]
    q_aug = []
    for h in heads:
        chosen = forced | ((few | (score[h] == -jnp.inf)) & causal)
        bias_t = jnp.where(chosen, 0.0, NEG)
        bias = bias_t.T.astype(BF16)
        rows = jnp.concatenate([bias[a * QB:(a + 1) * QB] for a in range(NB) for g in range(G)], axis=0)
        q_aug.append(jnp.concatenate([rows, q[h], jnp.zeros((R, HD), BF16)], axis=1))

    def scores(c, slot):
        k0 = pl.multiple_of(c * SLC_CHUNK, SLC_CHUNK)
        for h in heads:
            s_ref[slot, h] = lax.dot_general(ks_ref[0, h, pl.ds(k0, SLC_CHUNK), :], q_aug[h], NT, preferred_element_type=F32)

    def absorb(c, slot, carry):
        out = []
        for h in heads:
            m_i, acc = carry[h]
            sc = s_ref[slot, h]
            m_new = jnp.maximum(m_i, jnp.max(sc, axis=0, keepdims=True))
            pp = jnp.exp2(sc - m_new).astype(BF16)
            acc = jnp.exp2(m_i - m_new) * acc + jnp.dot(vst_ref[0, h, c], pp, preferred_element_type=F32)
            out.append((m_new, acc))
        return tuple(out)

    diag = t0 // SLC_CHUNK
    n_pairs = diag // 2
    scores(0, 0)

    def pair(j, carry):
        scores(2 * j + 1, 1)
        carry = absorb(2 * j, 0, carry)
        scores(2 * j + 2, 0)
        return absorb(2 * j + 1, 1, carry)

    init = tuple((jnp.full((1, R), NEG, F32), jnp.zeros((VT_ROWS, R), F32)) for h in heads)
    carry = lax.fori_loop(0, n_pairs, pair, init)
    scores(2 * n_pairs + 1, 1)
    r0 = pl.multiple_of(t0 - diag * SLC_CHUNK, NB * QB)
    tri = tri_ref[...]
    for h in heads:
        s_ref[diag % 2, h, pl.ds(r0, NB * QB), :] += tri
    carry = absorb(2 * n_pairs, 0, carry)
    carry = absorb(2 * n_pairs + 1, 1, carry)

    sig = _sigmoid(gt_ref[0])
    glane = lax.broadcasted_iota(jnp.int32, (QB, 128), 1)
    outs = [[] for a in range(NB)]
    for h in heads:
        acc_s = carry[h][1]
        oc, os_, ow = o_cmp[h].T, (acc_s[:HD] * (1.0 / acc_s[HD:HD + 1])).T, o_win[h].T
        for a in range(NB):
            sig_a = sig[a * QB:(a + 1) * QB]
            for g in range(G):
                base = ((hk0 + h) * G + g) * N_NSA_BRANCH
                gc, gs, gw = [jnp.sum(jnp.where(glane == base + b, sig_a, 0.0), axis=-1, keepdims=True) for b in range(3)]
                r = slice((a * G + g) * QB, (a * G + g + 1) * QB)
                outs[a].append(gc * oc[r] + gs * os_[r] + gw * ow[r])
    o = jnp.concatenate([jnp.concatenate(oa, axis=1) for oa in outs], axis=0)
    o_ref[0] = (o * _silu(gate_ref[0].astype(F32))).astype(o_ref.dtype)


def _nsa_attn(qr, kc, vct, ks, vst, vwt, proj3, small3):
    B, S, _ = qr.shape
    Hk, HD, NH = NSA_KV_HEADS, NSA_HD, NSA_STEP_HEADS
    TQ = NSA_STEP_BLOCKS * SLC_BLOCK
    r = np.arange(TQ)[:, None]
    ln = np.arange(NSA_GROUP * TQ)[None, :]
    a, qi = ln // (NSA_GROUP * SLC_BLOCK), ln % SLC_BLOCK
    tri = jnp.asarray(np.where((r // SLC_BLOCK != a) | (r % SLC_BLOCK <= qi), 0.0, NEG), F32)
    nc = kc.shape[2]
    tl = a * SLC_BLOCK + qi
    u = np.arange(2 * nc)[:, None]
    cmask = jnp.asarray(np.where(u - nc <= (tl - (CMP_LEN - 1)) // CMP_STRIDE, 0.0, NEG), F32)
    u = np.arange(WINDOW + WIN_KEYS)[:, None]
    wmask = jnp.asarray(np.where((u > tl) & (u <= WINDOW + tl), 0.0, NEG), F32)
    const = lambda c: pl.BlockSpec(c.shape, lambda b, h, t: (0, 0), pipeline_mode=pl.Buffered(1))
    W = NH * NSA_GROUP * HD
    full = lambda a: pl.BlockSpec((1, NH) + a.shape[2:], lambda b, h, t: (b, h) + (0,) * (a.ndim - 2),
                                  pipeline_mode=pl.Buffered(1))
    return pl.pallas_call(
        _nsa_attn_kernel,
        grid=(B, Hk // NH, S // TQ),
        in_specs=[
            pl.BlockSpec((1, TQ, W), lambda b, h, t: (b, t, h)),
            full(kc), full(vct), full(ks), full(vst), full(vwt),
            pl.BlockSpec((1, TQ, 128), lambda b, h, t: (b, t, CS_NSG // 128)),
            pl.BlockSpec((1, TQ, W), lambda b, h, t: (b, t, C_NGATE // W + h)),
            const(tri), const(cmask), const(wmask),
        ],
        out_specs=pl.BlockSpec((1, TQ, W), lambda b, h, t: (b, t, h)),
        out_shape=jax.ShapeDtypeStruct((B, S, NSA_HEADS * HD), BF16),
        scratch_shapes=[pltpu.VMEM((2, NH, SLC_CHUNK, NSA_GROUP * TQ), F32), pltpu.VMEM((NH, nc, NSA_GROUP * TQ), BF16)],
        compiler_params=_cp(("parallel", "parallel", "arbitrary")),
        name="nsa_attn",
    )(qr, kc, vct, ks, vst, vwt, small3, proj3, tri, cmask, wmask)


def _head_rms(x, gain, hd):
    outs = []
    for h in range(x.shape[-1] // hd):
        xh = x[:, h * hd:(h + 1) * hd]
        ms = jnp.mean(xh * xh, axis=-1, keepdims=True)
        outs.append(xh * lax.rsqrt(ms + EPS) * gain)
    return jnp.concatenate(outs, axis=1)


def _mem_prep_kernel(mem_ref, g_ref, w_ref, kn_ref, mk_ref, mv_ref):
    x = mem_ref[0]
    ms = jnp.mean(x * x, axis=-1, keepdims=True)
    xn = (x * lax.rsqrt(ms + EPS) * g_ref[...]).astype(BF16)
    kv = jnp.dot(xn, w_ref[...], preferred_element_type=F32)
    W = MEM_HEADS * MEM_HD
    mk_ref[0] = _head_rms(kv[:, :W], kn_ref[...], MEM_HD).astype(mk_ref.dtype)
    mv_ref[0] = kv[:, W:].astype(mv_ref.dtype)


def _mem_prep(mem, gain, w_kv, kn):
    B, N, D = mem.shape
    W = MEM_HEADS * MEM_HD
    return pl.pallas_call(
        _mem_prep_kernel,
        grid=(B,),
        in_specs=[
            pl.BlockSpec((1, N, D), lambda b: (b, 0, 0)),
            pl.BlockSpec((1, D), lambda b: (0, 0)),
            pl.BlockSpec((D, 2 * W), lambda b: (0, 0)),
            pl.BlockSpec((1, MEM_HD), lambda b: (0, 0)),
        ],
        out_specs=[pl.BlockSpec((1, N, W), lambda b: (b, 0, 0))] * 2,
        out_shape=[jax.ShapeDtypeStruct((B, N, W), BF16)] * 2,
        compiler_params=_cp(("parallel",)),
        name="mem_prep",
    )(mem, gain, w_kv, kn)


def _final_kernel(x_ref, ya_ref, yb_ref, mq_ref, mg_ref, mr_ref, mk_ref, mv_ref, qn_ref, wb_ref, wo_ref, o_ref):
    mq = _head_rms(mq_ref[...].astype(F32), qn_ref[...], MEM_HD)
    mk = mk_ref[0]
    mv = mv_ref[0]
    heads = []
    for h in range(MEM_HEADS):
        sl = slice(h * MEM_HD, (h + 1) * MEM_HD)
        s = lax.dot_general(mq[:, sl].astype(BF16), mk[:, sl], NT, preferred_element_type=F32) * (MEM_HD ** -0.5)
        m = jnp.max(s, axis=-1, keepdims=True)
        p = jnp.exp(s - m)
        o = jnp.dot(p.astype(BF16), mv[:, sl], preferred_element_type=F32) / jnp.sum(p, axis=-1, keepdims=True)
        heads.append(o)
    ym = jnp.concatenate(heads, axis=1) * _silu(mg_ref[...].astype(F32))

    mixed = None
    for c, y in enumerate((ya_ref[...], yb_ref[...], ym.astype(BF16))):
        z = jnp.dot(y, wb_ref[c], preferred_element_type=F32)
        term = _sigmoid(mr_ref[:, c * D_MODEL:(c + 1) * D_MODEL].astype(F32)) * z
        mixed = term if mixed is None else mixed + term
    o_ref[...] = x_ref[...] + jnp.dot(mixed.astype(BF16), wo_ref[...], preferred_element_type=F32)


def _final(x2, ya2, yb2, proj2, mk, mv, qn, wb, wo, S, tr=512):
    M, D = x2.shape
    N = mk.shape[1]
    nb = S // tr
    row = lambda c: pl.BlockSpec((tr, D), lambda i: (i, c))
    return pl.pallas_call(
        _final_kernel,
        grid=(M // tr,),
        in_specs=[
            row(0), row(0), row(0),
            row(C_MQ // D), row(C_MG // D),
            pl.BlockSpec((tr, 3 * D), lambda i: (i, C_MERGE // (3 * D))),
            pl.BlockSpec((1, N, D), lambda i: (i // nb, 0, 0)),
            pl.BlockSpec((1, N, D), lambda i: (i // nb, 0, 0)),
            pl.BlockSpec((1, MEM_HD), lambda i: (0, 0)),
            pl.BlockSpec((3, D, D), lambda i: (0, 0, 0), pipeline_mode=pl.Buffered(1)),
            pl.BlockSpec((D, D), lambda i: (0, 0), pipeline_mode=pl.Buffered(1)),
        ],
        out_specs=pl.BlockSpec((tr, D), lambda i: (i, 0)),
        out_shape=jax.ShapeDtypeStruct((M, D), F32),
        compiler_params=_cp(("parallel",)),
        name="final",
    )(x2, ya2, yb2, proj2, proj2, proj2, mk, mv, qn, wb, wo)


def _overlap_matrix_t(S):
    n_cmp = (S - CMP_LEN) // CMP_STRIDE + 1
    n_slc = S // SLC_BLOCK
    cs = np.arange(n_cmp)[:, None] * CMP_STRIDE
    ss = np.arange(n_slc)[None, :] * SLC_BLOCK
    ov = np.clip(np.minimum(cs + CMP_LEN, ss + SLC_BLOCK) - np.maximum(cs, ss), 0, None) / CMP_LEN
    out = np.zeros((SLC_LANES, S // CMP_STRIDE), np.float32)
    out[:n_slc, :n_cmp] = ov.T
    return jnp.asarray(out, BF16)


def _pad_cols(w, n):
    return jnp.pad(w, ((0, 0), (0, n - w.shape[1])))


def _layer(x, mem, positions, norm_gain, mem_norm_gain, w_in, w_gla_alpha, b_gla_alpha, gla_out_norm,
           nsa_q_norm, nsa_k_norm, pe_cmp_k, pe_cmp_v, w_cmp_k1, w_cmp_k2, w_cmp_v1, w_cmp_v2,
           w_mem_kv, mem_q_norm, mem_k_norm, w_branch, w_out):
    B, S, D = x.shape
    assert D == D_MODEL and S % (2 * SLC_CHUNK) == 0 and S >= WIN_KEYS and S // SLC_BLOCK <= SLC_LANES
    assert SLC_TOPN % NSA_STEP_BLOCKS == 0 and SLC_CHUNK % (NSA_STEP_BLOCKS * SLC_BLOCK) == 0

    o = np.cumsum([0, 512, 512, 1024, 16, 1024, 1024, 1536, 48, 1024, 1024, 1024, 3072])
    runs = [(2, 3), (4, 6), (8, 12), (6, 7), (0, 2)]
    wide = [w_in[:, o[a]:o[b]].astype(BF16) for a, b in runs]
    w_all = jnp.concatenate(wide + [jnp.zeros((D, NP - sum(w.shape[1] for w in wide)), BF16)], axis=1)
    glr, nsg = w_in[:, o[3]:o[4]], w_in[:, o[7]:o[8]]
    w_small = jnp.concatenate([_pad_cols(glr, 128), _pad_cols(nsg, 128)], axis=1).astype(BF16)

    x2 = x.reshape(B * S, D)
    proj2, small2 = _proj(x2, norm_gain.reshape(1, D), w_all, w_small)
    proj3 = proj2.reshape(B, S, NP)
    small3 = small2.reshape(B, S, NP_SMALL)

    wa = jnp.pad(w_gla_alpha, ((0, 128 - GLA_RANK), (0, 0)))
    ya = _gla(proj3, small3, wa, b_gla_alpha.reshape(1, -1), gla_out_norm.reshape(1, -1))

    half = NSA_HD // 2
    inv = ROPE_THETA ** (-jnp.arange(half, dtype=F32) / half)
    inv = jnp.tile(inv, 128 // half).reshape(1, 128)
    qn = jnp.tile(nsa_q_norm, NSA_KV_HEADS).reshape(1, -1)
    kn = jnp.tile(nsa_k_norm, (1, NSA_KV_HEADS))
    gid = np.arange(256) // NSA_HD
    gmat = jnp.asarray(gid[:, None] == gid[None, :], BF16)
    qr, kc_tok, vc_tok, ks, vst, vwt = _nsa_prep(proj3, positions.reshape(B, S, 1), inv, qn, kn, gmat)

    def cmp_args(pe, w1, w2):
        pe2 = pe.reshape(2, CMP_STRIDE * NSA_HD)
        pe16 = jnp.concatenate([jnp.broadcast_to(pe2[0:1], (8, pe2.shape[1])), jnp.broadcast_to(pe2[1:2], (8, pe2.shape[1]))], 0)
        w1f = w1.reshape(CMP_LEN * NSA_HD, CMP_HIDDEN).astype(BF16)
        return pe16, w1f[:CMP_STRIDE * NSA_HD], w1f[CMP_STRIDE * NSA_HD:], w2.astype(BF16)

    kc = _compress(kc_tok, *cmp_args(pe_cmp_k, w_cmp_k1, w_cmp_k2))
    tail = jnp.concatenate([jnp.ones((VT_ROWS - NSA_HD, S // CMP_STRIDE), BF16), _overlap_matrix_t(S)], axis=0)
    vct = _compress(vc_tok, *cmp_args(pe_cmp_v, w_cmp_v1, w_cmp_v2), tail=tail)
    yb = _nsa_attn(qr, kc, vct, ks, vst, vwt, proj3, small3)

    mk, mv = _mem_prep(mem, mem_norm_gain.reshape(1, D), w_mem_kv.astype(BF16), mem_k_norm.reshape(1, -1))
    out = _final(x2, ya.reshape(B * S, D), yb.reshape(B * S, D), proj2, mk, mv, mem_q_norm.reshape(1, -1),
                 w_branch.astype(BF16), w_out.astype(BF16), S)
    return out.reshape(B, S, D)


def kernel(x, mem, positions, norm_gain, mem_norm_gain, w_in, w_gla_alpha, b_gla_alpha, gla_out_norm, nsa_q_norm, nsa_k_norm, pe_cmp_k, pe_cmp_v, w_cmp_k1, w_cmp_k2, w_cmp_v1, w_cmp_v2, w_mem_kv, mem_q_norm, mem_k_norm, w_branch, w_out):
    h = x
    for l in range(norm_gain.shape[0]):
        h = _layer(h, mem, positions, norm_gain[l], mem_norm_gain[l], w_in[l], w_gla_alpha[l], b_gla_alpha[l],
                   gla_out_norm[l], nsa_q_norm[l], nsa_k_norm[l], pe_cmp_k[l], pe_cmp_v[l], w_cmp_k1[l],
                   w_cmp_k2[l], w_cmp_v1[l], w_cmp_v2[l], w_mem_kv[l], mem_q_norm[l], mem_k_norm[l],
                   w_branch[l], w_out[l])
    return h
```

```python
import functools

import numpy as np
import jax
import jax.numpy as jnp
from jax import lax
from jax.experimental import pallas as pl
from jax.experimental.pallas import tpu as pltpu

F32 = jnp.float32
BF16 = jnp.bfloat16

D_MODEL = 1024
ROPE_THETA = 10000.0
EPS = 1e-6
NEG = -1e30

GLA_HEADS = 4
GLA_DK = 128
GLA_DV = 256
GLA_RANK = 16
GLA_TAU = 16.0
GLA_CHUNK = 64
GLA_SUB = 16

NSA_HEADS = 16
NSA_KV_HEADS = 4
NSA_GROUP = 4
NSA_HD = 64
CMP_LEN = 32
CMP_STRIDE = 16
CMP_HIDDEN = 256
SLC_BLOCK = 64
SLC_TOPN = 16
WINDOW = 512
N_NSA_BRANCH = 3
SLC_LANES = 128
SLC_CHUNK = 512
VT_ROWS = NSA_HD + 16
LOG2E = 1.4426950408889634
WIN_ALIGN = 128
NSA_STEP_HEADS = 2
NSA_STEP_BLOCKS = 4
WIN_KEYS = WINDOW + max(NSA_STEP_BLOCKS * SLC_BLOCK, WIN_ALIGN)
KAUG = SLC_LANES + 2 * NSA_HD

MEM_HEADS = 4
MEM_HD = 256

C_GV, C_GG, C_NQ, C_NGATE, C_MQ, C_MG, C_MERGE = 0, 1024, 2048, 3072, 4096, 5120, 6144
C_NKV, C_GQ, C_GK = 9216, 10752, 11264
NP = 12288
CS_LR, CS_NSG, NP_SMALL = 0, 128, 256

VMEM_LIMIT = 48 * 1024 * 1024

NT = (((1,), (1,)), ((), ()))
TN = (((0,), (0,)), ((), ()))


def _cp(sem):
    return pltpu.CompilerParams(dimension_semantics=sem, vmem_limit_bytes=VMEM_LIMIT)


def _silu(x):
    return x * (1.0 / (1.0 + jnp.exp(-x)))


def _sigmoid(x):
    return 1.0 / (1.0 + jnp.exp(-x))


def _proj_kernel(x_ref, g_ref, w_ref, ws_ref, o_ref, os_ref, xn_ref):
    @pl.when(pl.program_id(1) == 0)
    def _():
        x = x_ref[...]
        ms = jnp.mean(x * x, axis=-1, keepdims=True)
        xn = (x * lax.rsqrt(ms + EPS) * g_ref[...]).astype(BF16)
        xn_ref[...] = xn
        os_ref[...] = jnp.dot(xn, ws_ref[...], preferred_element_type=F32)

    o_ref[...] = jnp.dot(xn_ref[...], w_ref[...], preferred_element_type=F32).astype(o_ref.dtype)


def _proj(x2, gain, w_all, w_small, tm=1024, tn=2048):
    M = x2.shape[0]
    ns = w_small.shape[1]
    return pl.pallas_call(
        _proj_kernel,
        grid=(M // tm, NP // tn),
        in_specs=[
            pl.BlockSpec((tm, D_MODEL), lambda i, j: (i, 0)),
            pl.BlockSpec((1, D_MODEL), lambda i, j: (0, 0)),
            pl.BlockSpec((D_MODEL, tn), lambda i, j: (0, j)),
            pl.BlockSpec((D_MODEL, ns), lambda i, j: (0, 0)),
        ],
        out_specs=[pl.BlockSpec((tm, tn), lambda i, j: (i, j)), pl.BlockSpec((tm, ns), lambda i, j: (i, 0))],
        out_shape=[jax.ShapeDtypeStruct((M, NP), BF16), jax.ShapeDtypeStruct((M, ns), F32)],
        scratch_shapes=[pltpu.VMEM((tm, D_MODEL), BF16)],
        compiler_params=_cp(("parallel", "arbitrary")),
        name="proj",
    )(x2, gain, w_all, w_small)


def _split2(x):
    hi = x.astype(BF16)
    return hi, (x - hi.astype(F32)).astype(BF16)


def _gla_kernel(q_ref, k_ref, v_ref, gate_ref, lr_ref, wa_ref, ba_ref, gn_ref, o_ref, st_ref, g_ref, kf_ref, *, n_chunks):
    C, SB, H, DK, DV = GLA_CHUNK, GLA_SUB, GLA_HEADS, GLA_DK, GLA_DV

    @pl.when(pl.program_id(1) == 0)
    def _():
        st_ref[...] = jnp.zeros_like(st_ref)

    gn = gn_ref[...]
    ri = lax.broadcasted_iota(jnp.int32, (C, C), 0)
    ci = lax.broadcasted_iota(jnp.int32, (C, C), 1)
    tri = (ri >= ci).astype(BF16)
    si = lax.broadcasted_iota(jnp.int32, (SB, SB), 0)
    sj = lax.broadcasted_iota(jnp.int32, (SB, SB), 1)
    sub_causal = si >= sj

    lr_hi, lr_lo = _split2(lr_ref[0])
    wa_hi, wa_lo = _split2(wa_ref[...])
    z = (jnp.dot(lr_hi, wa_hi, preferred_element_type=F32) + jnp.dot(lr_hi, wa_lo, preferred_element_type=F32)
         + jnp.dot(lr_lo, wa_hi, preferred_element_type=F32)) + ba_ref[...]
    la = -(jnp.maximum(-z, 0.0) + jnp.log(1.0 + jnp.exp(-jnp.abs(z)))) * (LOG2E / GLA_TAU)
    la1 = la.astype(BF16)
    la2, la3 = _split2(la - la1.astype(F32))
    for c in range(n_chunks):
        r = slice(c * C, (c + 1) * C)
        g_ref[r, :] = (jnp.dot(tri, la1[r], preferred_element_type=F32) + jnp.dot(tri, la2[r], preferred_element_type=F32)
                       + jnp.dot(tri, la3[r], preferred_element_type=F32))

    kf_ref[...] = k_ref[0].astype(F32)

    def chunk(c):
        rows = pl.ds(c * C, C)
        for h in range(H):
            kl = slice(h * DK, (h + 1) * DK)
            vl = slice(h * DV, (h + 1) * DV)
            q = q_ref[0, rows, kl].astype(F32) * (DK ** -0.5)
            k = kf_ref[rows, kl]
            vb = v_ref[0, rows, vl]
            g = g_ref[rows, kl]

            outs = []
            for i in range(C // SB):
                lo = i * SB
                gi = g[lo:lo + SB]
                qi = q[lo:lo + SB]
                ki = k[lo:lo + SB]
                HB = SB // 2
                a_top = jnp.zeros((HB, SB), F32)
                a_bot = jnp.zeros((HB, SB), F32)
                for jj in range(SB):
                    row = slice(lo + jj, lo + jj + 1)
                    k_row = kf_ref.at[rows, kl][row, :]
                    g_row = g_ref.at[rows, kl][row, :]
                    qs, gs = (qi, gi) if jj < HB else (qi[HB:], gi[HB:])
                    w = (qs * k_row) * jnp.exp2(jnp.minimum(gs - g_row, 0.0))
                    col = jnp.sum(w, axis=-1, keepdims=True)
                    if jj < HB:
                        a_top = jnp.where(sj[:HB] == jj, col[:HB], a_top)
                        a_bot = jnp.where(sj[:HB] == jj, col[HB:], a_bot)
                    else:
                        a_bot = jnp.where(sj[:HB] == jj, col, a_bot)
                a_d = jnp.where(sub_causal, jnp.concatenate([a_top, a_bot], axis=0), 0.0)
                o_i = jnp.dot(a_d.astype(BF16), vb[lo:lo + SB], preferred_element_type=F32)
                if i > 0:
                    r = g[lo:lo + 1]
                    qt = qi * jnp.exp2(gi - r)
                    kt = k[:lo] * jnp.exp2(r - g[:lo])
                    a_o = lax.dot_general(qt.astype(BF16), kt.astype(BF16), NT, preferred_element_type=F32)
                    o_i = o_i + jnp.dot(a_o.astype(BF16), vb[:lo], preferred_element_type=F32)
                outs.append(o_i)
            o = jnp.concatenate(outs, axis=0)

            st = st_ref[h]
            qg = q * jnp.exp2(g)
            o = o + lax.dot_general(qg.astype(BF16), st.astype(BF16), NT, preferred_element_type=F32)
            gl = g[C - 1:C]
            kd = k * jnp.exp2(gl - g)
            st_ref[h] = st * jnp.exp2(gl) + lax.dot_general(vb, kd.astype(BF16), TN, preferred_element_type=F32)

            ms = jnp.mean(o * o, axis=-1, keepdims=True)
            y = o * lax.rsqrt(ms + EPS) * gn
            o_ref[0, rows, vl] = (y * _silu(gate_ref[0, rows, vl].astype(F32))).astype(o_ref.dtype)

    for c in range(n_chunks):
        chunk(c)


def _gla(proj3, small3, wa, ba, gn, tr=512):
    B, S, _ = proj3.shape
    H, DK, DV = GLA_HEADS, GLA_DK, GLA_DV
    kern = functools.partial(_gla_kernel, n_chunks=tr // GLA_CHUNK)
    return pl.pallas_call(
        kern,
        grid=(B, S // tr),
        in_specs=[
            pl.BlockSpec((1, tr, H * DK), lambda b, r: (b, r, C_GQ // (H * DK))),
            pl.BlockSpec((1, tr, H * DK), lambda b, r: (b, r, C_GK // (H * DK))),
            pl.BlockSpec((1, tr, H * DV), lambda b, r: (b, r, C_GV // (H * DV))),
            pl.BlockSpec((1, tr, H * DV), lambda b, r: (b, r, C_GG // (H * DV))),
            pl.BlockSpec((1, tr, 128), lambda b, r: (b, r, CS_LR // 128)),
            pl.BlockSpec((128, H * DK), lambda b, r: (0, 0)),
            pl.BlockSpec((1, H * DK), lambda b, r: (0, 0)),
            pl.BlockSpec((1, DV), lambda b, r: (0, 0)),
        ],
        out_specs=pl.BlockSpec((1, tr, H * DV), lambda b, r: (b, r, 0)),
        out_shape=jax.ShapeDtypeStruct((B, S, H * DV), BF16),
        scratch_shapes=[pltpu.VMEM((H, DV, DK), F32), pltpu.VMEM((tr, H * DK), F32), pltpu.VMEM((tr, H * DK), F32)],
        compiler_params=_cp(("parallel", "arbitrary")),
        name="gla",
    )(proj3, proj3, proj3, proj3, small3, wa, ba, gn)


def _group_meansq(x, gmat):
    sq = x * x
    hi = sq.astype(BF16)
    lo = (sq - hi.astype(F32)).astype(BF16)
    s = jnp.dot(hi, gmat, preferred_element_type=F32) + jnp.dot(lo, gmat, preferred_element_type=F32)
    return s * (1.0 / NSA_HD)


def _rope(x, cos, sin, first_half):
    w = x.shape[-1]
    rot = jnp.where(first_half, -pltpu.roll(x, w - NSA_HD // 2, 1), pltpu.roll(x, NSA_HD // 2, 1))
    return x * cos + rot * sin


def _nsa_prep_kernel(q_ref, kvc_ref, kvs_ref, kvw_ref, pos_ref, inv_ref, qn_ref, kn_ref, gm_ref,
                     qr_ref, kc_ref, vc_ref, ks_ref, vst_ref, vwt_ref):
    tr = q_ref.shape[1]
    W = NSA_KV_HEADS * NSA_HD
    HD = NSA_HD
    gmat = gm_ref[...]
    ang = pos_ref[0].astype(F32) * inv_ref[...]
    cos1, sin1 = jnp.cos(ang), jnp.sin(ang)
    cos = jnp.concatenate([cos1, cos1], axis=1)
    sin = jnp.concatenate([sin1, sin1], axis=1)
    lane = lax.broadcasted_iota(jnp.int32, (tr, W), 1)
    first_half = (lane % HD) < (HD // 2)

    def norm_rope(x, gain):
        y = x * lax.rsqrt(_group_meansq(x, gmat) + EPS) * gain
        return _rope(y, cos, sin, first_half)

    qn = qn_ref[...]
    for s in range(NSA_HEADS * HD // W):
        xq = q_ref[0, :, s * W:(s + 1) * W].astype(F32)
        qr_ref[0, :, s * W:(s + 1) * W] = (norm_rope(xq, qn) * (HD ** -0.5 * LOG2E)).astype(qr_ref.dtype)

    kc = norm_rope(kvc_ref[0, :, :W].astype(F32), kn_ref[0:1, :]).astype(BF16)
    vc = kvc_ref[0, :, W:]
    for h in range(NSA_KV_HEADS):
        kc_ref[0, h] = kc[:, h * HD:(h + 1) * HD]
        vc_ref[0, h] = vc[:, h * HD:(h + 1) * HD]

    ks = norm_rope(kvs_ref[0, :, :W].astype(F32), kn_ref[1:2, :]).astype(BF16)
    kw = norm_rope(kvw_ref[0, :, :W].astype(F32), kn_ref[2:3, :]).astype(BF16)
    row_blk = (pl.program_id(1) * tr + lax.broadcasted_iota(jnp.int32, (tr, SLC_LANES), 0)) // SLC_BLOCK
    onehot = (row_blk == lax.broadcasted_iota(jnp.int32, (tr, SLC_LANES), 1)).astype(BF16)
    vst = kvs_ref[0, :, W:].astype(F32).T
    for h in range(NSA_KV_HEADS):
        ks_ref[0, h] = jnp.concatenate([onehot, ks[:, h * HD:(h + 1) * HD], kw[:, h * HD:(h + 1) * HD]], axis=1)
        for j in range(tr // SLC_CHUNK):
            vst_ref[0, h, j, :HD] = vst[h * HD:(h + 1) * HD, j * SLC_CHUNK:(j + 1) * SLC_CHUNK].astype(BF16)
            vst_ref[0, h, j, HD:] = jnp.ones((VT_ROWS - HD, SLC_CHUNK), BF16)

    vwt = kvw_ref[0, :, W:].astype(F32).T
    for h in range(NSA_KV_HEADS):
        for j in range(tr // WIN_ALIGN):
            vwt_ref[0, h, j, :HD] = vwt[h * HD:(h + 1) * HD, j * WIN_ALIGN:(j + 1) * WIN_ALIGN].astype(BF16)
            vwt_ref[0, h, j, HD:] = jnp.ones((VT_ROWS - HD, WIN_ALIGN), BF16)


def _nsa_prep(proj3, pos3, inv, qn, kn, gmat, tr=512):
    B, S, _ = proj3.shape
    Hk, HD = NSA_KV_HEADS, NSA_HD
    hm = lambda w, dt: jax.ShapeDtypeStruct((B, Hk, S, w), dt)
    hspec = lambda w: pl.BlockSpec((1, Hk, tr, w), lambda b, r: (b, 0, r, 0))
    tspec = lambda c, rows=HD: pl.BlockSpec((1, Hk, tr // c, rows, c), lambda b, r: (b, 0, r, 0, 0))
    return pl.pallas_call(
        _nsa_prep_kernel,
        grid=(B, S // tr),
        in_specs=[
            pl.BlockSpec((1, tr, 1024), lambda b, r: (b, r, C_NQ // 1024)),
            pl.BlockSpec((1, tr, 512), lambda b, r: (b, r, C_NKV // 512)),
            pl.BlockSpec((1, tr, 512), lambda b, r: (b, r, C_NKV // 512 + 1)),
            pl.BlockSpec((1, tr, 512), lambda b, r: (b, r, C_NKV // 512 + 2)),
            pl.BlockSpec((1, tr, 1), lambda b, r: (b, r, 0)),
            pl.BlockSpec((1, 128), lambda b, r: (0, 0)),
            pl.BlockSpec((1, 256), lambda b, r: (0, 0)),
            pl.BlockSpec((3, 256), lambda b, r: (0, 0)),
            pl.BlockSpec((256, 256), lambda b, r: (0, 0)),
        ],
        out_specs=[pl.BlockSpec((1, tr, 1024), lambda b, r: (b, r, 0)), hspec(HD), hspec(HD), hspec(KAUG),
                   tspec(SLC_CHUNK, VT_ROWS), tspec(WIN_ALIGN, VT_ROWS)],
        out_shape=[jax.ShapeDtypeStruct((B, S, 1024), BF16), hm(HD, BF16), hm(HD, BF16), hm(KAUG, BF16),
                   jax.ShapeDtypeStruct((B, Hk, S // SLC_CHUNK, VT_ROWS, SLC_CHUNK), BF16),
                   jax.ShapeDtypeStruct((B, Hk, S // WIN_ALIGN, VT_ROWS, WIN_ALIGN), BF16)],
        compiler_params=_cp(("parallel", "parallel")),
        name="nsa_prep",
    )(proj3, proj3, proj3, proj3, pos3, inv, qn, kn, gmat)


def _compress_kernel(t_ref, pe_ref, w1a_ref, w1b_ref, w2_ref, *rest, transposed):
    tail_ref, o_ref = rest if transposed else (None,) + rest
    t = t_ref[0, 0]
    w1a, w1b = w1a_ref[...], w1b_ref[...]
    u = jnp.dot(t, w1a, preferred_element_type=F32)
    v = jnp.dot(t, w1b, preferred_element_type=F32)
    pe = pe_ref[...].astype(BF16)
    c = (jnp.dot(pe[0:8], w1a, preferred_element_type=F32) + jnp.dot(pe[8:16], w1b, preferred_element_type=F32))[0:1]
    n = v.shape[0]
    h = u + pltpu.roll(v, n - 1, 0) + c
    h = jax.nn.gelu(h).astype(BF16)
    if transposed:
        hd = w2_ref.shape[0]
        o_ref[0, 0, :hd] = lax.dot_general(w2_ref[...], h, NT, preferred_element_type=F32).astype(o_ref.dtype)
        o_ref[0, 0, hd:] = tail_ref[...]
    else:
        o_ref[0, 0] = jnp.dot(h, w2_ref[...], preferred_element_type=F32).astype(o_ref.dtype)


def _compress(tok, pe2, w1a, w1b, w2, tail=None):
    B, Hk, S, HD = tok.shape
    n = S // CMP_STRIDE
    t2 = tok.reshape(B, Hk, n, CMP_STRIDE * HD)
    transposed = tail is not None
    oshape = (HD + tail.shape[0], n) if transposed else (n, HD)
    w2 = w2.T if transposed else w2
    extra = ([tail], [pl.BlockSpec(tail.shape, lambda b, h: (0, 0))]) if transposed else ([], [])
    return pl.pallas_call(
        functools.partial(_compress_kernel, transposed=transposed),
        grid=(B, Hk),
        in_specs=[
            pl.BlockSpec((1, 1, n, CMP_STRIDE * HD), lambda b, h: (b, h, 0, 0)),
            pl.BlockSpec((16, CMP_STRIDE * HD), lambda b, h: (0, 0)),
            pl.BlockSpec((CMP_STRIDE * HD, CMP_HIDDEN), lambda b, h: (0, 0)),
            pl.BlockSpec((CMP_STRIDE * HD, CMP_HIDDEN), lambda b, h: (0, 0)),
            pl.BlockSpec(w2.shape, lambda b, h: (0, 0)),
        ] + extra[1],
        out_specs=pl.BlockSpec((1, 1) + oshape, lambda b, h: (b, h, 0, 0)),
        out_shape=jax.ShapeDtypeStruct((B, Hk) + oshape, BF16),
        compiler_params=_cp(("parallel", "parallel")),
        name="compress_v" if transposed else "compress_k",
    )(t2, pe2, w1a, w1b, w2, *extra[0])


def _nsa_attn_kernel(q_ref, kc_ref, vct_ref, ks_ref, vst_ref, vwt_ref, gt_ref, gate_ref, tri_ref, cmask_ref, wmask_ref,
                     o_ref,
                     s_ref, e_ref):
    G, HD, QB, NH, NB = NSA_GROUP, NSA_HD, SLC_BLOCK, NSA_STEP_HEADS, NSA_STEP_BLOCKS
    R = NB * G * QB
    heads = range(NH)
    hk0 = pl.program_id(1) * NH
    step = pl.program_id(2)
    t0 = step * (NB * QB)
    q = [jnp.concatenate([q_ref[0, a * QB:(a + 1) * QB, (h * G + g) * HD:(h * G + g + 1) * HD]
                          for a in range(NB) for g in range(G)], axis=0)
         for h in heads]

    nc = kc_ref.shape[2]
    cmask = cmask_ref[pl.ds(pl.multiple_of(nc - t0 // CMP_STRIDE, CMP_STRIDE), nc), :]
    for h in heads:
        sm = lax.dot_general(kc_ref[0, h], q[h], NT, preferred_element_type=F32) + cmask
        m = jnp.maximum(jnp.max(sm, axis=0, keepdims=True), 0.1 * NEG)
        e_ref[h] = jnp.exp2(sm - m).astype(BF16)
    QL = max(NB * QB, 2 * QB)
    blk = lax.broadcasted_iota(jnp.int32, (SLC_LANES, QL), 0)
    qlane = lax.broadcasted_iota(jnp.int32, (1, QL), 1)
    cur = step * NB + (qlane // QB) % NB
    forced = (blk == 0) | (blk == cur) | (blk == cur - 1)
    o_cmp, score = [], []
    for h in heads:
        oc = jnp.dot(vct_ref[0, h], e_ref[h], preferred_element_type=F32)
        den = oc[HD:HD + 1]
        rden = 1.0 / jnp.where(den > 0.0, den, 1.0)
        o_cmp.append(oc[:HD] * rden)
        impf = oc[VT_ROWS:] * rden
        parts = []
        for a in range(NB):
            pa = impf[:, a * G * QB:(a + 1) * G * QB]
            p2 = pa[:, :2 * QB] + pa[:, 2 * QB:]
            parts.append(p2 + pltpu.roll(p2, QB, 1))
        first_half = qlane[:, :2 * QB] < QB
        imp = parts[0] if NB == 1 else jnp.concatenate(
            [jnp.where(first_half, parts[a], parts[a + 1]) for a in range(0, NB, 2)], axis=1)
        score.append(jnp.where(forced, -jnp.inf, jnp.where(blk <= cur, imp, NEG)))

    wc = jnp.maximum(t0 - WINDOW, 0) // WIN_ALIGN
    w0 = pl.multiple_of(wc * WIN_ALIGN, WIN_ALIGN)
    wmask = wmask_ref[pl.ds(pl.multiple_of(jnp.maximum(WINDOW - t0, 0), NB * QB), WIN_KEYS), :]
    o_win = []
    for h in heads:
        q_win = jnp.concatenate([jnp.zeros((R, SLC_LANES + HD), BF16), q[h]], axis=1)
        sw = lax.dot_general(ks_ref[0, h, pl.ds(w0, WIN_KEYS), :], q_win, NT, preferred_element_type=F32)
        sw = sw + wmask
        pw = jnp.exp2(sw - jnp.max(sw, axis=0, keepdims=True))
        vv = jnp.concatenate([vwt_ref[0, h, wc + j] for j in range(WIN_KEYS // WIN_ALIGN)], axis=1)
        ow = jnp.dot(vv, pw.astype(BF16), preferred_element_type=F32)
        o_win.append(ow[:HD] * (1.0 / ow[HD:HD + 1]))

    few = cur < SLC_TOPN
    causal = blk <= cur
    blkf = blk.astype(F32)

    def pick(sc):
        mx = jnp.max(sc, axis=0, keepdims=True)
        first = jnp.min(jnp.where(sc == mx, blkf, float(SLC_LANES)), axis=0, keepdims=True)
        return jnp.where(blkf == first, -jnp.inf, sc)

    for _ in range(SLC_TOPN - 3):
        score = [pick(sc) for sc in score]
    q_aug = []
    for h in heads:
        chosen = forced | ((few | (score[h] == -jnp.inf)) & causal)
        bias_t = jnp.where(chosen, 0.0, NEG)
        bias = bias_t.T.astype(BF16)
        rows = jnp.concatenate([bias[a * QB:(a + 1) * QB] for a in range(NB) for g in range(G)], axis=0)
        q_aug.append(jnp.concatenate([rows, q[h], jnp.zeros((R, HD), BF16)], axis=1))

    def scores(c, slot):
        k0 = pl.multiple_of(c * SLC_CHUNK, SLC_CHUNK)
        for h in heads:
            s_ref[slot, h] = lax.dot_general(ks_ref[0, h, pl.ds(k0, SLC_CHUNK), :], q_aug[h], NT, preferred_element_type=F32)

    def absorb(c, slot, carry):
        out = []
        for h in heads:
            m_i, acc = carry[h]
            sc = s_ref[slot, h]
            m_new = jnp.maximum(m_i, jnp.max(sc, axis=0, keepdims=True))
            pp = jnp.exp2(sc - m_new).astype(BF16)
            acc = jnp.exp2(m_i - m_new) * acc + jnp.dot(vst_ref[0, h, c], pp, preferred_element_type=F32)
            out.append((m_new, acc))
        return tuple(out)

    diag = t0 // SLC_CHUNK
    n_pairs = diag // 2
    scores(0, 0)

    def pair(j, carry):
        scores(2 * j + 1, 1)
        carry = absorb(2 * j, 0, carry)
        scores(2 * j + 2, 0)
        return absorb(2 * j + 1, 1, carry)

    init = tuple((jnp.full((1, R), NEG, F32), jnp.zeros((VT_ROWS, R), F32)) for h in heads)
    carry = lax.fori_loop(0, n_pairs, pair, init)
    scores(2 * n_pairs + 1, 1)
    r0 = pl.multiple_of(t0 - diag * SLC_CHUNK, NB * QB)
    tri = tri_ref[...]
    for h in heads:
        s_ref[diag % 2, h, pl.ds(r0, NB * QB), :] += tri
    carry = absorb(2 * n_pairs, 0, carry)
    carry = absorb(2 * n_pairs + 1, 1, carry)

    sig = _sigmoid(gt_ref[0])
    glane = lax.broadcasted_iota(jnp.int32, (QB, 128), 1)
    outs = [[] for a in range(NB)]
    for h in heads:
        acc_s = carry[h][1]
        oc, os_, ow = o_cmp[h].T, (acc_s[:HD] * (1.0 / acc_s[HD:HD + 1])).T, o_win[h].T
        for a in range(NB):
            sig_a = sig[a * QB:(a + 1) * QB]
            for g in range(G):
                base = ((hk0 + h) * G + g) * N_NSA_BRANCH
                gc, gs, gw = [jnp.sum(jnp.where(glane == base + b, sig_a, 0.0), axis=-1, keepdims=True) for b in range(3)]
                r = slice((a * G + g) * QB, (a * G + g + 1) * QB)
                outs[a].append(gc * oc[r] + gs * os_[r] + gw * ow[r])
    o = jnp.concatenate([jnp.concatenate(oa, axis=1) for oa in outs], axis=0)
    o_ref[0] = (o * _silu(gate_ref[0].astype(F32))).astype(o_ref.dtype)


def _nsa_attn(qr, kc, vct, ks, vst, vwt, proj3, small3):
    B, S, _ = qr.shape
    Hk, HD, NH = NSA_KV_HEADS, NSA_HD, NSA_STEP_HEADS
    TQ = NSA_STEP_BLOCKS * SLC_BLOCK
    r = np.arange(TQ)[:, None]
    ln = np.arange(NSA_GROUP * TQ)[None, :]
    a, qi = ln // (NSA_GROUP * SLC_BLOCK), ln % SLC_BLOCK
    tri = jnp.asarray(np.where((r // SLC_BLOCK != a) | (r % SLC_BLOCK <= qi), 0.0, NEG), F32)
    nc = kc.shape[2]
    tl = a * SLC_BLOCK + qi
    u = np.arange(2 * nc)[:, None]
    cmask = jnp.asarray(np.where(u - nc <= (tl - (CMP_LEN - 1)) // CMP_STRIDE, 0.0, NEG), F32)
    u = np.arange(WINDOW + WIN_KEYS)[:, None]
    wmask = jnp.asarray(np.where((u > tl) & (u <= WINDOW + tl), 0.0, NEG), F32)
    const = lambda c: pl.BlockSpec(c.shape, lambda b, h, t: (0, 0), pipeline_mode=pl.Buffered(1))
    W = NH * NSA_GROUP * HD
    full = lambda a: pl.BlockSpec((1, NH) + a.shape[2:], lambda b, h, t: (b, h) + (0,) * (a.ndim - 2),
                                  pipeline_mode=pl.Buffered(1))
    return pl.pallas_call(
        _nsa_attn_kernel,
        grid=(B, Hk // NH, S // TQ),
        in_specs=[
            pl.BlockSpec((1, TQ, W), lambda b, h, t: (b, t, h)),
            full(kc), full(vct), full(ks), full(vst), full(vwt),
            pl.BlockSpec((1, TQ, 128), lambda b, h, t: (b, t, CS_NSG // 128)),
            pl.BlockSpec((1, TQ, W), lambda b, h, t: (b, t, C_NGATE // W + h)),
            const(tri), const(cmask), const(wmask),
        ],
        out_specs=pl.BlockSpec((1, TQ, W), lambda b, h, t: (b, t, h)),
        out_shape=jax.ShapeDtypeStruct((B, S, NSA_HEADS * HD), BF16),
        scratch_shapes=[pltpu.VMEM((2, NH, SLC_CHUNK, NSA_GROUP * TQ), F32), pltpu.VMEM((NH, nc, NSA_GROUP * TQ), BF16)],
        compiler_params=_cp(("parallel", "parallel", "arbitrary")),
        name="nsa_attn",
    )(qr, kc, vct, ks, vst, vwt, small3, proj3, tri, cmask, wmask)


def _head_rms(x, gain, hd):
    outs = []
    for h in range(x.shape[-1] // hd):
        xh = x[:, h * hd:(h + 1) * hd]
        ms = jnp.mean(xh * xh, axis=-1, keepdims=True)
        outs.append(xh * lax.rsqrt(ms + EPS) * gain)
    return jnp.concatenate(outs, axis=1)


def _mem_prep_kernel(mem_ref, g_ref, w_ref, kn_ref, mk_ref, mv_ref):
    x = mem_ref[0]
    ms = jnp.mean(x * x, axis=-1, keepdims=True)
    xn = (x * lax.rsqrt(ms + EPS) * g_ref[...]).astype(BF16)
    kv = jnp.dot(xn, w_ref[...], preferred_element_type=F32)
    W = MEM_HEADS * MEM_HD
    mk_ref[0] = _head_rms(kv[:, :W], kn_ref[...], MEM_HD).astype(mk_ref.dtype)
    mv_ref[0] = kv[:, W:].astype(mv_ref.dtype)


def _mem_prep(mem, gain, w_kv, kn):
    B, N, D = mem.shape
    W = MEM_HEADS * MEM_HD
    return pl.pallas_call(
        _mem_prep_kernel,
        grid=(B,),
        in_specs=[
            pl.BlockSpec((1, N, D), lambda b: (b, 0, 0)),
            pl.BlockSpec((1, D), lambda b: (0, 0)),
            pl.BlockSpec((D, 2 * W), lambda b: (0, 0)),
            pl.BlockSpec((1, MEM_HD), lambda b: (0, 0)),
        ],
        out_specs=[pl.BlockSpec((1, N, W), lambda b: (b, 0, 0))] * 2,
        out_shape=[jax.ShapeDtypeStruct((B, N, W), BF16)] * 2,
        compiler_params=_cp(("parallel",)),
        name="mem_prep",
    )(mem, gain, w_kv, kn)


def _final_kernel(x_ref, ya_ref, yb_ref, mq_ref, mg_ref, mr_ref, mk_ref, mv_ref, qn_ref, wb_ref, wo_ref, o_ref):
    mq = _head_rms(mq_ref[...].astype(F32), qn_ref[...], MEM_HD)
    mk = mk_ref[0]
    mv = mv_ref[0]
    heads = []
    for h in range(MEM_HEADS):
        sl = slice(h * MEM_HD, (h + 1) * MEM_HD)
        s = lax.dot_general(mq[:, sl].astype(BF16), mk[:, sl], NT, preferred_element_type=F32) * (MEM_HD ** -0.5)
        m = jnp.max(s, axis=-1, keepdims=True)
        p = jnp.exp(s - m)
        o = jnp.dot(p.astype(BF16), mv[:, sl], preferred_element_type=F32) / jnp.sum(p, axis=-1, keepdims=True)
        heads.append(o)
    ym = jnp.concatenate(heads, axis=1) * _silu(mg_ref[...].astype(F32))

    mixed = None
    for c, y in enumerate((ya_ref[...], yb_ref[...], ym.astype(BF16))):
        z = jnp.dot(y, wb_ref[c], preferred_element_type=F32)
        term = _sigmoid(mr_ref[:, c * D_MODEL:(c + 1) * D_MODEL].astype(F32)) * z
        mixed = term if mixed is None else mixed + term
    o_ref[...] = x_ref[...] + jnp.dot(mixed.astype(BF16), wo_ref[...], preferred_element_type=F32)


def _final(x2, ya2, yb2, proj2, mk, mv, qn, wb, wo, S, tr=512):
    M, D = x2.shape
    N = mk.shape[1]
    nb = S // tr
    row = lambda c: pl.BlockSpec((tr, D), lambda i: (i, c))
    return pl.pallas_call(
        _final_kernel,
        grid=(M // tr,),
        in_specs=[
            row(0), row(0), row(0),
            row(C_MQ // D), row(C_MG // D),
            pl.BlockSpec((tr, 3 * D), lambda i: (i, C_MERGE // (3 * D))),
            pl.BlockSpec((1, N, D), lambda i: (i // nb, 0, 0)),
            pl.BlockSpec((1, N, D), lambda i: (i // nb, 0, 0)),
            pl.BlockSpec((1, MEM_HD), lambda i: (0, 0)),
            pl.BlockSpec((3, D, D), lambda i: (0, 0, 0), pipeline_mode=pl.Buffered(1)),
            pl.BlockSpec((D, D), lambda i: (0, 0), pipeline_mode=pl.Buffered(1)),
        ],
        out_specs=pl.BlockSpec((tr, D), lambda i: (i, 0)),
        out_shape=jax.ShapeDtypeStruct((M, D), F32),
        compiler_params=_cp(("parallel",)),
        name="final",
    )(x2, ya2, yb2, proj2, proj2, proj2, mk, mv, qn, wb, wo)


def _overlap_matrix_t(S):
    n_cmp = (S - CMP_LEN) // CMP_STRIDE + 1
    n_slc = S // SLC_BLOCK
    cs = np.arange(n_cmp)[:, None] * CMP_STRIDE
    ss = np.arange(n_slc)[None, :] * SLC_BLOCK
    ov = np.clip(np.minimum(cs + CMP_LEN, ss + SLC_BLOCK) - np.maximum(cs, ss), 0, None) / CMP_LEN
    out = np.zeros((SLC_LANES, S // CMP_STRIDE), np.float32)
    out[:n_slc, :n_cmp] = ov.T
    return jnp.asarray(out, BF16)


def _pad_cols(w, n):
    return jnp.pad(w, ((0, 0), (0, n - w.shape[1])))


def _layer(x, mem, positions, norm_gain, mem_norm_gain, w_in, w_gla_alpha, b_gla_alpha, gla_out_norm,
           nsa_q_norm, nsa_k_norm, pe_cmp_k, pe_cmp_v, w_cmp_k1, w_cmp_k2, w_cmp_v1, w_cmp_v2,
           w_mem_kv, mem_q_norm, mem_k_norm, w_branch, w_out):
    B, S, D = x.shape
    assert D == D_MODEL and S % (2 * SLC_CHUNK) == 0 and S >= WIN_KEYS and S // SLC_BLOCK <= SLC_LANES
    assert SLC_TOPN % NSA_STEP_BLOCKS == 0 and SLC_CHUNK % (NSA_STEP_BLOCKS * SLC_BLOCK) == 0

    o = np.cumsum([0, 512, 512, 1024, 16, 1024, 1024, 1536, 48, 1024, 1024, 1024, 3072])
    runs = [(2, 3), (4, 6), (8, 12), (6, 7), (0, 2)]
    wide = [w_in[:, o[a]:o[b]].astype(BF16) for a, b in runs]
    w_all = jnp.concatenate(wide + [jnp.zeros((D, NP - sum(w.shape[1] for w in wide)), BF16)], axis=1)
    glr, nsg = w_in[:, o[3]:o[4]], w_in[:, o[7]:o[8]]
    w_small = jnp.concatenate([_pad_cols(glr, 128), _pad_cols(nsg, 128)], axis=1).astype(BF16)

    x2 = x.reshape(B * S, D)
    proj2, small2 = _proj(x2, norm_gain.reshape(1, D), w_all, w_small)
    proj3 = proj2.reshape(B, S, NP)
    small3 = small2.reshape(B, S, NP_SMALL)

    wa = jnp.pad(w_gla_alpha, ((0, 128 - GLA_RANK), (0, 0)))
    ya = _gla(proj3, small3, wa, b_gla_alpha.reshape(1, -1), gla_out_norm.reshape(1, -1))

    half = NSA_HD // 2
    inv = ROPE_THETA ** (-jnp.arange(half, dtype=F32) / half)
    inv = jnp.tile(inv, 128 // half).reshape(1, 128)
    qn = jnp.tile(nsa_q_norm, NSA_KV_HEADS).reshape(1, -1)
    kn = jnp.tile(nsa_k_norm, (1, NSA_KV_HEADS))
    gid = np.arange(256) // NSA_HD
    gmat = jnp.asarray(gid[:, None] == gid[None, :], BF16)
    qr, kc_tok, vc_tok, ks, vst, vwt = _nsa_prep(proj3, positions.reshape(B, S, 1), inv, qn, kn, gmat)

    def cmp_args(pe, w1, w2):
        pe2 = pe.reshape(2, CMP_STRIDE * NSA_HD)
        pe16 = jnp.concatenate([jnp.broadcast_to(pe2[0:1], (8, pe2.shape[1])), jnp.broadcast_to(pe2[1:2], (8, pe2.shape[1]))], 0)
        w1f = w1.reshape(CMP_LEN * NSA_HD, CMP_HIDDEN).astype(BF16)
        return pe16, w1f[:CMP_STRIDE * NSA_HD], w1f[CMP_STRIDE * NSA_HD:], w2.astype(BF16)

    kc = _compress(kc_tok, *cmp_args(pe_cmp_k, w_cmp_k1, w_cmp_k2))
    tail = jnp.concatenate([jnp.ones((VT_ROWS - NSA_HD, S // CMP_STRIDE), BF16), _overlap_matrix_t(S)], axis=0)
    vct = _compress(vc_tok, *cmp_args(pe_cmp_v, w_cmp_v1, w_cmp_v2), tail=tail)
    yb = _nsa_attn(qr, kc, vct, ks, vst, vwt, proj3, small3)

    mk, mv = _mem_prep(mem, mem_norm_gain.reshape(1, D), w_mem_kv.astype(BF16), mem_k_norm.reshape(1, -1))
    out = _final(x2, ya.reshape(B * S, D), yb.reshape(B * S, D), proj2, mk, mv, mem_q_norm.reshape(1, -1),
                 w_branch.astype(BF16), w_out.astype(BF16), S)
    return out.reshape(B, S, D)


def kernel(x, mem, positions, norm_gain, mem_norm_gain, w_in, w_gla_alpha, b_gla_alpha, gla_out_norm, nsa_q_norm, nsa_k_norm, pe_cmp_k, pe_cmp_v, w_cmp_k1, w_cmp_k2, w_cmp_v1, w_cmp_v2, w_mem_kv, mem_q_norm, mem_k_norm, w_branch, w_out):
    h = x
    for l in range(norm_gain.shape[0]):
        h = _layer(h, mem, positions, norm_gain[l], mem_norm_gain[l], w_in[l], w_gla_alpha[l], b_gla_alpha[l],
                   gla_out_norm[l], nsa_q_norm[l], nsa_k_norm[l], pe_cmp_k[l], pe_cmp_v[l], w_cmp_k1[l],
                   w_cmp_k2[l], w_cmp_v1[l], w_cmp_v2[l], w_mem_kv[l], mem_q_norm[l], mem_k_norm[l],
                   w_branch[l], w_out[l])
    return h
```

```python
import functools

import numpy as np
import jax
import jax.numpy as jnp
from jax import lax
from jax.experimental import pallas as pl
from jax.experimental.pallas import tpu as pltpu

F32 = jnp.float32
BF16 = jnp.bfloat16

D_MODEL = 1024
ROPE_THETA = 10000.0
EPS = 1e-6
NEG = -1e30

GLA_HEADS = 4
GLA_DK = 128
GLA_DV = 256
GLA_RANK = 16
GLA_TAU = 16.0
GLA_CHUNK = 64
GLA_SUB = 16

NSA_HEADS = 16
NSA_KV_HEADS = 4
NSA_GROUP = 4
NSA_HD = 64
CMP_LEN = 32
CMP_STRIDE = 16
CMP_HIDDEN = 256
SLC_BLOCK = 64
SLC_TOPN = 16
WINDOW = 512
N_NSA_BRANCH = 3
SLC_LANES = 128
SLC_CHUNK = 512
VT_ROWS = NSA_HD + 16
LOG2E = 1.4426950408889634
WIN_ALIGN = 128
NSA_STEP_HEADS = 2
NSA_STEP_BLOCKS = 4
WIN_KEYS = WINDOW + max(NSA_STEP_BLOCKS * SLC_BLOCK, WIN_ALIGN)
KAUG = SLC_LANES + 2 * NSA_HD

MEM_HEADS = 4
MEM_HD = 256

C_GV, C_GG, C_NQ, C_NGATE, C_MQ, C_MG, C_MERGE = 0, 1024, 2048, 3072, 4096, 5120, 6144
C_NKV, C_GQ, C_GK = 9216, 10752, 11264
NP = 12288
CS_LR, CS_NSG, NP_SMALL = 0, 128, 256

VMEM_LIMIT = 48 * 1024 * 1024

NT = (((1,), (1,)), ((), ()))
TN = (((0,), (0,)), ((), ()))


def _cp(sem):
    return pltpu.CompilerParams(dimension_semantics=sem, vmem_limit_bytes=VMEM_LIMIT)


def _silu(x):
    return x * (1.0 / (1.0 + jnp.exp(-x)))


def _sigmoid(x):
    return 1.0 / (1.0 + jnp.exp(-x))


def _proj_kernel(x_ref, g_ref, w_ref, ws_ref, o_ref, os_ref, xn_ref):
    @pl.when(pl.program_id(1) == 0)
    def _():
        x = x_ref[...]
        ms = jnp.mean(x * x, axis=-1, keepdims=True)
        xn = (x * lax.rsqrt(ms + EPS) * g_ref[...]).astype(BF16)
        xn_ref[...] = xn
        os_ref[...] = jnp.dot(xn, ws_ref[...], preferred_element_type=F32)

    o_ref[...] = jnp.dot(xn_ref[...], w_ref[...], preferred_element_type=F32).astype(o_ref.dtype)


def _proj(x2, gain, w_all, w_small, tm=1024, tn=2048):
    M = x2.shape[0]
    ns = w_small.shape[1]
    return pl.pallas_call(
        _proj_kernel,
        grid=(M // tm, NP // tn),
        in_specs=[
            pl.BlockSpec((tm, D_MODEL), lambda i, j: (i, 0)),
            pl.BlockSpec((1, D_MODEL), lambda i, j: (0, 0)),
            pl.BlockSpec((D_MODEL, tn), lambda i, j: (0, j)),
            pl.BlockSpec((D_MODEL, ns), lambda i, j: (0, 0)),
        ],
        out_specs=[pl.BlockSpec((tm, tn), lambda i, j: (i, j)), pl.BlockSpec((tm, ns), lambda i, j: (i, 0))],
        out_shape=[jax.ShapeDtypeStruct((M, NP), BF16), jax.ShapeDtypeStruct((M, ns), F32)],
        scratch_shapes=[pltpu.VMEM((tm, D_MODEL), BF16)],
        compiler_params=_cp(("parallel", "arbitrary")),
        name="proj",
    )(x2, gain, w_all, w_small)


def _split2(x):
    hi = x.astype(BF16)
    return hi, (x - hi.astype(F32)).astype(BF16)


def _gla_kernel(q_ref, k_ref, v_ref, gate_ref, lr_ref, wa_ref, ba_ref, gn_ref, o_ref, st_ref, g_ref, kf_ref, *, n_chunks):
    C, SB, H, DK, DV = GLA_CHUNK, GLA_SUB, GLA_HEADS, GLA_DK, GLA_DV

    @pl.when(pl.program_id(1) == 0)
    def _():
        st_ref[...] = jnp.zeros_like(st_ref)

    gn = gn_ref[...]
    ri = lax.broadcasted_iota(jnp.int32, (C, C), 0)
    ci = lax.broadcasted_iota(jnp.int32, (C, C), 1)
    tri = (ri >= ci).astype(BF16)
    si = lax.broadcasted_iota(jnp.int32, (SB, SB), 0)
    sj = lax.broadcasted_iota(jnp.int32, (SB, SB), 1)
    sub_causal = si >= sj

    lr_hi, lr_lo = _split2(lr_ref[0])
    wa_hi, wa_lo = _split2(wa_ref[...])
    z = (jnp.dot(lr_hi, wa_hi, preferred_element_type=F32) + jnp.dot(lr_hi, wa_lo, preferred_element_type=F32)
         + jnp.dot(lr_lo, wa_hi, preferred_element_type=F32)) + ba_ref[...]
    la = -(jnp.maximum(-z, 0.0) + jnp.log(1.0 + jnp.exp(-jnp.abs(z)))) * (LOG2E / GLA_TAU)
    la1 = la.astype(BF16)
    la2, la3 = _split2(la - la1.astype(F32))
    for c in range(n_chunks):
        r = slice(c * C, (c + 1) * C)
        g_ref[r, :] = (jnp.dot(tri, la1[r], preferred_element_type=F32) + jnp.dot(tri, la2[r], preferred_element_type=F32)
                       + jnp.dot(tri, la3[r], preferred_element_type=F32))

    kf_ref[...] = k_ref[0].astype(F32)

    def chunk(c):
        rows = pl.ds(c * C, C)
        for h in range(H):
            kl = slice(h * DK, (h + 1) * DK)
            vl = slice(h * DV, (h + 1) * DV)
            q = q_ref[0, rows, kl].astype(F32) * (DK ** -0.5)
            k = kf_ref[rows, kl]
            vb = v_ref[0, rows, vl]
            g = g_ref[rows, kl]

            outs = []
            for i in range(C // SB):
                lo = i * SB
                gi = g[lo:lo + SB]
                qi = q[lo:lo + SB]
                ki = k[lo:lo + SB]
                HB = SB // 2
                a_top = jnp.zeros((HB, SB), F32)
                a_bot = jnp.zeros((HB, SB), F32)
                for jj in range(SB):
                    row = slice(lo + jj, lo + jj + 1)
                    k_row = kf_ref.at[rows, kl][row, :]
                    g_row = g_ref.at[rows, kl][row, :]
                    qs, gs = (qi, gi) if jj < HB else (qi[HB:], gi[HB:])
                    w = (qs * k_row) * jnp.exp2(jnp.minimum(gs - g_row, 0.0))
                    col = jnp.sum(w, axis=-1, keepdims=True)
                    if jj < HB:
                        a_top = jnp.where(sj[:HB] == jj, col[:HB], a_top)
                        a_bot = jnp.where(sj[:HB] == jj, col[HB:], a_bot)
                    else:
                        a_bot = jnp.where(sj[:HB] == jj, col, a_bot)
                a_d = jnp.where(sub_causal, jnp.concatenate([a_top, a_bot], axis=0), 0.0)
                o_i = jnp.dot(a_d.astype(BF16), vb[lo:lo + SB], preferred_element_type=F32)
                if i > 0:
                    r = g[lo:lo + 1]
                    qt = qi * jnp.exp2(gi - r)
                    kt = k[:lo] * jnp.exp2(r - g[:lo])
                    a_o = lax.dot_general(qt.astype(BF16), kt.astype(BF16), NT, preferred_element_type=F32)
                    o_i = o_i + jnp.dot(a_o.astype(BF16), vb[:lo], preferred_element_type=F32)
                outs.append(o_i)
            o = jnp.concatenate(outs, axis=0)

            st = st_ref[h]
            qg = q * jnp.exp2(g)
            o = o + lax.dot_general(qg.astype(BF16), st.astype(BF16), NT, preferred_element_type=F32)
            gl = g[C - 1:C]
            kd = k * jnp.exp2(gl - g)
            st_ref[h] = st * jnp.exp2(gl) + lax.dot_general(vb, kd.astype(BF16), TN, preferred_element_type=F32)

            ms = jnp.mean(o * o, axis=-1, keepdims=True)
            y = o * lax.rsqrt(ms + EPS) * gn
            o_ref[0, rows, vl] = (y * _silu(gate_ref[0, rows, vl].astype(F32))).astype(o_ref.dtype)

    for c in range(n_chunks):
        chunk(c)


def _gla(proj3, small3, wa, ba, gn, tr=512):
    B, S, _ = proj3.shape
    H, DK, DV = GLA_HEADS, GLA_DK, GLA_DV
    kern = functools.partial(_gla_kernel, n_chunks=tr // GLA_CHUNK)
    return pl.pallas_call(
        kern,
        grid=(B, S // tr),
        in_specs=[
            pl.BlockSpec((1, tr, H * DK), lambda b, r: (b, r, C_GQ // (H * DK))),
            pl.BlockSpec((1, tr, H * DK), lambda b, r: (b, r, C_GK // (H * DK))),
            pl.BlockSpec((1, tr, H * DV), lambda b, r: (b, r, C_GV // (H * DV))),
            pl.BlockSpec((1, tr, H * DV), lambda b, r: (b, r, C_GG // (H * DV))),
            pl.BlockSpec((1, tr, 128), lambda b, r: (b, r, CS_LR // 128)),
            pl.BlockSpec((128, H * DK), lambda b, r: (0, 0)),
            pl.BlockSpec((1, H * DK), lambda b, r: (0, 0)),
            pl.BlockSpec((1, DV), lambda b, r: (0, 0)),
        ],
        out_specs=pl.BlockSpec((1, tr, H * DV), lambda b, r: (b, r, 0)),
        out_shape=jax.ShapeDtypeStruct((B, S, H * DV), BF16),
        scratch_shapes=[pltpu.VMEM((H, DV, DK), F32), pltpu.VMEM((tr, H * DK), F32), pltpu.VMEM((tr, H * DK), F32)],
        compiler_params=_cp(("parallel", "arbitrary")),
        name="gla",
    )(proj3, proj3, proj3, proj3, small3, wa, ba, gn)


def _group_meansq(x, gmat):
    return jnp.dot((x * x).astype(BF16), gmat, preferred_element_type=F32) * (1.0 / NSA_HD)


def _rope(x, cos, sin_signed, first_half):
    w = x.shape[-1]
    rot = jnp.where(first_half, pltpu.roll(x, w - NSA_HD // 2, 1), pltpu.roll(x, NSA_HD // 2, 1))
    return x * cos + rot * sin_signed


def _nsa_prep_kernel(q_ref, kvc_ref, kvs_ref, kvw_ref, pos_ref, inv_ref, qn_ref, kn_ref, gm_ref,
                     qr_ref, kc_ref, vc_ref, ks_ref, vst_ref, vwt_ref, tok_ref):
    tr = q_ref.shape[1]
    W = NSA_KV_HEADS * NSA_HD
    HD = NSA_HD
    gmat = gm_ref[...]
    ang = pos_ref[0].astype(F32) * inv_ref[...]
    cos1, sin1 = jnp.cos(ang), jnp.sin(ang)
    cos = jnp.concatenate([cos1, cos1], axis=1)
    lane = lax.broadcasted_iota(jnp.int32, (tr, W), 1)
    first_half = (lane % HD) < (HD // 2)
    sin = jnp.concatenate([sin1, sin1], axis=1)
    sin = jnp.where(first_half, -sin, sin)

    def norm_rope(x, gain):
        y = x * lax.rsqrt(_group_meansq(x, gmat) + EPS) * gain
        return _rope(y, cos, sin, first_half)

    qn = qn_ref[...]
    for s in range(NSA_HEADS * HD // W):
        xq = q_ref[0, :, s * W:(s + 1) * W].astype(F32)
        qr_ref[0, :, s * W:(s + 1) * W] = (norm_rope(xq, qn) * (HD ** -0.5 * LOG2E)).astype(qr_ref.dtype)

    toks = (norm_rope(kvc_ref[0, :, :W].astype(F32), kn_ref[0:1, :]), kvc_ref[0, :, W:].astype(F32))
    for j, dst in enumerate((kc_ref, vc_ref)):
        for h in range(NSA_KV_HEADS):
            tok_ref[j, h] = toks[j][:, h * HD:(h + 1) * HD]
            cols = [tok_ref[j, h, pl.ds(l, tr // CMP_STRIDE, stride=CMP_STRIDE), :] for l in range(CMP_STRIDE)]
            dst[0, h] = jnp.concatenate(cols, axis=1).astype(dst.dtype)

    ks = norm_rope(kvs_ref[0, :, :W].astype(F32), kn_ref[1:2, :]).astype(BF16)
    kw = norm_rope(kvw_ref[0, :, :W].astype(F32), kn_ref[2:3, :]).astype(BF16)
    row_blk = (pl.program_id(1) * tr + lax.broadcasted_iota(jnp.int32, (tr, SLC_LANES), 0)) // SLC_BLOCK
    onehot = (row_blk == lax.broadcasted_iota(jnp.int32, (tr, SLC_LANES), 1)).astype(BF16)
    vst = kvs_ref[0, :, W:].astype(F32).T
    for h in range(NSA_KV_HEADS):
        ks_ref[0, h] = jnp.concatenate([onehot, ks[:, h * HD:(h + 1) * HD], kw[:, h * HD:(h + 1) * HD]], axis=1)
        for j in range(tr // SLC_CHUNK):
            vst_ref[0, h, j, :HD] = vst[h * HD:(h + 1) * HD, j * SLC_CHUNK:(j + 1) * SLC_CHUNK].astype(BF16)
            vst_ref[0, h, j, HD:] = jnp.ones((VT_ROWS - HD, SLC_CHUNK), BF16)

    vwt = kvw_ref[0, :, W:].astype(F32).T
    for h in range(NSA_KV_HEADS):
        for j in range(tr // WIN_ALIGN):
            vwt_ref[0, h, j, :HD] = vwt[h * HD:(h + 1) * HD, j * WIN_ALIGN:(j + 1) * WIN_ALIGN].astype(BF16)
            vwt_ref[0, h, j, HD:] = jnp.ones((VT_ROWS - HD, WIN_ALIGN), BF16)


def _nsa_prep(proj3, pos3, inv, qn, kn, gmat, tr=512):
    B, S, _ = proj3.shape
    Hk, HD = NSA_KV_HEADS, NSA_HD
    hm = lambda w, dt: jax.ShapeDtypeStruct((B, Hk, S, w), dt)
    hspec = lambda w: pl.BlockSpec((1, Hk, tr, w), lambda b, r: (b, 0, r, 0))
    ctok = jax.ShapeDtypeStruct((B, Hk, S // CMP_STRIDE, CMP_STRIDE * HD), BF16)
    cspec = pl.BlockSpec((1, Hk, tr // CMP_STRIDE, CMP_STRIDE * HD), lambda b, r: (b, 0, r, 0))
    tspec = lambda c, rows=HD: pl.BlockSpec((1, Hk, tr // c, rows, c), lambda b, r: (b, 0, r, 0, 0))
    return pl.pallas_call(
        _nsa_prep_kernel,
        grid=(B, S // tr),
        in_specs=[
            pl.BlockSpec((1, tr, 1024), lambda b, r: (b, r, C_NQ // 1024)),
            pl.BlockSpec((1, tr, 512), lambda b, r: (b, r, C_NKV // 512)),
            pl.BlockSpec((1, tr, 512), lambda b, r: (b, r, C_NKV // 512 + 1)),
            pl.BlockSpec((1, tr, 512), lambda b, r: (b, r, C_NKV // 512 + 2)),
            pl.BlockSpec((1, tr, 1), lambda b, r: (b, r, 0)),
            pl.BlockSpec((1, 128), lambda b, r: (0, 0)),
            pl.BlockSpec((1, 256), lambda b, r: (0, 0)),
            pl.BlockSpec((3, 256), lambda b, r: (0, 0)),
            pl.BlockSpec((256, 256), lambda b, r: (0, 0)),
        ],
        out_specs=[pl.BlockSpec((1, tr, 1024), lambda b, r: (b, r, 0)), cspec, cspec, hspec(KAUG),
                   tspec(SLC_CHUNK, VT_ROWS), tspec(WIN_ALIGN, VT_ROWS)],
        out_shape=[jax.ShapeDtypeStruct((B, S, 1024), BF16), ctok, ctok, hm(KAUG, BF16),
                   jax.ShapeDtypeStruct((B, Hk, S // SLC_CHUNK, VT_ROWS, SLC_CHUNK), BF16),
                   jax.ShapeDtypeStruct((B, Hk, S // WIN_ALIGN, VT_ROWS, WIN_ALIGN), BF16)],
        scratch_shapes=[pltpu.VMEM((2, Hk, tr, HD), F32)],
        compiler_params=_cp(("parallel", "parallel")),
        name="nsa_prep",
    )(proj3, proj3, proj3, proj3, pos3, inv, qn, kn, gmat)


def _compress_kernel(t_ref, pe_ref, w1a_ref, w1b_ref, w2_ref, *rest, transposed):
    tail_ref, o_ref = rest if transposed else (None,) + rest
    t = t_ref[0, 0]
    w1a, w1b = w1a_ref[...], w1b_ref[...]
    u = jnp.dot(t, w1a, preferred_element_type=F32)
    v = jnp.dot(t, w1b, preferred_element_type=F32)
    pe = pe_ref[...].astype(BF16)
    c = (jnp.dot(pe[0:8], w1a, preferred_element_type=F32) + jnp.dot(pe[8:16], w1b, preferred_element_type=F32))[0:1]
    n = v.shape[0]
    h = u + pltpu.roll(v, n - 1, 0) + c
    h = jax.nn.gelu(h).astype(BF16)
    if transposed:
        hd = w2_ref.shape[0]
        o_ref[0, 0, :hd] = lax.dot_general(w2_ref[...], h, NT, preferred_element_type=F32).astype(o_ref.dtype)
        o_ref[0, 0, hd:] = tail_ref[...]
    else:
        o_ref[0, 0] = jnp.dot(h, w2_ref[...], preferred_element_type=F32).astype(o_ref.dtype)


def _compress(tok, pe2, w1a, w1b, w2, tail=None):
    B, Hk, n, _ = tok.shape
    HD = NSA_HD
    t2 = tok
    transposed = tail is not None
    oshape = (HD + tail.shape[0], n) if transposed else (n, HD)
    w2 = w2.T if transposed else w2
    extra = ([tail], [pl.BlockSpec(tail.shape, lambda b, h: (0, 0))]) if transposed else ([], [])
    return pl.pallas_call(
        functools.partial(_compress_kernel, transposed=transposed),
        grid=(B, Hk),
        in_specs=[
            pl.BlockSpec((1, 1, n, CMP_STRIDE * HD), lambda b, h: (b, h, 0, 0)),
            pl.BlockSpec((16, CMP_STRIDE * HD), lambda b, h: (0, 0)),
            pl.BlockSpec((CMP_STRIDE * HD, CMP_HIDDEN), lambda b, h: (0, 0)),
            pl.BlockSpec((CMP_STRIDE * HD, CMP_HIDDEN), lambda b, h: (0, 0)),
            pl.BlockSpec(w2.shape, lambda b, h: (0, 0)),
        ] + extra[1],
        out_specs=pl.BlockSpec((1, 1) + oshape, lambda b, h: (b, h, 0, 0)),
        out_shape=jax.ShapeDtypeStruct((B, Hk) + oshape, BF16),
        compiler_params=_cp(("parallel", "parallel")),
        name="compress_v" if transposed else "compress_k",
    )(t2, pe2, w1a, w1b, w2, *extra[0])


def _nsa_attn_kernel(q_ref, kc_ref, vct_ref, ks_ref, vst_ref, vwt_ref, gt_ref, gate_ref, tri_ref, cmask_ref, wmask_ref,
                     o_ref,
                     s_ref, e_ref):
    G, HD, QB, NH, NB = NSA_GROUP, NSA_HD, SLC_BLOCK, NSA_STEP_HEADS, NSA_STEP_BLOCKS
    R = NB * G * QB
    heads = range(NH)
    hk0 = pl.program_id(1) * NH
    step = pl.program_id(2)
    t0 = step * (NB * QB)
    q = [jnp.concatenate([q_ref[0, a * QB:(a + 1) * QB, (h * G + g) * HD:(h * G + g + 1) * HD]
                          for a in range(NB) for g in range(G)], axis=0)
         for h in heads]

    nc = kc_ref.shape[2]
    cmask = cmask_ref[pl.ds(pl.multiple_of(nc - t0 // CMP_STRIDE, CMP_STRIDE), nc), :]
    for h in heads:
        sm = lax.dot_general(kc_ref[0, h], q[h], NT, preferred_element_type=F32) + cmask
        m = jnp.maximum(jnp.max(sm, axis=0, keepdims=True), 0.1 * NEG)
        e_ref[h] = jnp.exp2(sm - m).astype(BF16)
    QL = max(NB * QB, 2 * QB)
    blk = lax.broadcasted_iota(jnp.int32, (SLC_LANES, QL), 0)
    qlane = lax.broadcasted_iota(jnp.int32, (1, QL), 1)
    cur = step * NB + (qlane // QB) % NB
    forced = (blk == 0) | (blk == cur) | (blk == cur - 1)
    o_cmp, score = [], []
    for h in heads:
        oc = jnp.dot(vct_ref[0, h], e_ref[h], preferred_element_type=F32)
        den = oc[HD:HD + 1]
        rden = 1.0 / jnp.where(den > 0.0, den, 1.0)
        o_cmp.append(oc[:HD] * rden)
        impf = oc[VT_ROWS:] * rden
        parts = []
        for a in range(NB):
            pa = impf[:, a * G * QB:(a + 1) * G * QB]
            p2 = pa[:, :2 * QB] + pa[:, 2 * QB:]
            parts.append(p2 + pltpu.roll(p2, QB, 1))
        first_half = qlane[:, :2 * QB] < QB
        imp = parts[0] if NB == 1 else jnp.concatenate(
            [jnp.where(first_half, parts[a], parts[a + 1]) for a in range(0, NB, 2)], axis=1)
        score.append(jnp.where(forced, -jnp.inf, jnp.where(blk <= cur, imp, NEG)))

    wc = jnp.maximum(t0 - WINDOW, 0) // WIN_ALIGN
    w0 = pl.multiple_of(wc * WIN_ALIGN, WIN_ALIGN)
    wmask = wmask_ref[pl.ds(pl.multiple_of(jnp.maximum(WINDOW - t0, 0), NB * QB), WIN_KEYS), :]
    o_win = []
    for h in heads:
        q_win = jnp.concatenate([jnp.zeros((R, SLC_LANES + HD), BF16), q[h]], axis=1)
        sw = lax.dot_general(ks_ref[0, h, pl.ds(w0, WIN_KEYS), :], q_win, NT, preferred_element_type=F32)
        sw = sw + wmask
        pw = jnp.exp2(sw - jnp.max(sw, axis=0, keepdims=True))
        vv = jnp.concatenate([vwt_ref[0, h, wc + j] for j in range(WIN_KEYS // WIN_ALIGN)], axis=1)
        ow = jnp.dot(vv, pw.astype(BF16), preferred_element_type=F32)
        o_win.append(ow[:HD] * (1.0 / ow[HD:HD + 1]))

    few = cur < SLC_TOPN
    causal = blk <= cur
    blkf = blk.astype(F32)

    def pick(sc):
        mx = jnp.max(sc, axis=0, keepdims=True)
        first = jnp.min(jnp.where(sc == mx, blkf, float(SLC_LANES)), axis=0, keepdims=True)
        return jnp.where(blkf == first, -jnp.inf, sc)

    for _ in range(SLC_TOPN - 3):
        score = [pick(sc) for sc in score]
    q_aug = []
    for h in heads:
        chosen = forced | ((few | (score[h] == -jnp.inf)) & causal)
        bias_t = jnp.where(chosen, 0.0, NEG)
        bias = bias_t.T.astype(BF16)
        rows = jnp.concatenate([bias[a * QB:(a + 1) * QB] for a in range(NB) for g in range(G)], axis=0)
        q_aug.append(jnp.concatenate([rows, q[h], jnp.zeros((R, HD), BF16)], axis=1))

    def scores(c, slot):
        k0 = pl.multiple_of(c * SLC_CHUNK, SLC_CHUNK)
        for h in heads:
            s_ref[slot, h] = lax.dot_general(ks_ref[0, h, pl.ds(k0, SLC_CHUNK), :], q_aug[h], NT, preferred_element_type=F32)

    def absorb(c, slot, carry):
        out = []
        for h in heads:
            m_i, acc = carry[h]
            sc = s_ref[slot, h]
            m_new = jnp.maximum(m_i, jnp.max(sc, axis=0, keepdims=True))
            pp = jnp.exp2(sc - m_new).astype(BF16)
            acc = jnp.exp2(m_i - m_new) * acc + jnp.dot(vst_ref[0, h, c], pp, preferred_element_type=F32)
            out.append((m_new, acc))
        return tuple(out)

    diag = t0 // SLC_CHUNK
    n_pairs = diag // 2
    scores(0, 0)

    def pair(j, carry):
        scores(2 * j + 1, 1)
        carry = absorb(2 * j, 0, carry)
        scores(2 * j + 2, 0)
        return absorb(2 * j + 1, 1, carry)

    init = tuple((jnp.full((1, R), NEG, F32), jnp.zeros((VT_ROWS, R), F32)) for h in heads)
    carry = lax.fori_loop(0, n_pairs, pair, init)
    scores(2 * n_pairs + 1, 1)
    r0 = pl.multiple_of(t0 - diag * SLC_CHUNK, NB * QB)
    tri = tri_ref[...]
    for h in heads:
        s_ref[diag % 2, h, pl.ds(r0, NB * QB), :] += tri
    carry = absorb(2 * n_pairs, 0, carry)
    carry = absorb(2 * n_pairs + 1, 1, carry)

    sig = _sigmoid(gt_ref[0])
    glane = lax.broadcasted_iota(jnp.int32, (QB, 128), 1)
    outs = [[] for a in range(NB)]
    for h in heads:
        acc_s = carry[h][1]
        oc, os_, ow = o_cmp[h].T, (acc_s[:HD] * (1.0 / acc_s[HD:HD + 1])).T, o_win[h].T
        for a in range(NB):
            sig_a = sig[a * QB:(a + 1) * QB]
            for g in range(G):
                base = ((hk0 + h) * G + g) * N_NSA_BRANCH
                gc, gs, gw = [jnp.sum(jnp.where(glane == base + b, sig_a, 0.0), axis=-1, keepdims=True) for b in range(3)]
                r = slice((a * G + g) * QB, (a * G + g + 1) * QB)
                outs[a].append(gc * oc[r] + gs * os_[r] + gw * ow[r])
    o = jnp.concatenate([jnp.concatenate(oa, axis=1) for oa in outs], axis=0)
    o_ref[0] = (o * _silu(gate_ref[0].astype(F32))).astype(o_ref.dtype)


def _nsa_attn(qr, kc, vct, ks, vst, vwt, proj3, small3):
    B, S, _ = qr.shape
    Hk, HD, NH = NSA_KV_HEADS, NSA_HD, NSA_STEP_HEADS
    TQ = NSA_STEP_BLOCKS * SLC_BLOCK
    r = np.arange(TQ)[:, None]
    ln = np.arange(NSA_GROUP * TQ)[None, :]
    a, qi = ln // (NSA_GROUP * SLC_BLOCK), ln % SLC_BLOCK
    tri = jnp.asarray(np.where((r // SLC_BLOCK != a) | (r % SLC_BLOCK <= qi), 0.0, NEG), F32)
    nc = kc.shape[2]
    tl = a * SLC_BLOCK + qi
    u = np.arange(2 * nc)[:, None]
    cmask = jnp.asarray(np.where(u - nc <= (tl - (CMP_LEN - 1)) // CMP_STRIDE, 0.0, NEG), F32)
    u = np.arange(WINDOW + WIN_KEYS)[:, None]
    wmask = jnp.asarray(np.where((u > tl) & (u <= WINDOW + tl), 0.0, NEG), F32)
    const = lambda c: pl.BlockSpec(c.shape, lambda b, h, t: (0, 0), pipeline_mode=pl.Buffered(1))
    W = NH * NSA_GROUP * HD
    full = lambda a: pl.BlockSpec((1, NH) + a.shape[2:], lambda b, h, t: (b, h) + (0,) * (a.ndim - 2),
                                  pipeline_mode=pl.Buffered(1))
    return pl.pallas_call(
        _nsa_attn_kernel,
        grid=(B, Hk // NH, S // TQ),
        in_specs=[
            pl.BlockSpec((1, TQ, W), lambda b, h, t: (b, t, h)),
            full(kc), full(vct), full(ks), full(vst), full(vwt),
            pl.BlockSpec((1, TQ, 128), lambda b, h, t: (b, t, CS_NSG // 128)),
            pl.BlockSpec((1, TQ, W), lambda b, h, t: (b, t, C_NGATE // W + h)),
            const(tri), const(cmask), const(wmask),
        ],
        out_specs=pl.BlockSpec((1, TQ, W), lambda b, h, t: (b, t, h)),
        out_shape=jax.ShapeDtypeStruct((B, S, NSA_HEADS * HD), BF16),
        scratch_shapes=[pltpu.VMEM((2, NH, SLC_CHUNK, NSA_GROUP * TQ), F32), pltpu.VMEM((NH, nc, NSA_GROUP * TQ), BF16)],
        compiler_params=_cp(("parallel", "parallel", "arbitrary")),
        name="nsa_attn",
    )(qr, kc, vct, ks, vst, vwt, small3, proj3, tri, cmask, wmask)


def _head_rms(x, gain, hd):
    outs = []
    for h in range(x.shape[-1] // hd):
        xh = x[:, h * hd:(h + 1) * hd]
        ms = jnp.mean(xh * xh, axis=-1, keepdims=True)
        outs.append(xh * lax.rsqrt(ms + EPS) * gain)
    return jnp.concatenate(outs, axis=1)


def _mem_prep_kernel(mem_ref, g_ref, w_ref, kn_ref, mk_ref, mv_ref):
    x = mem_ref[0]
    ms = jnp.mean(x * x, axis=-1, keepdims=True)
    xn = (x * lax.rsqrt(ms + EPS) * g_ref[...]).astype(BF16)
    kv = jnp.dot(xn, w_ref[...], preferred_element_type=F32)
    W = MEM_HEADS * MEM_HD
    mk_ref[0] = _head_rms(kv[:, :W], kn_ref[...], MEM_HD).astype(mk_ref.dtype)
    mv_ref[0] = kv[:, W:].astype(mv_ref.dtype)


def _mem_prep(mem, gain, w_kv, kn):
    B, N, D = mem.shape
    W = MEM_HEADS * MEM_HD
    return pl.pallas_call(
        _mem_prep_kernel,
        grid=(B,),
        in_specs=[
            pl.BlockSpec((1, N, D), lambda b: (b, 0, 0)),
            pl.BlockSpec((1, D), lambda b: (0, 0)),
            pl.BlockSpec((D, 2 * W), lambda b: (0, 0)),
            pl.BlockSpec((1, MEM_HD), lambda b: (0, 0)),
        ],
        out_specs=[pl.BlockSpec((1, N, W), lambda b: (b, 0, 0))] * 2,
        out_shape=[jax.ShapeDtypeStruct((B, N, W), BF16)] * 2,
        compiler_params=_cp(("parallel",)),
        name="mem_prep",
    )(mem, gain, w_kv, kn)


def _final_kernel(x_ref, ya_ref, yb_ref, mq_ref, mg_ref, mr_ref, mk_ref, mv_ref, qn_ref, wb_ref, wo_ref, o_ref):
    mq = _head_rms(mq_ref[...].astype(F32), qn_ref[...], MEM_HD)
    mk = mk_ref[0]
    mv = mv_ref[0]
    heads = []
    for h in range(MEM_HEADS):
        sl = slice(h * MEM_HD, (h + 1) * MEM_HD)
        s = lax.dot_general(mq[:, sl].astype(BF16), mk[:, sl], NT, preferred_element_type=F32) * (MEM_HD ** -0.5)
        m = jnp.max(s, axis=-1, keepdims=True)
        p = jnp.exp(s - m)
        o = jnp.dot(p.astype(BF16), mv[:, sl], preferred_element_type=F32) / jnp.sum(p, axis=-1, keepdims=True)
        heads.append(o)
    ym = jnp.concatenate(heads, axis=1) * _silu(mg_ref[...].astype(F32))

    mixed = None
    for c, y in enumerate((ya_ref[...], yb_ref[...], ym.astype(BF16))):
        z = jnp.dot(y, wb_ref[c], preferred_element_type=F32)
        term = _sigmoid(mr_ref[:, c * D_MODEL:(c + 1) * D_MODEL].astype(F32)) * z
        mixed = term if mixed is None else mixed + term
    o_ref[...] = x_ref[...] + jnp.dot(mixed.astype(BF16), wo_ref[...], preferred_element_type=F32)


def _final(x2, ya2, yb2, proj2, mk, mv, qn, wb, wo, S, tr=512):
    M, D = x2.shape
    N = mk.shape[1]
    nb = S // tr
    row = lambda c: pl.BlockSpec((tr, D), lambda i: (i, c))
    return pl.pallas_call(
        _final_kernel,
        grid=(M // tr,),
        in_specs=[
            row(0), row(0), row(0),
            row(C_MQ // D), row(C_MG // D),
            pl.BlockSpec((tr, 3 * D), lambda i: (i, C_MERGE // (3 * D))),
            pl.BlockSpec((1, N, D), lambda i: (i // nb, 0, 0)),
            pl.BlockSpec((1, N, D), lambda i: (i // nb, 0, 0)),
            pl.BlockSpec((1, MEM_HD), lambda i: (0, 0)),
            pl.BlockSpec((3, D, D), lambda i: (0, 0, 0), pipeline_mode=pl.Buffered(1)),
            pl.BlockSpec((D, D), lambda i: (0, 0), pipeline_mode=pl.Buffered(1)),
        ],
        out_specs=pl.BlockSpec((tr, D), lambda i: (i, 0)),
        out_shape=jax.ShapeDtypeStruct((M, D), F32),
        compiler_params=_cp(("parallel",)),
        name="final",
    )(x2, ya2, yb2, proj2, proj2, proj2, mk, mv, qn, wb, wo)


def _overlap_matrix_t(S):
    n_cmp = (S - CMP_LEN) // CMP_STRIDE + 1
    n_slc = S // SLC_BLOCK
    cs = np.arange(n_cmp)[:, None] * CMP_STRIDE
    ss = np.arange(n_slc)[None, :] * SLC_BLOCK
    ov = np.clip(np.minimum(cs + CMP_LEN, ss + SLC_BLOCK) - np.maximum(cs, ss), 0, None) / CMP_LEN
    out = np.zeros((SLC_LANES, S // CMP_STRIDE), np.float32)
    out[:n_slc, :n_cmp] = ov.T
    return jnp.asarray(out, BF16)


def _pad_cols(w, n):
    return jnp.pad(w, ((0, 0), (0, n - w.shape[1])))


def _layer(x, mem, positions, norm_gain, mem_norm_gain, w_in, w_gla_alpha, b_gla_alpha, gla_out_norm,
           nsa_q_norm, nsa_k_norm, pe_cmp_k, pe_cmp_v, w_cmp_k1, w_cmp_k2, w_cmp_v1, w_cmp_v2,
           w_mem_kv, mem_q_norm, mem_k_norm, w_branch, w_out):
    B, S, D = x.shape
    assert D == D_MODEL and S % (2 * SLC_CHUNK) == 0 and S >= WIN_KEYS and S // SLC_BLOCK <= SLC_LANES
    assert SLC_TOPN % NSA_STEP_BLOCKS == 0 and SLC_CHUNK % (NSA_STEP_BLOCKS * SLC_BLOCK) == 0

    o = np.cumsum([0, 512, 512, 1024, 16, 1024, 1024, 1536, 48, 1024, 1024, 1024, 3072])
    runs = [(2, 3), (4, 6), (8, 12), (6, 7), (0, 2)]
    wide = [w_in[:, o[a]:o[b]].astype(BF16) for a, b in runs]
    w_all = jnp.concatenate(wide + [jnp.zeros((D, NP - sum(w.shape[1] for w in wide)), BF16)], axis=1)
    glr, nsg = w_in[:, o[3]:o[4]], w_in[:, o[7]:o[8]]
    w_small = jnp.concatenate([_pad_cols(glr, 128), _pad_cols(nsg, 128)], axis=1).astype(BF16)

    x2 = x.reshape(B * S, D)
    proj2, small2 = _proj(x2, norm_gain.reshape(1, D), w_all, w_small)
    proj3 = proj2.reshape(B, S, NP)
    small3 = small2.reshape(B, S, NP_SMALL)

    wa = jnp.pad(w_gla_alpha, ((0, 128 - GLA_RANK), (0, 0)))
    ya = _gla(proj3, small3, wa, b_gla_alpha.reshape(1, -1), gla_out_norm.reshape(1, -1))

    half = NSA_HD // 2
    inv = ROPE_THETA ** (-jnp.arange(half, dtype=F32) / half)
    inv = jnp.tile(inv, 128 // half).reshape(1, 128)
    qn = jnp.tile(nsa_q_norm, NSA_KV_HEADS).reshape(1, -1)
    kn = jnp.tile(nsa_k_norm, (1, NSA_KV_HEADS))
    gid = np.arange(256) // NSA_HD
    gmat = jnp.asarray(gid[:, None] == gid[None, :], BF16)
    qr, kc_tok, vc_tok, ks, vst, vwt = _nsa_prep(proj3, positions.reshape(B, S, 1), inv, qn, kn, gmat)

    def cmp_args(pe, w1, w2):
        pe2 = pe.reshape(2, CMP_STRIDE * NSA_HD)
        pe16 = jnp.concatenate([jnp.broadcast_to(pe2[0:1], (8, pe2.shape[1])), jnp.broadcast_to(pe2[1:2], (8, pe2.shape[1]))], 0)
        w1f = w1.reshape(CMP_LEN * NSA_HD, CMP_HIDDEN).astype(BF16)
        return pe16, w1f[:CMP_STRIDE * NSA_HD], w1f[CMP_STRIDE * NSA_HD:], w2.astype(BF16)

    kc = _compress(kc_tok, *cmp_args(pe_cmp_k, w_cmp_k1, w_cmp_k2))
    tail = jnp.concatenate([jnp.ones((VT_ROWS - NSA_HD, S // CMP_STRIDE), BF16), _overlap_matrix_t(S)], axis=0)
    vct = _compress(vc_tok, *cmp_args(pe_cmp_v, w_cmp_v1, w_cmp_v2), tail=tail)
    yb = _nsa_attn(qr, kc, vct, ks, vst, vwt, proj3, small3)

    mk, mv = _mem_prep(mem, mem_norm_gain.reshape(1, D), w_mem_kv.astype(BF16), mem_k_norm.reshape(1, -1))
    out = _final(x2, ya.reshape(B * S, D), yb.reshape(B * S, D), proj2, mk, mv, mem_q_norm.reshape(1, -1),
                 w_branch.astype(BF16), w_out.astype(BF16), S)
    return out.reshape(B, S, D)


def kernel(x, mem, positions, norm_gain, mem_norm_gain, w_in, w_gla_alpha, b_gla_alpha, gla_out_norm, nsa_q_norm, nsa_k_norm, pe_cmp_k, pe_cmp_v, w_cmp_k1, w_cmp_k2, w_cmp_v1, w_cmp_v2, w_mem_kv, mem_q_norm, mem_k_norm, w_branch, w_out):
    h = x
    for l in range(norm_gain.shape[0]):
        h = _layer(h, mem, positions, norm_gain[l], mem_norm_gain[l], w_in[l], w_gla_alpha[l], b_gla_alpha[l],
                   gla_out_norm[l], nsa_q_norm[l], nsa_k_norm[l], pe_cmp_k[l], pe_cmp_v[l], w_cmp_k1[l],
                   w_cmp_k2[l], w_cmp_v1[l], w_cmp_v2[l], w_mem_kv[l], mem_q_norm[l], mem_k_norm[l],
                   w_branch[l], w_out[l])
    return h
```

```python
import functools

import numpy as np
import jax
import jax.numpy as jnp
from jax import lax
from jax.experimental import pallas as pl
from jax.experimental.pallas import tpu as pltpu

F32 = jnp.float32
BF16 = jnp.bfloat16

D_MODEL = 1024
ROPE_THETA = 10000.0
EPS = 1e-6
NEG = -1e30

GLA_HEADS = 4
GLA_DK = 128
GLA_DV = 256
GLA_RANK = 16
GLA_TAU = 16.0
GLA_CHUNK = 64
GLA_SUB = 16

NSA_HEADS = 16
NSA_KV_HEADS = 4
NSA_GROUP = 4
NSA_HD = 64
CMP_LEN = 32
CMP_STRIDE = 16
CMP_HIDDEN = 256
SLC_BLOCK = 64
SLC_TOPN = 16
WINDOW = 512
N_NSA_BRANCH = 3
SLC_LANES = 128
SLC_CHUNK = 512
VT_ROWS = NSA_HD + 16
LOG2E = 1.4426950408889634
WIN_ALIGN = 128
NSA_STEP_HEADS = 2
NSA_STEP_BLOCKS = 4
WIN_KEYS = WINDOW + max(NSA_STEP_BLOCKS * SLC_BLOCK, WIN_ALIGN)
KAUG = SLC_LANES + 2 * NSA_HD

MEM_HEADS = 4
MEM_HD = 256

C_GV, C_GG, C_NQ, C_NGATE, C_MQ, C_MG, C_MERGE = 0, 1024, 2048, 3072, 4096, 5120, 6144
C_NKV, C_GQ, C_GK = 9216, 10752, 11264
NP = 12288
CS_LR, CS_NSG, NP_SMALL = 0, 128, 256

VMEM_LIMIT = 48 * 1024 * 1024

NT = (((1,), (1,)), ((), ()))
TN = (((0,), (0,)), ((), ()))


def _cp(sem):
    return pltpu.CompilerParams(dimension_semantics=sem, vmem_limit_bytes=VMEM_LIMIT)


def _silu(x):
    return x * (1.0 / (1.0 + jnp.exp(-x)))


def _sigmoid(x):
    return 1.0 / (1.0 + jnp.exp(-x))


def _proj_kernel(x_ref, g_ref, w_ref, ws_ref, o_ref, os_ref, xn_ref):
    @pl.when(pl.program_id(1) == 0)
    def _():
        x = x_ref[...]
        ms = jnp.mean(x * x, axis=-1, keepdims=True)
        xn = (x * lax.rsqrt(ms + EPS) * g_ref[...]).astype(BF16)
        xn_ref[...] = xn
        os_ref[...] = jnp.dot(xn, ws_ref[...], preferred_element_type=F32)

    o_ref[...] = jnp.dot(xn_ref[...], w_ref[...], preferred_element_type=F32).astype(o_ref.dtype)


def _proj(x2, gain, w_all, w_small, tm=1024, tn=2048):
    M = x2.shape[0]
    ns = w_small.shape[1]
    return pl.pallas_call(
        _proj_kernel,
        grid=(M // tm, NP // tn),
        in_specs=[
            pl.BlockSpec((tm, D_MODEL), lambda i, j: (i, 0)),
            pl.BlockSpec((1, D_MODEL), lambda i, j: (0, 0)),
            pl.BlockSpec((D_MODEL, tn), lambda i, j: (0, j)),
            pl.BlockSpec((D_MODEL, ns), lambda i, j: (0, 0)),
        ],
        out_specs=[pl.BlockSpec((tm, tn), lambda i, j: (i, j)), pl.BlockSpec((tm, ns), lambda i, j: (i, 0))],
        out_shape=[jax.ShapeDtypeStruct((M, NP), BF16), jax.ShapeDtypeStruct((M, ns), F32)],
        scratch_shapes=[pltpu.VMEM((tm, D_MODEL), BF16)],
        compiler_params=_cp(("parallel", "arbitrary")),
        name="proj",
    )(x2, gain, w_all, w_small)


def _split2(x):
    hi = x.astype(BF16)
    return hi, (x - hi.astype(F32)).astype(BF16)


def _gla_kernel(q_ref, k_ref, v_ref, gate_ref, lr_ref, wa_ref, ba_ref, gn_ref, o_ref, st_ref, g_ref, kf_ref, *, n_chunks):
    C, SB, H, DK, DV = GLA_CHUNK, GLA_SUB, GLA_HEADS, GLA_DK, GLA_DV

    @pl.when(pl.program_id(1) == 0)
    def _():
        st_ref[...] = jnp.zeros_like(st_ref)

    gn = gn_ref[...]
    ri = lax.broadcasted_iota(jnp.int32, (C, C), 0)
    ci = lax.broadcasted_iota(jnp.int32, (C, C), 1)
    tri = (ri >= ci).astype(BF16)
    si = lax.broadcasted_iota(jnp.int32, (SB, SB), 0)
    sj = lax.broadcasted_iota(jnp.int32, (SB, SB), 1)
    sub_causal = si >= sj

    lr_hi, lr_lo = _split2(lr_ref[0])
    wa_hi, wa_lo = _split2(wa_ref[...])
    z = (jnp.dot(lr_hi, wa_hi, preferred_element_type=F32) + jnp.dot(lr_hi, wa_lo, preferred_element_type=F32)
         + jnp.dot(lr_lo, wa_hi, preferred_element_type=F32)) + ba_ref[...]
    la = -(jnp.maximum(-z, 0.0) + jnp.log(1.0 + jnp.exp(-jnp.abs(z)))) * (LOG2E / GLA_TAU)
    la1 = la.astype(BF16)
    la2, la3 = _split2(la - la1.astype(F32))
    for c in range(n_chunks):
        r = slice(c * C, (c + 1) * C)
        g_ref[r, :] = (jnp.dot(tri, la1[r], preferred_element_type=F32) + jnp.dot(tri, la2[r], preferred_element_type=F32)
                       + jnp.dot(tri, la3[r], preferred_element_type=F32))

    kf_ref[...] = k_ref[0].astype(F32)

    def chunk(c):
        rows = pl.ds(c * C, C)
        for h in range(H):
            kl = slice(h * DK, (h + 1) * DK)
            vl = slice(h * DV, (h + 1) * DV)
            q = q_ref[0, rows, kl].astype(F32) * (DK ** -0.5)
            k = kf_ref[rows, kl]
            vb = v_ref[0, rows, vl]
            g = g_ref[rows, kl]

            outs = []
            for i in range(C // SB):
                lo = i * SB
                gi = g[lo:lo + SB]
                qi = q[lo:lo + SB]
                ki = k[lo:lo + SB]
                HB = SB // 2
                a_top = jnp.zeros((HB, SB), F32)
                a_bot = jnp.zeros((HB, SB), F32)
                for jj in range(SB):
                    row = slice(lo + jj, lo + jj + 1)
                    k_row = kf_ref.at[rows, kl][row, :]
                    g_row = g_ref.at[rows, kl][row, :]
                    qs, gs = (qi, gi) if jj < HB else (qi[HB:], gi[HB:])
                    w = (qs * k_row) * jnp.exp2(jnp.minimum(gs - g_row, 0.0))
                    col = jnp.sum(w, axis=-1, keepdims=True)
                    if jj < HB:
                        a_top = jnp.where(sj[:HB] == jj, col[:HB], a_top)
                        a_bot = jnp.where(sj[:HB] == jj, col[HB:], a_bot)
                    else:
                        a_bot = jnp.where(sj[:HB] == jj, col, a_bot)
                a_d = jnp.where(sub_causal, jnp.concatenate([a_top, a_bot], axis=0), 0.0)
                o_i = jnp.dot(a_d.astype(BF16), vb[lo:lo + SB], preferred_element_type=F32)
                if i > 0:
                    r = g[lo:lo + 1]
                    qt = qi * jnp.exp2(gi - r)
                    kt = k[:lo] * jnp.exp2(r - g[:lo])
                    a_o = lax.dot_general(qt.astype(BF16), kt.astype(BF16), NT, preferred_element_type=F32)
                    o_i = o_i + jnp.dot(a_o.astype(BF16), vb[:lo], preferred_element_type=F32)
                outs.append(o_i)
            o = jnp.concatenate(outs, axis=0)

            st = st_ref[h]
            qg = q * jnp.exp2(g)
            o = o + lax.dot_general(qg.astype(BF16), st.astype(BF16), NT, preferred_element_type=F32)
            gl = g[C - 1:C]
            kd = k * jnp.exp2(gl - g)
            st_ref[h] = st * jnp.exp2(gl) + lax.dot_general(vb, kd.astype(BF16), TN, preferred_element_type=F32)

            ms = jnp.mean(o * o, axis=-1, keepdims=True)
            y = o * lax.rsqrt(ms + EPS) * gn
            o_ref[0, rows, vl] = (y * _silu(gate_ref[0, rows, vl].astype(F32))).astype(o_ref.dtype)

    for c in range(n_chunks):
        chunk(c)


def _gla(proj3, small3, wa, ba, gn, tr=512):
    B, S, _ = proj3.shape
    H, DK, DV = GLA_HEADS, GLA_DK, GLA_DV
    kern = functools.partial(_gla_kernel, n_chunks=tr // GLA_CHUNK)
    return pl.pallas_call(
        kern,
        grid=(B, S // tr),
        in_specs=[
            pl.BlockSpec((1, tr, H * DK), lambda b, r: (b, r, C_GQ // (H * DK))),
            pl.BlockSpec((1, tr, H * DK), lambda b, r: (b, r, C_GK // (H * DK))),
            pl.BlockSpec((1, tr, H * DV), lambda b, r: (b, r, C_GV // (H * DV))),
            pl.BlockSpec((1, tr, H * DV), lambda b, r: (b, r, C_GG // (H * DV))),
            pl.BlockSpec((1, tr, 128), lambda b, r: (b, r, CS_LR // 128)),
            pl.BlockSpec((128, H * DK), lambda b, r: (0, 0)),
            pl.BlockSpec((1, H * DK), lambda b, r: (0, 0)),
            pl.BlockSpec((1, DV), lambda b, r: (0, 0)),
        ],
        out_specs=pl.BlockSpec((1, tr, H * DV), lambda b, r: (b, r, 0)),
        out_shape=jax.ShapeDtypeStruct((B, S, H * DV), BF16),
        scratch_shapes=[pltpu.VMEM((H, DV, DK), F32), pltpu.VMEM((tr, H * DK), F32), pltpu.VMEM((tr, H * DK), F32)],
        compiler_params=_cp(("parallel", "arbitrary")),
        name="gla",
    )(proj3, proj3, proj3, proj3, small3, wa, ba, gn)


def _group_meansq(x, gmat):
    return jnp.dot((x * x).astype(BF16), gmat, preferred_element_type=F32) * (1.0 / NSA_HD)


def _rope(x, cos, sin_signed, first_half):
    w = x.shape[-1]
    rot = jnp.where(first_half, pltpu.roll(x, w - NSA_HD // 2, 1), pltpu.roll(x, NSA_HD // 2, 1))
    return x * cos + rot * sin_signed


def _nsa_prep_kernel(q_ref, kvc_ref, kvs_ref, kvw_ref, pos_ref, inv_ref, qn_ref, kn_ref, gm_ref,
                     qr_ref, kc_ref, vc_ref, ks_ref, vst_ref, vwt_ref, tok_ref):
    tr = q_ref.shape[1]
    W = NSA_KV_HEADS * NSA_HD
    HD = NSA_HD
    gmat = gm_ref[...]
    ang = pos_ref[0].astype(F32) * inv_ref[...]
    cos1, sin1 = jnp.cos(ang), jnp.sin(ang)
    cos = jnp.concatenate([cos1, cos1], axis=1)
    lane = lax.broadcasted_iota(jnp.int32, (tr, W), 1)
    first_half = (lane % HD) < (HD // 2)
    sin = jnp.concatenate([sin1, sin1], axis=1)
    sin = jnp.where(first_half, -sin, sin)

    def norm_rope(x, gain):
        y = x * lax.rsqrt(_group_meansq(x, gmat) + EPS) * gain
        return _rope(y, cos, sin, first_half)

    qn = qn_ref[...]
    for s in range(NSA_HEADS * HD // W):
        xq = q_ref[0, :, s * W:(s + 1) * W].astype(F32)
        qr_ref[0, :, s * W:(s + 1) * W] = (norm_rope(xq, qn) * (HD ** -0.5 * LOG2E)).astype(qr_ref.dtype)

    toks = (norm_rope(kvc_ref[0, :, :W].astype(F32), kn_ref[0:1, :]), kvc_ref[0, :, W:].astype(F32))
    for j, dst in enumerate((kc_ref, vc_ref)):
        for h in range(NSA_KV_HEADS):
            tok_ref[j, h] = toks[j][:, h * HD:(h + 1) * HD]
            cols = [tok_ref[j, h, pl.ds(l, tr // CMP_STRIDE, stride=CMP_STRIDE), :] for l in range(CMP_STRIDE)]
            dst[0, h] = jnp.concatenate(cols, axis=1).astype(dst.dtype)

    ks = norm_rope(kvs_ref[0, :, :W].astype(F32), kn_ref[1:2, :]).astype(BF16)
    kw = norm_rope(kvw_ref[0, :, :W].astype(F32), kn_ref[2:3, :]).astype(BF16)
    row_blk = (pl.program_id(1) * tr + lax.broadcasted_iota(jnp.int32, (tr, SLC_LANES), 0)) // SLC_BLOCK
    onehot = (row_blk == lax.broadcasted_iota(jnp.int32, (tr, SLC_LANES), 1)).astype(BF16)
    vst = kvs_ref[0, :, W:].astype(F32).T
    for h in range(NSA_KV_HEADS):
        ks_ref[0, h] = jnp.concatenate([onehot, ks[:, h * HD:(h + 1) * HD], kw[:, h * HD:(h + 1) * HD]], axis=1)
        for j in range(tr // SLC_CHUNK):
            vst_ref[0, h, j, :HD] = vst[h * HD:(h + 1) * HD, j * SLC_CHUNK:(j + 1) * SLC_CHUNK].astype(BF16)
            vst_ref[0, h, j, HD:] = jnp.ones((VT_ROWS - HD, SLC_CHUNK), BF16)

    vwt = kvw_ref[0, :, W:].astype(F32).T
    for h in range(NSA_KV_HEADS):
        for j in range(tr // WIN_ALIGN):
            vwt_ref[0, h, j, :HD] = vwt[h * HD:(h + 1) * HD, j * WIN_ALIGN:(j + 1) * WIN_ALIGN].astype(BF16)
            vwt_ref[0, h, j, HD:] = jnp.ones((VT_ROWS - HD, WIN_ALIGN), BF16)


def _nsa_prep(proj3, pos3, inv, qn, kn, gmat, tr=512):
    B, S, _ = proj3.shape
    Hk, HD = NSA_KV_HEADS, NSA_HD
    hm = lambda w, dt: jax.ShapeDtypeStruct((B, Hk, S, w), dt)
    hspec = lambda w: pl.BlockSpec((1, Hk, tr, w), lambda b, r: (b, 0, r, 0))
    ctok = jax.ShapeDtypeStruct((B, Hk, S // CMP_STRIDE, CMP_STRIDE * HD), BF16)
    cspec = pl.BlockSpec((1, Hk, tr // CMP_STRIDE, CMP_STRIDE * HD), lambda b, r: (b, 0, r, 0))
    tspec = lambda c, rows=HD: pl.BlockSpec((1, Hk, tr // c, rows, c), lambda b, r: (b, 0, r, 0, 0))
    return pl.pallas_call(
        _nsa_prep_kernel,
        grid=(B, S // tr),
        in_specs=[
            pl.BlockSpec((1, tr, 1024), lambda b, r: (b, r, C_NQ // 1024)),
            pl.BlockSpec((1, tr, 512), lambda b, r: (b, r, C_NKV // 512)),
            pl.BlockSpec((1, tr, 512), lambda b, r: (b, r, C_NKV // 512 + 1)),
            pl.BlockSpec((1, tr, 512), lambda b, r: (b, r, C_NKV // 512 + 2)),
            pl.BlockSpec((1, tr, 1), lambda b, r: (b, r, 0)),
            pl.BlockSpec((1, 128), lambda b, r: (0, 0)),
            pl.BlockSpec((1, 256), lambda b, r: (0, 0)),
            pl.BlockSpec((3, 256), lambda b, r: (0, 0)),
            pl.BlockSpec((256, 256), lambda b, r: (0, 0)),
        ],
        out_specs=[pl.BlockSpec((1, tr, 1024), lambda b, r: (b, r, 0)), cspec, cspec, hspec(KAUG),
                   tspec(SLC_CHUNK, VT_ROWS), tspec(WIN_ALIGN, VT_ROWS)],
        out_shape=[jax.ShapeDtypeStruct((B, S, 1024), BF16), ctok, ctok, hm(KAUG, BF16),
                   jax.ShapeDtypeStruct((B, Hk, S // SLC_CHUNK, VT_ROWS, SLC_CHUNK), BF16),
                   jax.ShapeDtypeStruct((B, Hk, S // WIN_ALIGN, VT_ROWS, WIN_ALIGN), BF16)],
        scratch_shapes=[pltpu.VMEM((2, Hk, tr, HD), F32)],
        compiler_params=_cp(("parallel", "parallel")),
        name="nsa_prep",
    )(proj3, proj3, proj3, proj3, pos3, inv, qn, kn, gmat)


def _compress_kernel(t_ref, pe_ref, w1a_ref, w1b_ref, w2_ref, *rest, transposed):
    tail_ref, o_ref = rest if transposed else (None,) + rest
    t = t_ref[0, 0]
    w1a, w1b = w1a_ref[...], w1b_ref[...]
    u = jnp.dot(t, w1a, preferred_element_type=F32)
    v = jnp.dot(t, w1b, preferred_element_type=F32)
    pe = pe_ref[...].astype(BF16)
    c = (jnp.dot(pe[0:8], w1a, preferred_element_type=F32) + jnp.dot(pe[8:16], w1b, preferred_element_type=F32))[0:1]
    n = v.shape[0]
    h = u + pltpu.roll(v, n - 1, 0) + c
    h = jax.nn.gelu(h).astype(BF16)
    if transposed:
        hd = w2_ref.shape[0]
        o_ref[0, 0, :hd] = lax.dot_general(w2_ref[...], h, NT, preferred_element_type=F32).astype(o_ref.dtype)
        o_ref[0, 0, hd:] = tail_ref[...]
    else:
        o_ref[0, 0] = jnp.dot(h, w2_ref[...], preferred_element_type=F32).astype(o_ref.dtype)


def _compress(tok, pe2, w1a, w1b, w2, tail=None):
    B, Hk, n, _ = tok.shape
    HD = NSA_HD
    t2 = tok
    transposed = tail is not None
    oshape = (HD + tail.shape[0], n) if transposed else (n, HD)
    w2 = w2.T if transposed else w2
    extra = ([tail], [pl.BlockSpec(tail.shape, lambda b, h: (0, 0))]) if transposed else ([], [])
    return pl.pallas_call(
        functools.partial(_compress_kernel, transposed=transposed),
        grid=(B, Hk),
        in_specs=[
            pl.BlockSpec((1, 1, n, CMP_STRIDE * HD), lambda b, h: (b, h, 0, 0)),
            pl.BlockSpec((16, CMP_STRIDE * HD), lambda b, h: (0, 0)),
            pl.BlockSpec((CMP_STRIDE * HD, CMP_HIDDEN), lambda b, h: (0, 0)),
            pl.BlockSpec((CMP_STRIDE * HD, CMP_HIDDEN), lambda b, h: (0, 0)),
            pl.BlockSpec(w2.shape, lambda b, h: (0, 0)),
        ] + extra[1],
        out_specs=pl.BlockSpec((1, 1) + oshape, lambda b, h: (b, h, 0, 0)),
        out_shape=jax.ShapeDtypeStruct((B, Hk) + oshape, BF16),
        compiler_params=_cp(("parallel", "parallel")),
        name="compress_v" if transposed else "compress_k",
    )(t2, pe2, w1a, w1b, w2, *extra[0])


def _nsa_attn_kernel(q_ref, kc_ref, vct_ref, ks_ref, vst_ref, vwt_ref, gt_ref, gate_ref, tri_ref, cmask_ref, wmask_ref,
                     o_ref,
                     s_ref, e_ref):
    G, HD, QB, NH, NB = NSA_GROUP, NSA_HD, SLC_BLOCK, NSA_STEP_HEADS, NSA_STEP_BLOCKS
    R = NB * G * QB
    heads = range(NH)
    hk0 = pl.program_id(1) * NH
    step = pl.program_id(2)
    t0 = step * (NB * QB)
    q = [jnp.concatenate([q_ref[0, a * QB:(a + 1) * QB, (h * G + g) * HD:(h * G + g + 1) * HD]
                          for a in range(NB) for g in range(G)], axis=0)
         for h in heads]

    nc = kc_ref.shape[2]
    cmask = cmask_ref[pl.ds(pl.multiple_of(nc - t0 // CMP_STRIDE, CMP_STRIDE), nc), :]
    for h in heads:
        sm = lax.dot_general(kc_ref[0, h], q[h], NT, preferred_element_type=F32) + cmask
        m = jnp.maximum(jnp.max(sm, axis=0, keepdims=True), 0.1 * NEG)
        e_ref[h] = jnp.exp2(sm - m).astype(BF16)
    QL = max(NB * QB, 2 * QB)
    blk = lax.broadcasted_iota(jnp.int32, (SLC_LANES, QL), 0)
    qlane = lax.broadcasted_iota(jnp.int32, (1, QL), 1)
    cur = step * NB + (qlane // QB) % NB
    forced = (blk == 0) | (blk == cur) | (blk == cur - 1)
    o_cmp, score = [], []
    for h in heads:
        oc = jnp.dot(vct_ref[0, h], e_ref[h], preferred_element_type=F32)
        den = oc[HD:HD + 1]
        rden = 1.0 / jnp.where(den > 0.0, den, 1.0)
        o_cmp.append(oc[:HD] * rden)
        impf = oc[VT_ROWS:] * rden
        parts = []
        for a in range(NB):
            pa = impf[:, a * G * QB:(a + 1) * G * QB]
            p2 = pa[:, :2 * QB] + pa[:, 2 * QB:]
            parts.append(p2 + pltpu.roll(p2, QB, 1))
        first_half = qlane[:, :2 * QB] < QB
        imp = parts[0] if NB == 1 else jnp.concatenate(
            [jnp.where(first_half, parts[a], parts[a + 1]) for a in range(0, NB, 2)], axis=1)
        score.append(jnp.where(forced, -jnp.inf, jnp.where(blk <= cur, imp, NEG)))

    wc = jnp.maximum(t0 - WINDOW, 0) // WIN_ALIGN
    w0 = pl.multiple_of(wc * WIN_ALIGN, WIN_ALIGN)
    wmask = wmask_ref[pl.ds(pl.multiple_of(jnp.maximum(WINDOW - t0, 0), NB * QB), WIN_KEYS), :]
    o_win = []
    for h in heads:
        q_win = jnp.concatenate([jnp.zeros((R, SLC_LANES + HD), BF16), q[h]], axis=1)
        sw = lax.dot_general(ks_ref[0, h, pl.ds(w0, WIN_KEYS), :], q_win, NT, preferred_element_type=F32)
        sw = sw + wmask
        pw = jnp.exp2(sw - jnp.max(sw, axis=0, keepdims=True))
        vv = jnp.concatenate([vwt_ref[0, h, wc + j] for j in range(WIN_KEYS // WIN_ALIGN)], axis=1)
        ow = jnp.dot(vv, pw.astype(BF16), preferred_element_type=F32)
        o_win.append(ow[:HD] * (1.0 / ow[HD:HD + 1]))

    few = cur < SLC_TOPN
    causal = blk <= cur
    blkf = blk.astype(F32)

    def pick(sc):
        mx = jnp.max(sc, axis=0, keepdims=True)
        first = jnp.min(jnp.where(sc == mx, blkf, float(SLC_LANES)), axis=0, keepdims=True)
        return jnp.where(blkf == first, -jnp.inf, sc)

    for _ in range(SLC_TOPN - 3):
        score = [pick(sc) for sc in score]
    q_aug = []
    for h in heads:
        chosen = forced | ((few | (score[h] == -jnp.inf)) & causal)
        bias_t = jnp.where(chosen, 0.0, NEG)
        bias = bias_t.T.astype(BF16)
        rows = jnp.concatenate([bias[a * QB:(a + 1) * QB] for a in range(NB) for g in range(G)], axis=0)
        q_aug.append(jnp.concatenate([rows, q[h], jnp.zeros((R, HD), BF16)], axis=1))

    def scores(c, slot):
        k0 = pl.multiple_of(c * SLC_CHUNK, SLC_CHUNK)
        for h in heads:
            s_ref[slot, h] = lax.dot_general(ks_ref[0, h, pl.ds(k0, SLC_CHUNK), :], q_aug[h], NT, preferred_element_type=F32)

    def absorb(c, slot, carry):
        out = []
        for h in heads:
            m_i, acc = carry[h]
            sc = s_ref[slot, h]
            m_new = jnp.maximum(m_i, jnp.max(sc, axis=0, keepdims=True))
            pp = jnp.exp2(sc - m_new).astype(BF16)
            acc = jnp.exp2(m_i - m_new) * acc + jnp.dot(vst_ref[0, h, c], pp, preferred_element_type=F32)
            out.append((m_new, acc))
        return tuple(out)

    diag = t0 // SLC_CHUNK
    n_pairs = diag // 2
    scores(0, 0)

    def pair(j, carry):
        scores(2 * j + 1, 1)
        carry = absorb(2 * j, 0, carry)
        scores(2 * j + 2, 0)
        return absorb(2 * j + 1, 1, carry)

    init = tuple((jnp.full((1, R), NEG, F32), jnp.zeros((VT_ROWS, R), F32)) for h in heads)
    carry = lax.fori_loop(0, n_pairs, pair, init)
    scores(2 * n_pairs + 1, 1)
    r0 = pl.multiple_of(t0 - diag * SLC_CHUNK, NB * QB)
    tri = tri_ref[...]
    for h in heads:
        s_ref[diag % 2, h, pl.ds(r0, NB * QB), :] += tri
    carry = absorb(2 * n_pairs, 0, carry)
    carry = absorb(2 * n_pairs + 1, 1, carry)

    sig = _sigmoid(gt_ref[0])
    glane = lax.broadcasted_iota(jnp.int32, (QB, 128), 1)
    outs = [[] for a in range(NB)]
    for h in heads:
        acc_s = carry[h][1]
        oc, os_, ow = o_cmp[h].T, (acc_s[:HD] * (1.0 / acc_s[HD:HD + 1])).T, o_win[h].T
        for a in range(NB):
            sig_a = sig[a * QB:(a + 1) * QB]
            for g in range(G):
                base = ((hk0 + h) * G + g) * N_NSA_BRANCH
                gc, gs, gw = [jnp.sum(jnp.where(glane == base + b, sig_a, 0.0), axis=-1, keepdims=True) for b in range(3)]
                r = slice((a * G + g) * QB, (a * G + g + 1) * QB)
                outs[a].append(gc * oc[r] + gs * os_[r] + gw * ow[r])
    o = jnp.concatenate([jnp.concatenate(oa, axis=1) for oa in outs], axis=0)
    o_ref[0] = (o * _silu(gate_ref[0].astype(F32))).astype(o_ref.dtype)


def _nsa_attn(qr, kc, vct, ks, vst, vwt, proj3, small3):
    B, S, _ = qr.shape
    Hk, HD, NH = NSA_KV_HEADS, NSA_HD, NSA_STEP_HEADS
    TQ = NSA_STEP_BLOCKS * SLC_BLOCK
    r = np.arange(TQ)[:, None]
    ln = np.arange(NSA_GROUP * TQ)[None, :]
    a, qi = ln // (NSA_GROUP * SLC_BLOCK), ln % SLC_BLOCK
    tri = jnp.asarray(np.where((r // SLC_BLOCK != a) | (r % SLC_BLOCK <= qi), 0.0, NEG), F32)
    nc = kc.shape[2]
    tl = a * SLC_BLOCK + qi
    u = np.arange(2 * nc)[:, None]
    cmask = jnp.asarray(np.where(u - nc <= (tl - (CMP_LEN - 1)) // CMP_STRIDE, 0.0, NEG), F32)
    u = np.arange(WINDOW + WIN_KEYS)[:, None]
    wmask = jnp.asarray(np.where((u > tl) & (u <= WINDOW + tl), 0.0, NEG), F32)
    const = lambda c: pl.BlockSpec(c.shape, lambda b, h, t: (0, 0), pipeline_mode=pl.Buffered(1))
    W = NH * NSA_GROUP * HD
    full = lambda a: pl.BlockSpec((1, NH) + a.shape[2:], lambda b, h, t: (b, h) + (0,) * (a.ndim - 2),
                                  pipeline_mode=pl.Buffered(1))
    return pl.pallas_call(
        _nsa_attn_kernel,
        grid=(B, Hk // NH, S // TQ),
        in_specs=[
            pl.BlockSpec((1, TQ, W), lambda b, h, t: (b, t, h)),
            full(kc), full(vct), full(ks), full(vst), full(vwt),
            pl.BlockSpec((1, TQ, 128), lambda b, h, t: (b, t, CS_NSG // 128)),
            pl.BlockSpec((1, TQ, W), lambda b, h, t: (b, t, C_NGATE // W + h)),
            const(tri), const(cmask), const(wmask),
        ],
        out_specs=pl.BlockSpec((1, TQ, W), lambda b, h, t: (b, t, h)),
        out_shape=jax.ShapeDtypeStruct((B, S, NSA_HEADS * HD), BF16),
        scratch_shapes=[pltpu.VMEM((2, NH, SLC_CHUNK, NSA_GROUP * TQ), F32), pltpu.VMEM((NH, nc, NSA_GROUP * TQ), BF16)],
        compiler_params=_cp(("parallel", "parallel", "arbitrary")),
        name="nsa_attn",
    )(qr, kc, vct, ks, vst, vwt, small3, proj3, tri, cmask, wmask)


def _head_rms(x, gain, hd):
    outs = []
    for h in range(x.shape[-1] // hd):
        xh = x[:, h * hd:(h + 1) * hd]
        ms = jnp.mean(xh * xh, axis=-1, keepdims=True)
        outs.append(xh * lax.rsqrt(ms + EPS) * gain)
    return jnp.concatenate(outs, axis=1)


def _mem_prep_kernel(mem_ref, g_ref, w_ref, kn_ref, mk_ref, mv_ref):
    x = mem_ref[0]
    ms = jnp.mean(x * x, axis=-1, keepdims=True)
    xn = (x * lax.rsqrt(ms + EPS) * g_ref[...]).astype(BF16)
    kv = jnp.dot(xn, w_ref[...], preferred_element_type=F32)
    W = MEM_HEADS * MEM_HD
    mk_ref[0] = _head_rms(kv[:, :W], kn_ref[...], MEM_HD).astype(mk_ref.dtype)
    mv_ref[0] = kv[:, W:].astype(mv_ref.dtype)


def _mem_prep(mem, gain, w_kv, kn):
    B, N, D = mem.shape
    W = MEM_HEADS * MEM_HD
    return pl.pallas_call(
        _mem_prep_kernel,
        grid=(B,),
        in_specs=[
            pl.BlockSpec((1, N, D), lambda b: (b, 0, 0)),
            pl.BlockSpec((1, D), lambda b: (0, 0)),
            pl.BlockSpec((D, 2 * W), lambda b: (0, 0)),
            pl.BlockSpec((1, MEM_HD), lambda b: (0, 0)),
        ],
        out_specs=[pl.BlockSpec((1, N, W), lambda b: (b, 0, 0))] * 2,
        out_shape=[jax.ShapeDtypeStruct((B, N, W), BF16)] * 2,
        compiler_params=_cp(("parallel",)),
        name="mem_prep",
    )(mem, gain, w_kv, kn)


def _final_kernel(x_ref, ya_ref, yb_ref, mq_ref, mg_ref, mr_ref, mk_ref, mv_ref, qn_ref, wb_ref, wo_ref, o_ref):
    mq = _head_rms(mq_ref[...].astype(F32), qn_ref[...], MEM_HD)
    mk = mk_ref[0]
    mv = mv_ref[0]
    heads = []
    for h in range(MEM_HEADS):
        sl = slice(h * MEM_HD, (h + 1) * MEM_HD)
        s = lax.dot_general(mq[:, sl].astype(BF16), mk[:, sl], NT, preferred_element_type=F32) * (MEM_HD ** -0.5)
        m = jnp.max(s, axis=-1, keepdims=True)
        p = jnp.exp(s - m)
        o = jnp.dot(p.astype(BF16), mv[:, sl], preferred_element_type=F32) / jnp.sum(p, axis=-1, keepdims=True)
        heads.append(o)
    ym = jnp.concatenate(heads, axis=1) * _silu(mg_ref[...].astype(F32))

    mixed = None
    for c, y in enumerate((ya_ref[...], yb_ref[...], ym.astype(BF16))):
        z = jnp.dot(y, wb_ref[c], preferred_element_type=F32)
        term = _sigmoid(mr_ref[:, c * D_MODEL:(c + 1) * D_MODEL].astype(F32)) * z
        mixed = term if mixed is None else mixed + term
    o_ref[...] = x_ref[...] + jnp.dot(mixed.astype(BF16), wo_ref[...], preferred_element_type=F32)


def _final(x2, ya2, yb2, proj2, mk, mv, qn, wb, wo, S, tr=512):
    M, D = x2.shape
    N = mk.shape[1]
    nb = S // tr
    row = lambda c: pl.BlockSpec((tr, D), lambda i: (i, c))
    return pl.pallas_call(
        _final_kernel,
        grid=(M // tr,),
        in_specs=[
            row(0), row(0), row(0),
            row(C_MQ // D), row(C_MG // D),
            pl.BlockSpec((tr, 3 * D), lambda i: (i, C_MERGE // (3 * D))),
            pl.BlockSpec((1, N, D), lambda i: (i // nb, 0, 0)),
            pl.BlockSpec((1, N, D), lambda i: (i // nb, 0, 0)),
            pl.BlockSpec((1, MEM_HD), lambda i: (0, 0)),
            pl.BlockSpec((3, D, D), lambda i: (0, 0, 0), pipeline_mode=pl.Buffered(1)),
            pl.BlockSpec((D, D), lambda i: (0, 0), pipeline_mode=pl.Buffered(1)),
        ],
        out_specs=pl.BlockSpec((tr, D), lambda i: (i, 0)),
        out_shape=jax.ShapeDtypeStruct((M, D), F32),
        compiler_params=_cp(("parallel",)),
        name="final",
    )(x2, ya2, yb2, proj2, proj2, proj2, mk, mv, qn, wb, wo)


def _regroup_kernel(w_ref, o_ref, s_ref, *, o, runs):
    dst = 0
    for a, b in runs:
        o_ref[:, dst:dst + o[b] - o[a]] = w_ref[:, o[a]:o[b]].astype(o_ref.dtype)
        dst += o[b] - o[a]
    o_ref[:, dst:] = jnp.zeros((o_ref.shape[0], o_ref.shape[1] - dst), o_ref.dtype)
    for sec, at in ((3, CS_LR), (7, CS_NSG)):
        wd = o[sec + 1] - o[sec]
        s_ref[:, at:at + wd] = w_ref[:, o[sec]:o[sec + 1]].astype(s_ref.dtype)
        s_ref[:, at + wd:at + 128] = jnp.zeros((s_ref.shape[0], 128 - wd), s_ref.dtype)


def _regroup(w_in, o, tr=128):
    D = w_in.shape[0]
    runs = ((2, 3), (4, 6), (8, 12), (6, 7), (0, 2))
    return pl.pallas_call(
        functools.partial(_regroup_kernel, o=o, runs=runs),
        grid=(D // tr,),
        in_specs=[pl.BlockSpec((tr, w_in.shape[1]), lambda i: (i, 0))],
        out_specs=[pl.BlockSpec((tr, NP), lambda i: (i, 0)), pl.BlockSpec((tr, NP_SMALL), lambda i: (i, 0))],
        out_shape=[jax.ShapeDtypeStruct((D, NP), BF16), jax.ShapeDtypeStruct((D, NP_SMALL), BF16)],
        compiler_params=_cp(("parallel",)),
        name="w_regroup",
    )(w_in)


def _overlap_matrix_t(S):
    n_cmp = (S - CMP_LEN) // CMP_STRIDE + 1
    n_slc = S // SLC_BLOCK
    cs = np.arange(n_cmp)[:, None] * CMP_STRIDE
    ss = np.arange(n_slc)[None, :] * SLC_BLOCK
    ov = np.clip(np.minimum(cs + CMP_LEN, ss + SLC_BLOCK) - np.maximum(cs, ss), 0, None) / CMP_LEN
    out = np.zeros((SLC_LANES, S // CMP_STRIDE), np.float32)
    out[:n_slc, :n_cmp] = ov.T
    return jnp.asarray(out, BF16)


def _layer(x, mem, positions, norm_gain, mem_norm_gain, w_in, w_gla_alpha, b_gla_alpha, gla_out_norm,
           nsa_q_norm, nsa_k_norm, pe_cmp_k, pe_cmp_v, w_cmp_k1, w_cmp_k2, w_cmp_v1, w_cmp_v2,
           w_mem_kv, mem_q_norm, mem_k_norm, w_branch, w_out):
    B, S, D = x.shape
    assert D == D_MODEL and S % (2 * SLC_CHUNK) == 0 and S >= WIN_KEYS and S // SLC_BLOCK <= SLC_LANES
    assert SLC_TOPN % NSA_STEP_BLOCKS == 0 and SLC_CHUNK % (NSA_STEP_BLOCKS * SLC_BLOCK) == 0

    o = np.cumsum([0, 512, 512, 1024, 16, 1024, 1024, 1536, 48, 1024, 1024, 1024, 3072])
    w_all, w_small = _regroup(w_in, [int(v) for v in o])

    x2 = x.reshape(B * S, D)
    proj2, small2 = _proj(x2, norm_gain.reshape(1, D), w_all, w_small)
    proj3 = proj2.reshape(B, S, NP)
    small3 = small2.reshape(B, S, NP_SMALL)

    wa = jnp.pad(w_gla_alpha, ((0, 128 - GLA_RANK), (0, 0)))
    ya = _gla(proj3, small3, wa, b_gla_alpha.reshape(1, -1), gla_out_norm.reshape(1, -1))

    half = NSA_HD // 2
    inv = ROPE_THETA ** (-jnp.arange(half, dtype=F32) / half)
    inv = jnp.tile(inv, 128 // half).reshape(1, 128)
    qn = jnp.tile(nsa_q_norm, NSA_KV_HEADS).reshape(1, -1)
    kn = jnp.tile(nsa_k_norm, (1, NSA_KV_HEADS))
    gid = np.arange(256) // NSA_HD
    gmat = jnp.asarray(gid[:, None] == gid[None, :], BF16)
    qr, kc_tok, vc_tok, ks, vst, vwt = _nsa_prep(proj3, positions.reshape(B, S, 1), inv, qn, kn, gmat)

    def cmp_args(pe, w1, w2):
        pe2 = pe.reshape(2, CMP_STRIDE * NSA_HD)
        pe16 = jnp.concatenate([jnp.broadcast_to(pe2[0:1], (8, pe2.shape[1])), jnp.broadcast_to(pe2[1:2], (8, pe2.shape[1]))], 0)
        w1f = w1.reshape(CMP_LEN * NSA_HD, CMP_HIDDEN).astype(BF16)
        return pe16, w1f[:CMP_STRIDE * NSA_HD], w1f[CMP_STRIDE * NSA_HD:], w2.astype(BF16)

    kc = _compress(kc_tok, *cmp_args(pe_cmp_k, w_cmp_k1, w_cmp_k2))
    tail = jnp.concatenate([jnp.ones((VT_ROWS - NSA_HD, S // CMP_STRIDE), BF16), _overlap_matrix_t(S)], axis=0)
    vct = _compress(vc_tok, *cmp_args(pe_cmp_v, w_cmp_v1, w_cmp_v2), tail=tail)
    yb = _nsa_attn(qr, kc, vct, ks, vst, vwt, proj3, small3)

    mk, mv = _mem_prep(mem, mem_norm_gain.reshape(1, D), w_mem_kv.astype(BF16), mem_k_norm.reshape(1, -1))
    out = _final(x2, ya.reshape(B * S, D), yb.reshape(B * S, D), proj2, mk, mv, mem_q_norm.reshape(1, -1),
                 w_branch.astype(BF16), w_out.astype(BF16), S)
    return out.reshape(B, S, D)


def kernel(x, mem, positions, norm_gain, mem_norm_gain, w_in, w_gla_alpha, b_gla_alpha, gla_out_norm, nsa_q_norm, nsa_k_norm, pe_cmp_k, pe_cmp_v, w_cmp_k1, w_cmp_k2, w_cmp_v1, w_cmp_v2, w_mem_kv, mem_q_norm, mem_k_norm, w_branch, w_out):
    h = x
    for l in range(norm_gain.shape[0]):
        h = _layer(h, mem, positions, norm_gain[l], mem_norm_gain[l], w_in[l], w_gla_alpha[l], b_gla_alpha[l],
                   gla_out_norm[l], nsa_q_norm[l], nsa_k_norm[l], pe_cmp_k[l], pe_cmp_v[l], w_cmp_k1[l],
                   w_cmp_k2[l], w_cmp_v1[l], w_cmp_v2[l], w_mem_kv[l], mem_q_norm[l], mem_k_norm[l],
                   w_branch[l], w_out[l])
    return h
```

```python
import functools

import numpy as np
import jax
import jax.numpy as jnp
from jax import lax
from jax.experimental import pallas as pl
from jax.experimental.pallas import tpu as pltpu

F32 = jnp.float32
BF16 = jnp.bfloat16

D_MODEL = 1024
ROPE_THETA = 10000.0
EPS = 1e-6
NEG = -1e30

GLA_HEADS = 4
GLA_DK = 128
GLA_DV = 256
GLA_RANK = 16
GLA_TAU = 16.0
GLA_CHUNK = 64
GLA_SUB = 16

NSA_HEADS = 16
NSA_KV_HEADS = 4
NSA_GROUP = 4
NSA_HD = 64
CMP_LEN = 32
CMP_STRIDE = 16
CMP_HIDDEN = 256
SLC_BLOCK = 64
SLC_TOPN = 16
WINDOW = 512
N_NSA_BRANCH = 3
SLC_LANES = 128
SLC_CHUNK = 512
VT_ROWS = NSA_HD + 16
LOG2E = 1.4426950408889634
WIN_ALIGN = 128
NSA_STEP_HEADS = 2
NSA_STEP_BLOCKS = 4
WIN_KEYS = WINDOW + max(NSA_STEP_BLOCKS * SLC_BLOCK, WIN_ALIGN)
KAUG = SLC_LANES + 2 * NSA_HD

MEM_HEADS = 4
MEM_HD = 256

C_GV, C_GG, C_NQ, C_NGATE, C_MQ, C_MG, C_MERGE = 0, 1024, 2048, 3072, 4096, 5120, 6144
C_NKV, C_GQ, C_GK = 9216, 10752, 11264
NP = 12288
CS_LR, CS_NSG, NP_SMALL = 0, 128, 256

VMEM_LIMIT = 48 * 1024 * 1024

NT = (((1,), (1,)), ((), ()))
TN = (((0,), (0,)), ((), ()))


def _cp(sem):
    return pltpu.CompilerParams(dimension_semantics=sem, vmem_limit_bytes=VMEM_LIMIT)


def _silu(x):
    return x * (1.0 / (1.0 + jnp.exp(-x)))


def _sigmoid(x):
    return 1.0 / (1.0 + jnp.exp(-x))


def _proj_kernel(x_ref, g_ref, w_ref, ws_ref, o_ref, os_ref, xn_ref):
    @pl.when(pl.program_id(1) == 0)
    def _():
        x = x_ref[...]
        ms = jnp.mean(x * x, axis=-1, keepdims=True)
        xn = (x * lax.rsqrt(ms + EPS) * g_ref[...]).astype(BF16)
        xn_ref[...] = xn
        os_ref[...] = lax.dot_general(xn, ws_ref[...], NT, preferred_element_type=F32)

    o_ref[...] = lax.dot_general(xn_ref[...], w_ref[...], NT, preferred_element_type=F32).astype(o_ref.dtype)


def _proj(x2, gain, w_all, w_small, tm=1024, tn=2048):
    M = x2.shape[0]
    ns = w_small.shape[0]
    return pl.pallas_call(
        _proj_kernel,
        grid=(M // tm, NP // tn),
        in_specs=[
            pl.BlockSpec((tm, D_MODEL), lambda i, j: (i, 0)),
            pl.BlockSpec((1, D_MODEL), lambda i, j: (0, 0)),
            pl.BlockSpec((tn, D_MODEL), lambda i, j: (j, 0)),
            pl.BlockSpec((ns, D_MODEL), lambda i, j: (0, 0)),
        ],
        out_specs=[pl.BlockSpec((tm, tn), lambda i, j: (i, j)), pl.BlockSpec((tm, ns), lambda i, j: (i, 0))],
        out_shape=[jax.ShapeDtypeStruct((M, NP), BF16), jax.ShapeDtypeStruct((M, ns), F32)],
        scratch_shapes=[pltpu.VMEM((tm, D_MODEL), BF16)],
        compiler_params=_cp(("parallel", "arbitrary")),
        name="proj",
    )(x2, gain, w_all, w_small)


def _split2(x):
    hi = x.astype(BF16)
    return hi, (x - hi.astype(F32)).astype(BF16)


def _gla_kernel(q_ref, k_ref, v_ref, gate_ref, lr_ref, wa_ref, ba_ref, gn_ref, o_ref, st_ref, g_ref, kf_ref, *, n_chunks):
    C, SB, H, DK, DV = GLA_CHUNK, GLA_SUB, GLA_HEADS, GLA_DK, GLA_DV

    @pl.when(pl.program_id(1) == 0)
    def _():
        st_ref[...] = jnp.zeros_like(st_ref)

    gn = gn_ref[...]
    ri = lax.broadcasted_iota(jnp.int32, (C, C), 0)
    ci = lax.broadcasted_iota(jnp.int32, (C, C), 1)
    tri = (ri >= ci).astype(BF16)
    si = lax.broadcasted_iota(jnp.int32, (SB, SB), 0)
    sj = lax.broadcasted_iota(jnp.int32, (SB, SB), 1)
    sub_causal = si >= sj

    lr_hi, lr_lo = _split2(lr_ref[0])
    wa_hi, wa_lo = _split2(wa_ref[...])
    z = (jnp.dot(lr_hi, wa_hi, preferred_element_type=F32) + jnp.dot(lr_hi, wa_lo, preferred_element_type=F32)
         + jnp.dot(lr_lo, wa_hi, preferred_element_type=F32)) + ba_ref[...]
    la = -(jnp.maximum(-z, 0.0) + jnp.log(1.0 + jnp.exp(-jnp.abs(z)))) * (LOG2E / GLA_TAU)
    la1 = la.astype(BF16)
    la2, la3 = _split2(la - la1.astype(F32))
    for c in range(n_chunks):
        r = slice(c * C, (c + 1) * C)
        g_ref[r, :] = (jnp.dot(tri, la1[r], preferred_element_type=F32) + jnp.dot(tri, la2[r], preferred_element_type=F32)
                       + jnp.dot(tri, la3[r], preferred_element_type=F32))

    kf_ref[...] = k_ref[0].astype(F32)

    def chunk(c):
        rows = pl.ds(c * C, C)
        for h in range(H):
            kl = slice(h * DK, (h + 1) * DK)
            vl = slice(h * DV, (h + 1) * DV)
            q = q_ref[0, rows, kl].astype(F32) * (DK ** -0.5)
            k = kf_ref[rows, kl]
            vb = v_ref[0, rows, vl]
            g = g_ref[rows, kl]

            outs = []
            for i in range(C // SB):
                lo = i * SB
                gi = g[lo:lo + SB]
                qi = q[lo:lo + SB]
                ki = k[lo:lo + SB]
                HB = SB // 2
                a_top = jnp.zeros((HB, SB), F32)
                a_bot = jnp.zeros((HB, SB), F32)
                for jj in range(SB):
                    row = slice(lo + jj, lo + jj + 1)
                    k_row = kf_ref.at[rows, kl][row, :]
                    g_row = g_ref.at[rows, kl][row, :]
                    qs, gs = (qi, gi) if jj < HB else (qi[HB:], gi[HB:])
                    w = (qs * k_row) * jnp.exp2(jnp.minimum(gs - g_row, 0.0))
                    col = jnp.sum(w, axis=-1, keepdims=True)
                    if jj < HB:
                        a_top = jnp.where(sj[:HB] == jj, col[:HB], a_top)
                        a_bot = jnp.where(sj[:HB] == jj, col[HB:], a_bot)
                    else:
                        a_bot = jnp.where(sj[:HB] == jj, col, a_bot)
                a_d = jnp.where(sub_causal, jnp.concatenate([a_top, a_bot], axis=0), 0.0)
                o_i = jnp.dot(a_d.astype(BF16), vb[lo:lo + SB], preferred_element_type=F32)
                if i > 0:
                    r = g[lo:lo + 1]
                    qt = qi * jnp.exp2(gi - r)
                    kt = k[:lo] * jnp.exp2(r - g[:lo])
                    a_o = lax.dot_general(qt.astype(BF16), kt.astype(BF16), NT, preferred_element_type=F32)
                    o_i = o_i + jnp.dot(a_o.astype(BF16), vb[:lo], preferred_element_type=F32)
                outs.append(o_i)
            o = jnp.concatenate(outs, axis=0)

            st = st_ref[h]
            qg = q * jnp.exp2(g)
            o = o + lax.dot_general(qg.astype(BF16), st.astype(BF16), NT, preferred_element_type=F32)
            gl = g[C - 1:C]
            kd = k * jnp.exp2(gl - g)
            st_ref[h] = st * jnp.exp2(gl) + lax.dot_general(vb, kd.astype(BF16), TN, preferred_element_type=F32)

            ms = jnp.mean(o * o, axis=-1, keepdims=True)
            y = o * lax.rsqrt(ms + EPS) * gn
            o_ref[0, rows, vl] = (y * _silu(gate_ref[0, rows, vl].astype(F32))).astype(o_ref.dtype)

    for c in range(n_chunks):
        chunk(c)


def _gla(proj3, small3, wa, ba, gn, tr=512):
    B, S, _ = proj3.shape
    H, DK, DV = GLA_HEADS, GLA_DK, GLA_DV
    kern = functools.partial(_gla_kernel, n_chunks=tr // GLA_CHUNK)
    return pl.pallas_call(
        kern,
        grid=(B, S // tr),
        in_specs=[
            pl.BlockSpec((1, tr, H * DK), lambda b, r: (b, r, C_GQ // (H * DK))),
            pl.BlockSpec((1, tr, H * DK), lambda b, r: (b, r, C_GK // (H * DK))),
            pl.BlockSpec((1, tr, H * DV), lambda b, r: (b, r, C_GV // (H * DV))),
            pl.BlockSpec((1, tr, H * DV), lambda b, r: (b, r, C_GG // (H * DV))),
            pl.BlockSpec((1, tr, 128), lambda b, r: (b, r, CS_LR // 128)),
            pl.BlockSpec((128, H * DK), lambda b, r: (0, 0)),
            pl.BlockSpec((1, H * DK), lambda b, r: (0, 0)),
            pl.BlockSpec((1, DV), lambda b, r: (0, 0)),
        ],
        out_specs=pl.BlockSpec((1, tr, H * DV), lambda b, r: (b, r, 0)),
        out_shape=jax.ShapeDtypeStruct((B, S, H * DV), BF16),
        scratch_shapes=[pltpu.VMEM((H, DV, DK), F32), pltpu.VMEM((tr, H * DK), F32), pltpu.VMEM((tr, H * DK), F32)],
        compiler_params=_cp(("parallel", "arbitrary")),
        name="gla",
    )(proj3, proj3, proj3, proj3, small3, wa, ba, gn)


def _group_meansq(x, gmat):
    return jnp.dot((x * x).astype(BF16), gmat, preferred_element_type=F32) * (1.0 / NSA_HD)


def _rope(x, cos, sin_signed, first_half):
    w = x.shape[-1]
    rot = jnp.where(first_half, pltpu.roll(x, w - NSA_HD // 2, 1), pltpu.roll(x, NSA_HD // 2, 1))
    return x * cos + rot * sin_signed


def _nsa_prep_kernel(q_ref, kvc_ref, kvs_ref, kvw_ref, pos_ref, inv_ref, qn_ref, kn_ref, gm_ref,
                     qr_ref, kc_ref, vc_ref, ks_ref, vst_ref, vwt_ref, tok_ref):
    tr = q_ref.shape[1]
    W = NSA_KV_HEADS * NSA_HD
    HD = NSA_HD
    gmat = gm_ref[...]
    ang = pos_ref[0].astype(F32) * inv_ref[...]
    cos1, sin1 = jnp.cos(ang), jnp.sin(ang)
    cos = jnp.concatenate([cos1, cos1], axis=1)
    lane = lax.broadcasted_iota(jnp.int32, (tr, W), 1)
    first_half = (lane % HD) < (HD // 2)
    sin = jnp.concatenate([sin1, sin1], axis=1)
    sin = jnp.where(first_half, -sin, sin)

    def norm_rope(x, gain):
        y = x * lax.rsqrt(_group_meansq(x, gmat) + EPS) * gain
        return _rope(y, cos, sin, first_half)

    qn = qn_ref[...]
    for s in range(NSA_HEADS * HD // W):
        xq = q_ref[0, :, s * W:(s + 1) * W].astype(F32)
        qr_ref[0, :, s * W:(s + 1) * W] = (norm_rope(xq, qn) * (HD ** -0.5 * LOG2E)).astype(qr_ref.dtype)

    toks = (norm_rope(kvc_ref[0, :, :W].astype(F32), kn_ref[0:1, :]), kvc_ref[0, :, W:].astype(F32))
    for j, dst in enumerate((kc_ref, vc_ref)):
        for h in range(NSA_KV_HEADS):
            tok_ref[j, h] = toks[j][:, h * HD:(h + 1) * HD]
            cols = [tok_ref[j, h, pl.ds(l, tr // CMP_STRIDE, stride=CMP_STRIDE), :] for l in range(CMP_STRIDE)]
            dst[0, h] = jnp.concatenate(cols, axis=1).astype(dst.dtype)

    ks = norm_rope(kvs_ref[0, :, :W].astype(F32), kn_ref[1:2, :]).astype(BF16)
    kw = norm_rope(kvw_ref[0, :, :W].astype(F32), kn_ref[2:3, :]).astype(BF16)
    row_blk = (pl.program_id(1) * tr + lax.broadcasted_iota(jnp.int32, (tr, SLC_LANES), 0)) // SLC_BLOCK
    onehot = (row_blk == lax.broadcasted_iota(jnp.int32, (tr, SLC_LANES), 1)).astype(BF16)
    vst = kvs_ref[0, :, W:].astype(F32).T
    for h in range(NSA_KV_HEADS):
        ks_ref[0, h] = jnp.concatenate([onehot, ks[:, h * HD:(h + 1) * HD], kw[:, h * HD:(h + 1) * HD]], axis=1)
        for j in range(tr // SLC_CHUNK):
            vst_ref[0, h, j, :HD] = vst[h * HD:(h + 1) * HD, j * SLC_CHUNK:(j + 1) * SLC_CHUNK].astype(BF16)
            vst_ref[0, h, j, HD:] = jnp.ones((VT_ROWS - HD, SLC_CHUNK), BF16)

    vwt = kvw_ref[0, :, W:].astype(F32).T
    for h in range(NSA_KV_HEADS):
        for j in range(tr // WIN_ALIGN):
            vwt_ref[0, h, j, :HD] = vwt[h * HD:(h + 1) * HD, j * WIN_ALIGN:(j + 1) * WIN_ALIGN].astype(BF16)
            vwt_ref[0, h, j, HD:] = jnp.ones((VT_ROWS - HD, WIN_ALIGN), BF16)


def _nsa_prep(proj3, pos3, inv, qn, kn, gmat, tr=512):
    B, S, _ = proj3.shape
    Hk, HD = NSA_KV_HEADS, NSA_HD
    hm = lambda w, dt: jax.ShapeDtypeStruct((B, Hk, S, w), dt)
    hspec = lambda w: pl.BlockSpec((1, Hk, tr, w), lambda b, r: (b, 0, r, 0))
    ctok = jax.ShapeDtypeStruct((B, Hk, S // CMP_STRIDE, CMP_STRIDE * HD), BF16)
    cspec = pl.BlockSpec((1, Hk, tr // CMP_STRIDE, CMP_STRIDE * HD), lambda b, r: (b, 0, r, 0))
    tspec = lambda c, rows=HD: pl.BlockSpec((1, Hk, tr // c, rows, c), lambda b, r: (b, 0, r, 0, 0))
    return pl.pallas_call(
        _nsa_prep_kernel,
        grid=(B, S // tr),
        in_specs=[
            pl.BlockSpec((1, tr, 1024), lambda b, r: (b, r, C_NQ // 1024)),
            pl.BlockSpec((1, tr, 512), lambda b, r: (b, r, C_NKV // 512)),
            pl.BlockSpec((1, tr, 512), lambda b, r: (b, r, C_NKV // 512 + 1)),
            pl.BlockSpec((1, tr, 512), lambda b, r: (b, r, C_NKV // 512 + 2)),
            pl.BlockSpec((1, tr, 1), lambda b, r: (b, r, 0)),
            pl.BlockSpec((1, 128), lambda b, r: (0, 0)),
            pl.BlockSpec((1, 256), lambda b, r: (0, 0)),
            pl.BlockSpec((3, 256), lambda b, r: (0, 0)),
            pl.BlockSpec((256, 256), lambda b, r: (0, 0)),
        ],
        out_specs=[pl.BlockSpec((1, tr, 1024), lambda b, r: (b, r, 0)), cspec, cspec, hspec(KAUG),
                   tspec(SLC_CHUNK, VT_ROWS), tspec(WIN_ALIGN, VT_ROWS)],
        out_shape=[jax.ShapeDtypeStruct((B, S, 1024), BF16), ctok, ctok, hm(KAUG, BF16),
                   jax.ShapeDtypeStruct((B, Hk, S // SLC_CHUNK, VT_ROWS, SLC_CHUNK), BF16),
                   jax.ShapeDtypeStruct((B, Hk, S // WIN_ALIGN, VT_ROWS, WIN_ALIGN), BF16)],
        scratch_shapes=[pltpu.VMEM((2, Hk, tr, HD), F32)],
        compiler_params=_cp(("parallel", "parallel")),
        name="nsa_prep",
    )(proj3, proj3, proj3, proj3, pos3, inv, qn, kn, gmat)


def _compress_kernel(t_ref, pe_ref, w1a_ref, w1b_ref, w2_ref, *rest, transposed):
    tail_ref, o_ref = rest if transposed else (None,) + rest
    t = t_ref[0, 0]
    w1a, w1b = w1a_ref[...], w1b_ref[...]
    u = jnp.dot(t, w1a, preferred_element_type=F32)
    v = jnp.dot(t, w1b, preferred_element_type=F32)
    pe = pe_ref[...].astype(BF16)
    c = (jnp.dot(pe[0:8], w1a, preferred_element_type=F32) + jnp.dot(pe[8:16], w1b, preferred_element_type=F32))[0:1]
    n = v.shape[0]
    h = u + pltpu.roll(v, n - 1, 0) + c
    h = jax.nn.gelu(h).astype(BF16)
    if transposed:
        hd = w2_ref.shape[0]
        o_ref[0, 0, :hd] = lax.dot_general(w2_ref[...], h, NT, preferred_element_type=F32).astype(o_ref.dtype)
        o_ref[0, 0, hd:] = tail_ref[...]
    else:
        o_ref[0, 0] = jnp.dot(h, w2_ref[...], preferred_element_type=F32).astype(o_ref.dtype)


def _compress(tok, pe2, w1a, w1b, w2, tail=None):
    B, Hk, n, _ = tok.shape
    HD = NSA_HD
    t2 = tok
    transposed = tail is not None
    oshape = (HD + tail.shape[0], n) if transposed else (n, HD)
    w2 = w2.T if transposed else w2
    extra = ([tail], [pl.BlockSpec(tail.shape, lambda b, h: (0, 0))]) if transposed else ([], [])
    return pl.pallas_call(
        functools.partial(_compress_kernel, transposed=transposed),
        grid=(B, Hk),
        in_specs=[
            pl.BlockSpec((1, 1, n, CMP_STRIDE * HD), lambda b, h: (b, h, 0, 0)),
            pl.BlockSpec((16, CMP_STRIDE * HD), lambda b, h: (0, 0)),
            pl.BlockSpec((CMP_STRIDE * HD, CMP_HIDDEN), lambda b, h: (0, 0)),
            pl.BlockSpec((CMP_STRIDE * HD, CMP_HIDDEN), lambda b, h: (0, 0)),
            pl.BlockSpec(w2.shape, lambda b, h: (0, 0)),
        ] + extra[1],
        out_specs=pl.BlockSpec((1, 1) + oshape, lambda b, h: (b, h, 0, 0)),
        out_shape=jax.ShapeDtypeStruct((B, Hk) + oshape, BF16),
        compiler_params=_cp(("parallel", "parallel")),
        name="compress_v" if transposed else "compress_k",
    )(t2, pe2, w1a, w1b, w2, *extra[0])


def _nsa_attn_kernel(q_ref, kc_ref, vct_ref, ks_ref, vst_ref, vwt_ref, gt_ref, gate_ref, tri_ref, cmask_ref, wmask_ref,
                     o_ref,
                     s_ref, e_ref):
    G, HD, QB, NH, NB = NSA_GROUP, NSA_HD, SLC_BLOCK, NSA_STEP_HEADS, NSA_STEP_BLOCKS
    R = NB * G * QB
    heads = range(NH)
    hk0 = pl.program_id(1) * NH
    step = pl.program_id(2)
    t0 = step * (NB * QB)
    q = [jnp.concatenate([q_ref[0, a * QB:(a + 1) * QB, (h * G + g) * HD:(h * G + g + 1) * HD]
                          for a in range(NB) for g in range(G)], axis=0)
         for h in heads]

    nc = kc_ref.shape[2]
    cmask = cmask_ref[pl.ds(pl.multiple_of(nc - t0 // CMP_STRIDE, CMP_STRIDE), nc), :]
    for h in heads:
        sm = lax.dot_general(kc_ref[0, h], q[h], NT, preferred_element_type=F32) + cmask
        m = jnp.maximum(jnp.max(sm, axis=0, keepdims=True), 0.1 * NEG)
        e_ref[h] = jnp.exp2(sm - m).astype(BF16)
    QL = max(NB * QB, 2 * QB)
    blk = lax.broadcasted_iota(jnp.int32, (SLC_LANES, QL), 0)
    qlane = lax.broadcasted_iota(jnp.int32, (1, QL), 1)
    cur = step * NB + (qlane // QB) % NB
    forced = (blk == 0) | (blk == cur) | (blk == cur - 1)
    o_cmp, score = [], []
    for h in heads:
        oc = jnp.dot(vct_ref[0, h], e_ref[h], preferred_element_type=F32)
        den = oc[HD:HD + 1]
        rden = 1.0 / jnp.where(den > 0.0, den, 1.0)
        o_cmp.append(oc[:HD] * rden)
        impf = oc[VT_ROWS:] * rden
        parts = []
        for a in range(NB):
            pa = impf[:, a * G * QB:(a + 1) * G * QB]
            p2 = pa[:, :2 * QB] + pa[:, 2 * QB:]
            parts.append(p2 + pltpu.roll(p2, QB, 1))
        first_half = qlane[:, :2 * QB] < QB
        imp = parts[0] if NB == 1 else jnp.concatenate(
            [jnp.where(first_half, parts[a], parts[a + 1]) for a in range(0, NB, 2)], axis=1)
        score.append(jnp.where(forced, -jnp.inf, jnp.where(blk <= cur, imp, NEG)))

    wc = jnp.maximum(t0 - WINDOW, 0) // WIN_ALIGN
    w0 = pl.multiple_of(wc * WIN_ALIGN, WIN_ALIGN)
    wmask = wmask_ref[pl.ds(pl.multiple_of(jnp.maximum(WINDOW - t0, 0), NB * QB), WIN_KEYS), :]
    o_win = []
    for h in heads:
        q_win = jnp.concatenate([jnp.zeros((R, SLC_LANES + HD), BF16), q[h]], axis=1)
        sw = lax.dot_general(ks_ref[0, h, pl.ds(w0, WIN_KEYS), :], q_win, NT, preferred_element_type=F32)
        sw = sw + wmask
        pw = jnp.exp2(sw - jnp.max(sw, axis=0, keepdims=True))
        vv = jnp.concatenate([vwt_ref[0, h, wc + j] for j in range(WIN_KEYS // WIN_ALIGN)], axis=1)
        ow = jnp.dot(vv, pw.astype(BF16), preferred_element_type=F32)
        o_win.append(ow[:HD] * (1.0 / ow[HD:HD + 1]))

    few = cur < SLC_TOPN
    causal = blk <= cur
    blkf = blk.astype(F32)

    def pick(sc):
        mx = jnp.max(sc, axis=0, keepdims=True)
        first = jnp.min(jnp.where(sc == mx, blkf, float(SLC_LANES)), axis=0, keepdims=True)
        return jnp.where(blkf == first, -jnp.inf, sc)

    for _ in range(SLC_TOPN - 3):
        score = [pick(sc) for sc in score]
    q_aug = []
    for h in heads:
        chosen = forced | ((few | (score[h] == -jnp.inf)) & causal)
        bias_t = jnp.where(chosen, 0.0, NEG)
        bias = bias_t.T.astype(BF16)
        rows = jnp.concatenate([bias[a * QB:(a + 1) * QB] for a in range(NB) for g in range(G)], axis=0)
        q_aug.append(jnp.concatenate([rows, q[h], jnp.zeros((R, HD), BF16)], axis=1))

    def scores(c, slot):
        k0 = pl.multiple_of(c * SLC_CHUNK, SLC_CHUNK)
        for h in heads:
            s_ref[slot, h] = lax.dot_general(ks_ref[0, h, pl.ds(k0, SLC_CHUNK), :], q_aug[h], NT, preferred_element_type=F32)

    def absorb(c, slot, carry):
        out = []
        for h in heads:
            m_i, acc = carry[h]
            sc = s_ref[slot, h]
            m_new = jnp.maximum(m_i, jnp.max(sc, axis=0, keepdims=True))
            pp = jnp.exp2(sc - m_new).astype(BF16)
            acc = jnp.exp2(m_i - m_new) * acc + jnp.dot(vst_ref[0, h, c], pp, preferred_element_type=F32)
            out.append((m_new, acc))
        return tuple(out)

    diag = t0 // SLC_CHUNK
    n_pairs = diag // 2
    scores(0, 0)

    def pair(j, carry):
        scores(2 * j + 1, 1)
        carry = absorb(2 * j, 0, carry)
        scores(2 * j + 2, 0)
        return absorb(2 * j + 1, 1, carry)

    init = tuple((jnp.full((1, R), NEG, F32), jnp.zeros((VT_ROWS, R), F32)) for h in heads)
    carry = lax.fori_loop(0, n_pairs, pair, init)
    scores(2 * n_pairs + 1, 1)
    r0 = pl.multiple_of(t0 - diag * SLC_CHUNK, NB * QB)
    tri = tri_ref[...]
    for h in heads:
        s_ref[diag % 2, h, pl.ds(r0, NB * QB), :] += tri
    carry = absorb(2 * n_pairs, 0, carry)
    carry = absorb(2 * n_pairs + 1, 1, carry)

    sig = _sigmoid(gt_ref[0])
    glane = lax.broadcasted_iota(jnp.int32, (QB, 128), 1)
    outs = [[] for a in range(NB)]
    for h in heads:
        acc_s = carry[h][1]
        oc, os_, ow = o_cmp[h].T, (acc_s[:HD] * (1.0 / acc_s[HD:HD + 1])).T, o_win[h].T
        for a in range(NB):
            sig_a = sig[a * QB:(a + 1) * QB]
            for g in range(G):
                base = ((hk0 + h) * G + g) * N_NSA_BRANCH
                gc, gs, gw = [jnp.sum(jnp.where(glane == base + b, sig_a, 0.0), axis=-1, keepdims=True) for b in range(3)]
                r = slice((a * G + g) * QB, (a * G + g + 1) * QB)
                outs[a].append(gc * oc[r] + gs * os_[r] + gw * ow[r])
    o = jnp.concatenate([jnp.concatenate(oa, axis=1) for oa in outs], axis=0)
    o_ref[0] = (o * _silu(gate_ref[0].astype(F32))).astype(o_ref.dtype)


def _nsa_attn(qr, kc, vct, ks, vst, vwt, proj3, small3):
    B, S, _ = qr.shape
    Hk, HD, NH = NSA_KV_HEADS, NSA_HD, NSA_STEP_HEADS
    TQ = NSA_STEP_BLOCKS * SLC_BLOCK
    r = np.arange(TQ)[:, None]
    ln = np.arange(NSA_GROUP * TQ)[None, :]
    a, qi = ln // (NSA_GROUP * SLC_BLOCK), ln % SLC_BLOCK
    tri = jnp.asarray(np.where((r // SLC_BLOCK != a) | (r % SLC_BLOCK <= qi), 0.0, NEG), F32)
    nc = kc.shape[2]
    tl = a * SLC_BLOCK + qi
    u = np.arange(2 * nc)[:, None]
    cmask = jnp.asarray(np.where(u - nc <= (tl - (CMP_LEN - 1)) // CMP_STRIDE, 0.0, NEG), F32)
    u = np.arange(WINDOW + WIN_KEYS)[:, None]
    wmask = jnp.asarray(np.where((u > tl) & (u <= WINDOW + tl), 0.0, NEG), F32)
    const = lambda c: pl.BlockSpec(c.shape, lambda b, h, t: (0, 0), pipeline_mode=pl.Buffered(1))
    W = NH * NSA_GROUP * HD
    full = lambda a: pl.BlockSpec((1, NH) + a.shape[2:], lambda b, h, t: (b, h) + (0,) * (a.ndim - 2),
                                  pipeline_mode=pl.Buffered(1))
    return pl.pallas_call(
        _nsa_attn_kernel,
        grid=(B, Hk // NH, S // TQ),
        in_specs=[
            pl.BlockSpec((1, TQ, W), lambda b, h, t: (b, t, h)),
            full(kc), full(vct), full(ks), full(vst), full(vwt),
            pl.BlockSpec((1, TQ, 128), lambda b, h, t: (b, t, CS_NSG // 128)),
            pl.BlockSpec((1, TQ, W), lambda b, h, t: (b, t, C_NGATE // W + h)),
            const(tri), const(cmask), const(wmask),
        ],
        out_specs=pl.BlockSpec((1, TQ, W), lambda b, h, t: (b, t, h)),
        out_shape=jax.ShapeDtypeStruct((B, S, NSA_HEADS * HD), BF16),
        scratch_shapes=[pltpu.VMEM((2, NH, SLC_CHUNK, NSA_GROUP * TQ), F32), pltpu.VMEM((NH, nc, NSA_GROUP * TQ), BF16)],
        compiler_params=_cp(("parallel", "parallel", "arbitrary")),
        name="nsa_attn",
    )(qr, kc, vct, ks, vst, vwt, small3, proj3, tri, cmask, wmask)


def _head_rms(x, gain, hd):
    outs = []
    for h in range(x.shape[-1] // hd):
        xh = x[:, h * hd:(h + 1) * hd]
        ms = jnp.mean(xh * xh, axis=-1, keepdims=True)
        outs.append(xh * lax.rsqrt(ms + EPS) * gain)
    return jnp.concatenate(outs, axis=1)


def _mem_prep_kernel(mem_ref, g_ref, w_ref, kn_ref, mk_ref, mv_ref):
    x = mem_ref[0]
    ms = jnp.mean(x * x, axis=-1, keepdims=True)
    xn = (x * lax.rsqrt(ms + EPS) * g_ref[...]).astype(BF16)
    kv = jnp.dot(xn, w_ref[...], preferred_element_type=F32)
    W = MEM_HEADS * MEM_HD
    mk_ref[0] = _head_rms(kv[:, :W], kn_ref[...], MEM_HD).astype(mk_ref.dtype)
    mv_ref[0] = kv[:, W:].astype(mv_ref.dtype)


def _mem_prep(mem, gain, w_kv, kn):
    B, N, D = mem.shape
    W = MEM_HEADS * MEM_HD
    return pl.pallas_call(
        _mem_prep_kernel,
        grid=(B,),
        in_specs=[
            pl.BlockSpec((1, N, D), lambda b: (b, 0, 0)),
            pl.BlockSpec((1, D), lambda b: (0, 0)),
            pl.BlockSpec((D, 2 * W), lambda b: (0, 0)),
            pl.BlockSpec((1, MEM_HD), lambda b: (0, 0)),
        ],
        out_specs=[pl.BlockSpec((1, N, W), lambda b: (b, 0, 0))] * 2,
        out_shape=[jax.ShapeDtypeStruct((B, N, W), BF16)] * 2,
        compiler_params=_cp(("parallel",)),
        name="mem_prep",
    )(mem, gain, w_kv, kn)


def _final_kernel(x_ref, ya_ref, yb_ref, mq_ref, mg_ref, mr_ref, mk_ref, mv_ref, qn_ref, wb_ref, wo_ref, o_ref):
    mq = _head_rms(mq_ref[...].astype(F32), qn_ref[...], MEM_HD)
    mk = mk_ref[0]
    mv = mv_ref[0]
    heads = []
    for h in range(MEM_HEADS):
        sl = slice(h * MEM_HD, (h + 1) * MEM_HD)
        s = lax.dot_general(mq[:, sl].astype(BF16), mk[:, sl], NT, preferred_element_type=F32) * (MEM_HD ** -0.5)
        m = jnp.max(s, axis=-1, keepdims=True)
        p = jnp.exp(s - m)
        o = jnp.dot(p.astype(BF16), mv[:, sl], preferred_element_type=F32) / jnp.sum(p, axis=-1, keepdims=True)
        heads.append(o)
    ym = jnp.concatenate(heads, axis=1) * _silu(mg_ref[...].astype(F32))

    mixed = None
    for c, y in enumerate((ya_ref[...], yb_ref[...], ym.astype(BF16))):
        z = jnp.dot(y, wb_ref[c], preferred_element_type=F32)
        term = _sigmoid(mr_ref[:, c * D_MODEL:(c + 1) * D_MODEL].astype(F32)) * z
        mixed = term if mixed is None else mixed + term
    o_ref[...] = x_ref[...] + jnp.dot(mixed.astype(BF16), wo_ref[...], preferred_element_type=F32)


def _final(x2, ya2, yb2, proj2, mk, mv, qn, wb, wo, S, tr=512):
    M, D = x2.shape
    N = mk.shape[1]
    nb = S // tr
    row = lambda c: pl.BlockSpec((tr, D), lambda i: (i, c))
    return pl.pallas_call(
        _final_kernel,
        grid=(M // tr,),
        in_specs=[
            row(0), row(0), row(0),
            row(C_MQ // D), row(C_MG // D),
            pl.BlockSpec((tr, 3 * D), lambda i: (i, C_MERGE // (3 * D))),
            pl.BlockSpec((1, N, D), lambda i: (i // nb, 0, 0)),
            pl.BlockSpec((1, N, D), lambda i: (i // nb, 0, 0)),
            pl.BlockSpec((1, MEM_HD), lambda i: (0, 0)),
            pl.BlockSpec((3, D, D), lambda i: (0, 0, 0), pipeline_mode=pl.Buffered(1)),
            pl.BlockSpec((D, D), lambda i: (0, 0), pipeline_mode=pl.Buffered(1)),
        ],
        out_specs=pl.BlockSpec((tr, D), lambda i: (i, 0)),
        out_shape=jax.ShapeDtypeStruct((M, D), F32),
        compiler_params=_cp(("parallel",)),
        name="final",
    )(x2, ya2, yb2, proj2, proj2, proj2, mk, mv, qn, wb, wo)


def _regroup_kernel(w_ref, o_ref, s_ref, *, o, runs):
    dst = 0
    for a, b in runs:
        o_ref[dst:dst + o[b] - o[a], :] = w_ref[0, o[a]:o[b], :].astype(o_ref.dtype)
        dst += o[b] - o[a]
    o_ref[dst:, :] = jnp.zeros((o_ref.shape[0] - dst, o_ref.shape[1]), o_ref.dtype)
    for sec, at in ((3, CS_LR), (7, CS_NSG)):
        wd = o[sec + 1] - o[sec]
        s_ref[at:at + wd, :] = w_ref[0, o[sec]:o[sec + 1], :].astype(s_ref.dtype)
        s_ref[at + wd:at + 128, :] = jnp.zeros((128 - wd, s_ref.shape[1]), s_ref.dtype)


def _regroup(w_in_t, layer, tc=128):
    _, C, D = w_in_t.shape
    o = [int(v) for v in np.cumsum([0, 512, 512, 1024, 16, 1024, 1024, 1536, 48, 1024, 1024, 1024, 3072])]
    assert o[-1] == C
    runs = ((2, 3), (4, 6), (8, 12), (6, 7), (0, 2))
    return pl.pallas_call(
        functools.partial(_regroup_kernel, o=o, runs=runs),
        grid=(D // tc,),
        in_specs=[pl.BlockSpec((1, C, tc), lambda i: (layer, 0, i))],
        out_specs=[pl.BlockSpec((NP, tc), lambda i: (0, i)), pl.BlockSpec((NP_SMALL, tc), lambda i: (0, i))],
        out_shape=[jax.ShapeDtypeStruct((NP, D), BF16), jax.ShapeDtypeStruct((NP_SMALL, D), BF16)],
        compiler_params=_cp(("parallel",)),
        name="w_regroup",
    )(w_in_t)


def _overlap_matrix_t(S):
    n_cmp = (S - CMP_LEN) // CMP_STRIDE + 1
    n_slc = S // SLC_BLOCK
    cs = np.arange(n_cmp)[:, None] * CMP_STRIDE
    ss = np.arange(n_slc)[None, :] * SLC_BLOCK
    ov = np.clip(np.minimum(cs + CMP_LEN, ss + SLC_BLOCK) - np.maximum(cs, ss), 0, None) / CMP_LEN
    out = np.zeros((SLC_LANES, S // CMP_STRIDE), np.float32)
    out[:n_slc, :n_cmp] = ov.T
    return jnp.asarray(out, BF16)


def _layer(x, mem, positions, norm_gain, mem_norm_gain, w_in, w_gla_alpha, b_gla_alpha, gla_out_norm,
           nsa_q_norm, nsa_k_norm, pe_cmp_k, pe_cmp_v, w_cmp_k1, w_cmp_k2, w_cmp_v1, w_cmp_v2,
           w_mem_kv, mem_q_norm, mem_k_norm, w_branch, w_out):
    B, S, D = x.shape
    assert D == D_MODEL and S % (2 * SLC_CHUNK) == 0 and S >= WIN_KEYS and S // SLC_BLOCK <= SLC_LANES
    assert SLC_TOPN % NSA_STEP_BLOCKS == 0 and SLC_CHUNK % (NSA_STEP_BLOCKS * SLC_BLOCK) == 0

    w_all, w_small = w_in

    x2 = x.reshape(B * S, D)
    proj2, small2 = _proj(x2, norm_gain.reshape(1, D), w_all, w_small)
    proj3 = proj2.reshape(B, S, NP)
    small3 = small2.reshape(B, S, NP_SMALL)

    wa = jnp.pad(w_gla_alpha, ((0, 128 - GLA_RANK), (0, 0)))
    ya = _gla(proj3, small3, wa, b_gla_alpha.reshape(1, -1), gla_out_norm.reshape(1, -1))

    half = NSA_HD // 2
    inv = ROPE_THETA ** (-jnp.arange(half, dtype=F32) / half)
    inv = jnp.tile(inv, 128 // half).reshape(1, 128)
    qn = jnp.tile(nsa_q_norm, NSA_KV_HEADS).reshape(1, -1)
    kn = jnp.tile(nsa_k_norm, (1, NSA_KV_HEADS))
    gid = np.arange(256) // NSA_HD
    gmat = jnp.asarray(gid[:, None] == gid[None, :], BF16)
    qr, kc_tok, vc_tok, ks, vst, vwt = _nsa_prep(proj3, positions.reshape(B, S, 1), inv, qn, kn, gmat)

    def cmp_args(pe, w1, w2):
        pe2 = pe.reshape(2, CMP_STRIDE * NSA_HD)
        pe16 = jnp.concatenate([jnp.broadcast_to(pe2[0:1], (8, pe2.shape[1])), jnp.broadcast_to(pe2[1:2], (8, pe2.shape[1]))], 0)
        w1f = w1.reshape(CMP_LEN * NSA_HD, CMP_HIDDEN).astype(BF16)
        return pe16, w1f[:CMP_STRIDE * NSA_HD], w1f[CMP_STRIDE * NSA_HD:], w2.astype(BF16)

    kc = _compress(kc_tok, *cmp_args(pe_cmp_k, w_cmp_k1, w_cmp_k2))
    tail = jnp.concatenate([jnp.ones((VT_ROWS - NSA_HD, S // CMP_STRIDE), BF16), _overlap_matrix_t(S)], axis=0)
    vct = _compress(vc_tok, *cmp_args(pe_cmp_v, w_cmp_v1, w_cmp_v2), tail=tail)
    yb = _nsa_attn(qr, kc, vct, ks, vst, vwt, proj3, small3)

    mk, mv = _mem_prep(mem, mem_norm_gain.reshape(1, D), w_mem_kv.astype(BF16), mem_k_norm.reshape(1, -1))
    out = _final(x2, ya.reshape(B * S, D), yb.reshape(B * S, D), proj2, mk, mv, mem_q_norm.reshape(1, -1),
                 w_branch.astype(BF16), w_out.astype(BF16), S)
    return out.reshape(B, S, D)


def kernel(x, mem, positions, norm_gain, mem_norm_gain, w_in, w_gla_alpha, b_gla_alpha, gla_out_norm, nsa_q_norm, nsa_k_norm, pe_cmp_k, pe_cmp_v, w_cmp_k1, w_cmp_k2, w_cmp_v1, w_cmp_v2, w_mem_kv, mem_q_norm, mem_k_norm, w_branch, w_out):
    h = x
    for l in range(norm_gain.shape[0]):
        h = _layer(h, mem, positions, norm_gain[l], mem_norm_gain[l], _regroup(jnp.swapaxes(w_in, 1, 2), l),
                   w_gla_alpha[l], b_gla_alpha[l],
                   gla_out_norm[l], nsa_q_norm[l], nsa_k_norm[l], pe_cmp_k[l], pe_cmp_v[l], w_cmp_k1[l],
                   w_cmp_k2[l], w_cmp_v1[l], w_cmp_v2[l], w_mem_kv[l], mem_q_norm[l], mem_k_norm[l],
                   w_branch[l], w_out[l])
    return h
```

```python
import functools

import numpy as np
import jax
import jax.numpy as jnp
from jax import lax
from jax.experimental import pallas as pl
from jax.experimental.pallas import tpu as pltpu

F32 = jnp.float32
BF16 = jnp.bfloat16

D_MODEL = 1024
ROPE_THETA = 10000.0
EPS = 1e-6
NEG = -1e30

GLA_HEADS = 4
GLA_DK = 128
GLA_DV = 256
GLA_RANK = 16
GLA_TAU = 16.0
GLA_CHUNK = 64
GLA_SUB = 16

NSA_HEADS = 16
NSA_KV_HEADS = 4
NSA_GROUP = 4
NSA_HD = 64
CMP_LEN = 32
CMP_STRIDE = 16
CMP_HIDDEN = 256
SLC_BLOCK = 64
SLC_TOPN = 16
WINDOW = 512
N_NSA_BRANCH = 3
SLC_LANES = 128
SLC_CHUNK = 512
VT_ROWS = NSA_HD + 16
LOG2E = 1.4426950408889634
WIN_ALIGN = 128
NSA_STEP_HEADS = 2
NSA_STEP_BLOCKS = 4
WIN_KEYS = WINDOW + max(NSA_STEP_BLOCKS * SLC_BLOCK, WIN_ALIGN)
KAUG = SLC_LANES + 2 * NSA_HD

MEM_HEADS = 4
MEM_HD = 256

C_GV, C_GG, C_NQ, C_NGATE, C_MQ, C_MG, C_MERGE = 0, 1024, 2048, 3072, 4096, 5120, 6144
C_NKV, C_GQ, C_GK = 9216, 10752, 11264
NP = 12288
CS_LR, CS_NSG, NP_SMALL = 0, 128, 256

VMEM_LIMIT = 48 * 1024 * 1024

NT = (((1,), (1,)), ((), ()))
TN = (((0,), (0,)), ((), ()))


def _cp(sem):
    return pltpu.CompilerParams(dimension_semantics=sem, vmem_limit_bytes=VMEM_LIMIT)


def _silu(x):
    return x * (1.0 / (1.0 + jnp.exp(-x)))


def _sigmoid(x):
    return 1.0 / (1.0 + jnp.exp(-x))


def _proj_kernel(x_ref, g_ref, w_ref, ws_ref, o_ref, os_ref, xn_ref):
    @pl.when(pl.program_id(1) == 0)
    def _():
        x = x_ref[...]
        ms = jnp.mean(x * x, axis=-1, keepdims=True)
        xn = (x * lax.rsqrt(ms + EPS) * g_ref[...]).astype(BF16)
        xn_ref[...] = xn
        os_ref[...] = lax.dot_general(xn, ws_ref[...], NT, preferred_element_type=F32)

    o_ref[...] = lax.dot_general(xn_ref[...], w_ref[...], NT, preferred_element_type=F32).astype(o_ref.dtype)


def _proj(x2, gain, w_all, w_small, tm=1024, tn=3072):
    M = x2.shape[0]
    ns = w_small.shape[0]
    return pl.pallas_call(
        _proj_kernel,
        grid=(M // tm, NP // tn),
        in_specs=[
            pl.BlockSpec((tm, D_MODEL), lambda i, j: (i, 0)),
            pl.BlockSpec((1, D_MODEL), lambda i, j: (0, 0)),
            pl.BlockSpec((tn, D_MODEL), lambda i, j: (j, 0)),
            pl.BlockSpec((ns, D_MODEL), lambda i, j: (0, 0)),
        ],
        out_specs=[pl.BlockSpec((tm, tn), lambda i, j: (i, j)), pl.BlockSpec((tm, ns), lambda i, j: (i, 0))],
        out_shape=[jax.ShapeDtypeStruct((M, NP), BF16), jax.ShapeDtypeStruct((M, ns), F32)],
        scratch_shapes=[pltpu.VMEM((tm, D_MODEL), BF16)],
        compiler_params=_cp(("parallel", "arbitrary")),
        name="proj",
    )(x2, gain, w_all, w_small)


def _split2(x):
    hi = x.astype(BF16)
    return hi, (x - hi.astype(F32)).astype(BF16)


def _gla_kernel(q_ref, k_ref, v_ref, gate_ref, lr_ref, wa_ref, ba_ref, gn_ref, o_ref, st_ref, g_ref, kf_ref, *, n_chunks):
    C, SB, H, DK, DV = GLA_CHUNK, GLA_SUB, GLA_HEADS, GLA_DK, GLA_DV

    @pl.when(pl.program_id(1) == 0)
    def _():
        st_ref[...] = jnp.zeros_like(st_ref)

    gn = gn_ref[...]
    ri = lax.broadcasted_iota(jnp.int32, (C, C), 0)
    ci = lax.broadcasted_iota(jnp.int32, (C, C), 1)
    tri = (ri >= ci).astype(BF16)
    si = lax.broadcasted_iota(jnp.int32, (SB, SB), 0)
    sj = lax.broadcasted_iota(jnp.int32, (SB, SB), 1)
    sub_causal = si >= sj

    lr_hi, lr_lo = _split2(lr_ref[0])
    wa_hi, wa_lo = _split2(wa_ref[...])
    z = (jnp.dot(lr_hi, wa_hi, preferred_element_type=F32) + jnp.dot(lr_hi, wa_lo, preferred_element_type=F32)
         + jnp.dot(lr_lo, wa_hi, preferred_element_type=F32)) + ba_ref[...]
    la = -(jnp.maximum(-z, 0.0) + jnp.log(1.0 + jnp.exp(-jnp.abs(z)))) * (LOG2E / GLA_TAU)
    la1 = la.astype(BF16)
    la2, la3 = _split2(la - la1.astype(F32))
    for c in range(n_chunks):
        r = slice(c * C, (c + 1) * C)
        g_ref[r, :] = (jnp.dot(tri, la1[r], preferred_element_type=F32) + jnp.dot(tri, la2[r], preferred_element_type=F32)
                       + jnp.dot(tri, la3[r], preferred_element_type=F32))

    kf_ref[...] = k_ref[0].astype(F32)

    def chunk(c):
        rows = pl.ds(c * C, C)
        for h in range(H):
            kl = slice(h * DK, (h + 1) * DK)
            vl = slice(h * DV, (h + 1) * DV)
            q = q_ref[0, rows, kl].astype(F32) * (DK ** -0.5)
            k = kf_ref[rows, kl]
            vb = v_ref[0, rows, vl]
            g = g_ref[rows, kl]

            outs = []
            for i in range(C // SB):
                lo = i * SB
                gi = g[lo:lo + SB]
                qi = q[lo:lo + SB]
                ki = k[lo:lo + SB]
                HB = SB // 2
                a_top = jnp.zeros((HB, SB), F32)
                a_bot = jnp.zeros((HB, SB), F32)
                for jj in range(SB):
                    row = slice(lo + jj, lo + jj + 1)
                    k_row = kf_ref.at[rows, kl][row, :]
                    g_row = g_ref.at[rows, kl][row, :]
                    qs, gs = (qi, gi) if jj < HB else (qi[HB:], gi[HB:])
                    w = (qs * k_row) * jnp.exp2(jnp.minimum(gs - g_row, 0.0))
                    col = jnp.sum(w, axis=-1, keepdims=True)
                    if jj < HB:
                        a_top = jnp.where(sj[:HB] == jj, col[:HB], a_top)
                        a_bot = jnp.where(sj[:HB] == jj, col[HB:], a_bot)
                    else:
                        a_bot = jnp.where(sj[:HB] == jj, col, a_bot)
                a_d = jnp.where(sub_causal, jnp.concatenate([a_top, a_bot], axis=0), 0.0)
                o_i = jnp.dot(a_d.astype(BF16), vb[lo:lo + SB], preferred_element_type=F32)
                if i > 0:
                    r = g[lo:lo + 1]
                    qt = qi * jnp.exp2(gi - r)
                    kt = k[:lo] * jnp.exp2(r - g[:lo])
                    a_o = lax.dot_general(qt.astype(BF16), kt.astype(BF16), NT, preferred_element_type=F32)
                    o_i = o_i + jnp.dot(a_o.astype(BF16), vb[:lo], preferred_element_type=F32)
                outs.append(o_i)
            o = jnp.concatenate(outs, axis=0)

            st = st_ref[h]
            qg = q * jnp.exp2(g)
            o = o + lax.dot_general(qg.astype(BF16), st.astype(BF16), NT, preferred_element_type=F32)
            gl = g[C - 1:C]
            kd = k * jnp.exp2(gl - g)
            st_ref[h] = st * jnp.exp2(gl) + lax.dot_general(vb, kd.astype(BF16), TN, preferred_element_type=F32)

            ms = jnp.mean(o * o, axis=-1, keepdims=True)
            y = o * lax.rsqrt(ms + EPS) * gn
            o_ref[0, rows, vl] = (y * _silu(gate_ref[0, rows, vl].astype(F32))).astype(o_ref.dtype)

    for c in range(n_chunks):
        chunk(c)


def _gla(proj3, small3, wa, ba, gn, tr=1024):
    B, S, _ = proj3.shape
    H, DK, DV = GLA_HEADS, GLA_DK, GLA_DV
    kern = functools.partial(_gla_kernel, n_chunks=tr // GLA_CHUNK)
    return pl.pallas_call(
        kern,
        grid=(B, S // tr),
        in_specs=[
            pl.BlockSpec((1, tr, H * DK), lambda b, r: (b, r, C_GQ // (H * DK))),
            pl.BlockSpec((1, tr, H * DK), lambda b, r: (b, r, C_GK // (H * DK))),
            pl.BlockSpec((1, tr, H * DV), lambda b, r: (b, r, C_GV // (H * DV))),
            pl.BlockSpec((1, tr, H * DV), lambda b, r: (b, r, C_GG // (H * DV))),
            pl.BlockSpec((1, tr, 128), lambda b, r: (b, r, CS_LR // 128)),
            pl.BlockSpec((128, H * DK), lambda b, r: (0, 0)),
            pl.BlockSpec((1, H * DK), lambda b, r: (0, 0)),
            pl.BlockSpec((1, DV), lambda b, r: (0, 0)),
        ],
        out_specs=pl.BlockSpec((1, tr, H * DV), lambda b, r: (b, r, 0)),
        out_shape=jax.ShapeDtypeStruct((B, S, H * DV), BF16),
        scratch_shapes=[pltpu.VMEM((H, DV, DK), F32), pltpu.VMEM((tr, H * DK), F32), pltpu.VMEM((tr, H * DK), F32)],
        compiler_params=_cp(("parallel", "arbitrary")),
        name="gla",
    )(proj3, proj3, proj3, proj3, small3, wa, ba, gn)


def _group_meansq(x, gmat):
    return jnp.dot((x * x).astype(BF16), gmat, preferred_element_type=F32) * (1.0 / NSA_HD)


def _rope(x, cos, sin_signed, first_half):
    w = x.shape[-1]
    rot = jnp.where(first_half, pltpu.roll(x, w - NSA_HD // 2, 1), pltpu.roll(x, NSA_HD // 2, 1))
    return x * cos + rot * sin_signed


def _nsa_prep_kernel(q_ref, kvc_ref, kvs_ref, kvw_ref, pos_ref, inv_ref, qn_ref, kn_ref, gm_ref,
                     qr_ref, kc_ref, vc_ref, ks_ref, vst_ref, vwt_ref, tok_ref):
    tr = q_ref.shape[1]
    W = NSA_KV_HEADS * NSA_HD
    HD = NSA_HD
    gmat = gm_ref[...]
    ang = pos_ref[0].astype(F32) * inv_ref[...]
    cos1, sin1 = jnp.cos(ang), jnp.sin(ang)
    cos = jnp.concatenate([cos1, cos1], axis=1)
    lane = lax.broadcasted_iota(jnp.int32, (tr, W), 1)
    first_half = (lane % HD) < (HD // 2)
    sin = jnp.concatenate([sin1, sin1], axis=1)
    sin = jnp.where(first_half, -sin, sin)

    def norm_rope(x, gain):
        y = x * lax.rsqrt(_group_meansq(x, gmat) + EPS) * gain
        return _rope(y, cos, sin, first_half)

    qn = qn_ref[...]
    for s in range(NSA_HEADS * HD // W):
        xq = q_ref[0, :, s * W:(s + 1) * W].astype(F32)
        qr_ref[0, :, s * W:(s + 1) * W] = (norm_rope(xq, qn) * (HD ** -0.5 * LOG2E)).astype(qr_ref.dtype)

    toks = (norm_rope(kvc_ref[0, :, :W].astype(F32), kn_ref[0:1, :]), kvc_ref[0, :, W:].astype(F32))
    for j, dst in enumerate((kc_ref, vc_ref)):
        for h in range(NSA_KV_HEADS):
            tok_ref[j, h] = toks[j][:, h * HD:(h + 1) * HD]
            cols = [tok_ref[j, h, pl.ds(l, tr // CMP_STRIDE, stride=CMP_STRIDE), :] for l in range(CMP_STRIDE)]
            dst[0, h] = jnp.concatenate(cols, axis=1).astype(dst.dtype)

    ks = norm_rope(kvs_ref[0, :, :W].astype(F32), kn_ref[1:2, :]).astype(BF16)
    kw = norm_rope(kvw_ref[0, :, :W].astype(F32), kn_ref[2:3, :]).astype(BF16)
    row_blk = (pl.program_id(1) * tr + lax.broadcasted_iota(jnp.int32, (tr, SLC_LANES), 0)) // SLC_BLOCK
    onehot = (row_blk == lax.broadcasted_iota(jnp.int32, (tr, SLC_LANES), 1)).astype(BF16)
    vst = kvs_ref[0, :, W:].astype(F32).T
    for h in range(NSA_KV_HEADS):
        ks_ref[0, h] = jnp.concatenate([onehot, ks[:, h * HD:(h + 1) * HD], kw[:, h * HD:(h + 1) * HD]], axis=1)
        for j in range(tr // SLC_CHUNK):
            vst_ref[0, h, j, :HD] = vst[h * HD:(h + 1) * HD, j * SLC_CHUNK:(j + 1) * SLC_CHUNK].astype(BF16)
            vst_ref[0, h, j, HD:] = jnp.ones((VT_ROWS - HD, SLC_CHUNK), BF16)

    vwt = kvw_ref[0, :, W:].astype(F32).T
    for h in range(NSA_KV_HEADS):
        for j in range(tr // WIN_ALIGN):
            vwt_ref[0, h, j, :HD] = vwt[h * HD:(h + 1) * HD, j * WIN_ALIGN:(j + 1) * WIN_ALIGN].astype(BF16)
            vwt_ref[0, h, j, HD:] = jnp.ones((VT_ROWS - HD, WIN_ALIGN), BF16)


def _nsa_prep(proj3, pos3, inv, qn, kn, gmat, tr=512):
    B, S, _ = proj3.shape
    Hk, HD = NSA_KV_HEADS, NSA_HD
    hm = lambda w, dt: jax.ShapeDtypeStruct((B, Hk, S, w), dt)
    hspec = lambda w: pl.BlockSpec((1, Hk, tr, w), lambda b, r: (b, 0, r, 0))
    ctok = jax.ShapeDtypeStruct((B, Hk, S // CMP_STRIDE, CMP_STRIDE * HD), BF16)
    cspec = pl.BlockSpec((1, Hk, tr // CMP_STRIDE, CMP_STRIDE * HD), lambda b, r: (b, 0, r, 0))
    tspec = lambda c, rows=HD: pl.BlockSpec((1, Hk, tr // c, rows, c), lambda b, r: (b, 0, r, 0, 0))
    return pl.pallas_call(
        _nsa_prep_kernel,
        grid=(B, S // tr),
        in_specs=[
            pl.BlockSpec((1, tr, 1024), lambda b, r: (b, r, C_NQ // 1024)),
            pl.BlockSpec((1, tr, 512), lambda b, r: (b, r, C_NKV // 512)),
            pl.BlockSpec((1, tr, 512), lambda b, r: (b, r, C_NKV // 512 + 1)),
            pl.BlockSpec((1, tr, 512), lambda b, r: (b, r, C_NKV // 512 + 2)),
            pl.BlockSpec((1, tr, 1), lambda b, r: (b, r, 0)),
            pl.BlockSpec((1, 128), lambda b, r: (0, 0)),
            pl.BlockSpec((1, 256), lambda b, r: (0, 0)),
            pl.BlockSpec((3, 256), lambda b, r: (0, 0)),
            pl.BlockSpec((256, 256), lambda b, r: (0, 0)),
        ],
        out_specs=[pl.BlockSpec((1, tr, 1024), lambda b, r: (b, r, 0)), cspec, cspec, hspec(KAUG),
                   tspec(SLC_CHUNK, VT_ROWS), tspec(WIN_ALIGN, VT_ROWS)],
        out_shape=[jax.ShapeDtypeStruct((B, S, 1024), BF16), ctok, ctok, hm(KAUG, BF16),
                   jax.ShapeDtypeStruct((B, Hk, S // SLC_CHUNK, VT_ROWS, SLC_CHUNK), BF16),
                   jax.ShapeDtypeStruct((B, Hk, S // WIN_ALIGN, VT_ROWS, WIN_ALIGN), BF16)],
        scratch_shapes=[pltpu.VMEM((2, Hk, tr, HD), F32)],
        compiler_params=_cp(("parallel", "parallel")),
        name="nsa_prep",
    )(proj3, proj3, proj3, proj3, pos3, inv, qn, kn, gmat)


def _compress_kernel(t_ref, pe_ref, w1a_ref, w1b_ref, w2_ref, *rest, transposed):
    tail_ref, o_ref = rest if transposed else (None,) + rest
    t = t_ref[0, 0]
    w1a, w1b = w1a_ref[...], w1b_ref[...]
    u = jnp.dot(t, w1a, preferred_element_type=F32)
    v = jnp.dot(t, w1b, preferred_element_type=F32)
    pe = pe_ref[...].astype(BF16)
    c = (jnp.dot(pe[0:8], w1a, preferred_element_type=F32) + jnp.dot(pe[8:16], w1b, preferred_element_type=F32))[0:1]
    n = v.shape[0]
    h = u + pltpu.roll(v, n - 1, 0) + c
    h = jax.nn.gelu(h).astype(BF16)
    if transposed:
        hd = w2_ref.shape[0]
        o_ref[0, 0, :hd] = lax.dot_general(w2_ref[...], h, NT, preferred_element_type=F32).astype(o_ref.dtype)
        o_ref[0, 0, hd:] = tail_ref[...]
    else:
        o_ref[0, 0] = jnp.dot(h, w2_ref[...], preferred_element_type=F32).astype(o_ref.dtype)


def _compress(tok, pe2, w1a, w1b, w2, tail=None):
    B, Hk, n, _ = tok.shape
    HD = NSA_HD
    t2 = tok
    transposed = tail is not None
    oshape = (HD + tail.shape[0], n) if transposed else (n, HD)
    w2 = w2.T if transposed else w2
    extra = ([tail], [pl.BlockSpec(tail.shape, lambda b, h: (0, 0))]) if transposed else ([], [])
    return pl.pallas_call(
        functools.partial(_compress_kernel, transposed=transposed),
        grid=(B, Hk),
        in_specs=[
            pl.BlockSpec((1, 1, n, CMP_STRIDE * HD), lambda b, h: (b, h, 0, 0)),
            pl.BlockSpec((16, CMP_STRIDE * HD), lambda b, h: (0, 0)),
            pl.BlockSpec((CMP_STRIDE * HD, CMP_HIDDEN), lambda b, h: (0, 0)),
            pl.BlockSpec((CMP_STRIDE * HD, CMP_HIDDEN), lambda b, h: (0, 0)),
            pl.BlockSpec(w2.shape, lambda b, h: (0, 0)),
        ] + extra[1],
        out_specs=pl.BlockSpec((1, 1) + oshape, lambda b, h: (b, h, 0, 0)),
        out_shape=jax.ShapeDtypeStruct((B, Hk) + oshape, BF16),
        compiler_params=_cp(("parallel", "parallel")),
        name="compress_v" if transposed else "compress_k",
    )(t2, pe2, w1a, w1b, w2, *extra[0])


def _nsa_attn_kernel(q_ref, kc_ref, vct_ref, ks_ref, vst_ref, vwt_ref, gt_ref, gate_ref, tri_ref, cmask_ref, wmask_ref,
                     o_ref,
                     s_ref, e_ref):
    G, HD, QB, NH, NB = NSA_GROUP, NSA_HD, SLC_BLOCK, NSA_STEP_HEADS, NSA_STEP_BLOCKS
    R = NB * G * QB
    heads = range(NH)
    hk0 = pl.program_id(1) * NH
    step = pl.program_id(2)
    t0 = step * (NB * QB)
    q = [jnp.concatenate([q_ref[0, a * QB:(a + 1) * QB, (h * G + g) * HD:(h * G + g + 1) * HD]
                          for a in range(NB) for g in range(G)], axis=0)
         for h in heads]

    nc = kc_ref.shape[2]
    cmask = cmask_ref[pl.ds(pl.multiple_of(nc - t0 // CMP_STRIDE, CMP_STRIDE), nc), :]
    for h in heads:
        sm = lax.dot_general(kc_ref[0, h], q[h], NT, preferred_element_type=F32) + cmask
        m = jnp.maximum(jnp.max(sm, axis=0, keepdims=True), 0.1 * NEG)
        e_ref[h] = jnp.exp2(sm - m).astype(BF16)
    QL = max(NB * QB, 2 * QB)
    blk = lax.broadcasted_iota(jnp.int32, (SLC_LANES, QL), 0)
    qlane = lax.broadcasted_iota(jnp.int32, (1, QL), 1)
    cur = step * NB + (qlane // QB) % NB
    forced = (blk == 0) | (blk == cur) | (blk == cur - 1)
    o_cmp, score = [], []
    for h in heads:
        oc = jnp.dot(vct_ref[0, h], e_ref[h], preferred_element_type=F32)
        den = oc[HD:HD + 1]
        rden = 1.0 / jnp.where(den > 0.0, den, 1.0)
        o_cmp.append(oc[:HD] * rden)
        impf = oc[VT_ROWS:] * rden
        parts = []
        for a in range(NB):
            pa = impf[:, a * G * QB:(a + 1) * G * QB]
            p2 = pa[:, :2 * QB] + pa[:, 2 * QB:]
            parts.append(p2 + pltpu.roll(p2, QB, 1))
        first_half = qlane[:, :2 * QB] < QB
        imp = parts[0] if NB == 1 else jnp.concatenate(
            [jnp.where(first_half, parts[a], parts[a + 1]) for a in range(0, NB, 2)], axis=1)
        score.append(jnp.where(forced, -jnp.inf, jnp.where(blk <= cur, imp, NEG)))

    wc = jnp.maximum(t0 - WINDOW, 0) // WIN_ALIGN
    w0 = pl.multiple_of(wc * WIN_ALIGN, WIN_ALIGN)
    wmask = wmask_ref[pl.ds(pl.multiple_of(jnp.maximum(WINDOW - t0, 0), NB * QB), WIN_KEYS), :]
    o_win = []
    for h in heads:
        q_win = jnp.concatenate([jnp.zeros((R, SLC_LANES + HD), BF16), q[h]], axis=1)
        sw = lax.dot_general(ks_ref[0, h, pl.ds(w0, WIN_KEYS), :], q_win, NT, preferred_element_type=F32)
        sw = sw + wmask
        pw = jnp.exp2(sw - jnp.max(sw, axis=0, keepdims=True))
        vv = jnp.concatenate([vwt_ref[0, h, wc + j] for j in range(WIN_KEYS // WIN_ALIGN)], axis=1)
        ow = jnp.dot(vv, pw.astype(BF16), preferred_element_type=F32)
        o_win.append(ow[:HD] * (1.0 / ow[HD:HD + 1]))

    few = cur < SLC_TOPN
    causal = blk <= cur
    blkf = blk.astype(F32)

    def pick(sc):
        mx = jnp.max(sc, axis=0, keepdims=True)
        first = jnp.min(jnp.where(sc == mx, blkf, float(SLC_LANES)), axis=0, keepdims=True)
        return jnp.where(blkf == first, -jnp.inf, sc)

    for _ in range(SLC_TOPN - 3):
        score = [pick(sc) for sc in score]
    q_aug = []
    for h in heads:
        chosen = forced | ((few | (score[h] == -jnp.inf)) & causal)
        bias_t = jnp.where(chosen, 0.0, NEG)
        bias = bias_t.T.astype(BF16)
        rows = jnp.concatenate([bias[a * QB:(a + 1) * QB] for a in range(NB) for g in range(G)], axis=0)
        q_aug.append(jnp.concatenate([rows, q[h], jnp.zeros((R, HD), BF16)], axis=1))

    def scores(c, slot):
        k0 = pl.multiple_of(c * SLC_CHUNK, SLC_CHUNK)
        for h in heads:
            s_ref[slot, h] = lax.dot_general(ks_ref[0, h, pl.ds(k0, SLC_CHUNK), :], q_aug[h], NT, preferred_element_type=F32)

    def absorb(c, slot, carry):
        out = []
        for h in heads:
            m_i, acc = carry[h]
            sc = s_ref[slot, h]
            m_new = jnp.maximum(m_i, jnp.max(sc, axis=0, keepdims=True))
            pp = jnp.exp2(sc - m_new).astype(BF16)
            acc = jnp.exp2(m_i - m_new) * acc + jnp.dot(vst_ref[0, h, c], pp, preferred_element_type=F32)
            out.append((m_new, acc))
        return tuple(out)

    diag = t0 // SLC_CHUNK
    n_pairs = diag // 2
    scores(0, 0)

    def pair(j, carry):
        scores(2 * j + 1, 1)
        carry = absorb(2 * j, 0, carry)
        scores(2 * j + 2, 0)
        return absorb(2 * j + 1, 1, carry)

    init = tuple((jnp.full((1, R), NEG, F32), jnp.zeros((VT_ROWS, R), F32)) for h in heads)
    carry = lax.fori_loop(0, n_pairs, pair, init)
    scores(2 * n_pairs + 1, 1)
    r0 = pl.multiple_of(t0 - diag * SLC_CHUNK, NB * QB)
    tri = tri_ref[...]
    for h in heads:
        s_ref[diag % 2, h, pl.ds(r0, NB * QB), :] += tri
    carry = absorb(2 * n_pairs, 0, carry)
    carry = absorb(2 * n_pairs + 1, 1, carry)

    sig = _sigmoid(gt_ref[0])
    glane = lax.broadcasted_iota(jnp.int32, (QB, 128), 1)
    outs = [[] for a in range(NB)]
    for h in heads:
        acc_s = carry[h][1]
        oc, os_, ow = o_cmp[h].T, (acc_s[:HD] * (1.0 / acc_s[HD:HD + 1])).T, o_win[h].T
        for a in range(NB):
            sig_a = sig[a * QB:(a + 1) * QB]
            for g in range(G):
                base = ((hk0 + h) * G + g) * N_NSA_BRANCH
                gc, gs, gw = [jnp.sum(jnp.where(glane == base + b, sig_a, 0.0), axis=-1, keepdims=True) for b in range(3)]
                r = slice((a * G + g) * QB, (a * G + g + 1) * QB)
                outs[a].append(gc * oc[r] + gs * os_[r] + gw * ow[r])
    o = jnp.concatenate([jnp.concatenate(oa, axis=1) for oa in outs], axis=0)
    o_ref[0] = (o * _silu(gate_ref[0].astype(F32))).astype(o_ref.dtype)


def _nsa_attn(qr, kc, vct, ks, vst, vwt, proj3, small3):
    B, S, _ = qr.shape
    Hk, HD, NH = NSA_KV_HEADS, NSA_HD, NSA_STEP_HEADS
    TQ = NSA_STEP_BLOCKS * SLC_BLOCK
    r = np.arange(TQ)[:, None]
    ln = np.arange(NSA_GROUP * TQ)[None, :]
    a, qi = ln // (NSA_GROUP * SLC_BLOCK), ln % SLC_BLOCK
    tri = jnp.asarray(np.where((r // SLC_BLOCK != a) | (r % SLC_BLOCK <= qi), 0.0, NEG), F32)
    nc = kc.shape[2]
    tl = a * SLC_BLOCK + qi
    u = np.arange(2 * nc)[:, None]
    cmask = jnp.asarray(np.where(u - nc <= (tl - (CMP_LEN - 1)) // CMP_STRIDE, 0.0, NEG), F32)
    u = np.arange(WINDOW + WIN_KEYS)[:, None]
    wmask = jnp.asarray(np.where((u > tl) & (u <= WINDOW + tl), 0.0, NEG), F32)
    const = lambda c: pl.BlockSpec(c.shape, lambda b, h, t: (0, 0), pipeline_mode=pl.Buffered(1))
    W = NH * NSA_GROUP * HD
    full = lambda a: pl.BlockSpec((1, NH) + a.shape[2:], lambda b, h, t: (b, h) + (0,) * (a.ndim - 2),
                                  pipeline_mode=pl.Buffered(1))
    return pl.pallas_call(
        _nsa_attn_kernel,
        grid=(B, Hk // NH, S // TQ),
        in_specs=[
            pl.BlockSpec((1, TQ, W), lambda b, h, t: (b, t, h)),
            full(kc), full(vct), full(ks), full(vst), full(vwt),
            pl.BlockSpec((1, TQ, 128), lambda b, h, t: (b, t, CS_NSG // 128)),
            pl.BlockSpec((1, TQ, W), lambda b, h, t: (b, t, C_NGATE // W + h)),
            const(tri), const(cmask), const(wmask),
        ],
        out_specs=pl.BlockSpec((1, TQ, W), lambda b, h, t: (b, t, h)),
        out_shape=jax.ShapeDtypeStruct((B, S, NSA_HEADS * HD), BF16),
        scratch_shapes=[pltpu.VMEM((2, NH, SLC_CHUNK, NSA_GROUP * TQ), F32), pltpu.VMEM((NH, nc, NSA_GROUP * TQ), BF16)],
        compiler_params=_cp(("parallel", "parallel", "arbitrary")),
        name="nsa_attn",
    )(qr, kc, vct, ks, vst, vwt, small3, proj3, tri, cmask, wmask)


def _head_rms(x, gain, hd):
    outs = []
    for h in range(x.shape[-1] // hd):
        xh = x[:, h * hd:(h + 1) * hd]
        ms = jnp.mean(xh * xh, axis=-1, keepdims=True)
        outs.append(xh * lax.rsqrt(ms + EPS) * gain)
    return jnp.concatenate(outs, axis=1)


def _mem_prep_kernel(mem_ref, g_ref, w_ref, kn_ref, mk_ref, mv_ref):
    x = mem_ref[0]
    ms = jnp.mean(x * x, axis=-1, keepdims=True)
    xn = (x * lax.rsqrt(ms + EPS) * g_ref[...]).astype(BF16)
    kv = jnp.dot(xn, w_ref[...], preferred_element_type=F32)
    W = MEM_HEADS * MEM_HD
    mk_ref[0] = _head_rms(kv[:, :W], kn_ref[...], MEM_HD).astype(mk_ref.dtype)
    mv_ref[0] = kv[:, W:].astype(mv_ref.dtype)


def _mem_prep(mem, gain, w_kv, kn):
    B, N, D = mem.shape
    W = MEM_HEADS * MEM_HD
    return pl.pallas_call(
        _mem_prep_kernel,
        grid=(B,),
        in_specs=[
            pl.BlockSpec((1, N, D), lambda b: (b, 0, 0)),
            pl.BlockSpec((1, D), lambda b: (0, 0)),
            pl.BlockSpec((D, 2 * W), lambda b: (0, 0)),
            pl.BlockSpec((1, MEM_HD), lambda b: (0, 0)),
        ],
        out_specs=[pl.BlockSpec((1, N, W), lambda b: (b, 0, 0))] * 2,
        out_shape=[jax.ShapeDtypeStruct((B, N, W), BF16)] * 2,
        compiler_params=_cp(("parallel",)),
        name="mem_prep",
    )(mem, gain, w_kv, kn)


def _final_kernel(x_ref, ya_ref, yb_ref, mq_ref, mg_ref, mr_ref, mk_ref, mv_ref, qn_ref, wb_ref, wo_ref, o_ref):
    mq = _head_rms(mq_ref[...].astype(F32), qn_ref[...], MEM_HD)
    mk = mk_ref[0]
    mv = mv_ref[0]
    heads = []
    for h in range(MEM_HEADS):
        sl = slice(h * MEM_HD, (h + 1) * MEM_HD)
        s = lax.dot_general(mq[:, sl].astype(BF16), mk[:, sl], NT, preferred_element_type=F32) * (MEM_HD ** -0.5)
        m = jnp.max(s, axis=-1, keepdims=True)
        p = jnp.exp(s - m)
        o = jnp.dot(p.astype(BF16), mv[:, sl], preferred_element_type=F32) / jnp.sum(p, axis=-1, keepdims=True)
        heads.append(o)
    ym = jnp.concatenate(heads, axis=1) * _silu(mg_ref[...].astype(F32))

    mixed = None
    for c, y in enumerate((ya_ref[...], yb_ref[...], ym.astype(BF16))):
        z = jnp.dot(y, wb_ref[c], preferred_element_type=F32)
        term = _sigmoid(mr_ref[:, c * D_MODEL:(c + 1) * D_MODEL].astype(F32)) * z
        mixed = term if mixed is None else mixed + term
    o_ref[...] = x_ref[...] + jnp.dot(mixed.astype(BF16), wo_ref[...], preferred_element_type=F32)


def _final(x2, ya2, yb2, proj2, mk, mv, qn, wb, wo, S, tr=512):
    M, D = x2.shape
    N = mk.shape[1]
    nb = S // tr
    row = lambda c: pl.BlockSpec((tr, D), lambda i: (i, c))
    return pl.pallas_call(
        _final_kernel,
        grid=(M // tr,),
        in_specs=[
            row(0), row(0), row(0),
            row(C_MQ // D), row(C_MG // D),
            pl.BlockSpec((tr, 3 * D), lambda i: (i, C_MERGE // (3 * D))),
            pl.BlockSpec((1, N, D), lambda i: (i // nb, 0, 0)),
            pl.BlockSpec((1, N, D), lambda i: (i // nb, 0, 0)),
            pl.BlockSpec((1, MEM_HD), lambda i: (0, 0)),
            pl.BlockSpec((3, D, D), lambda i: (0, 0, 0), pipeline_mode=pl.Buffered(1)),
            pl.BlockSpec((D, D), lambda i: (0, 0), pipeline_mode=pl.Buffered(1)),
        ],
        out_specs=pl.BlockSpec((tr, D), lambda i: (i, 0)),
        out_shape=jax.ShapeDtypeStruct((M, D), F32),
        compiler_params=_cp(("parallel",)),
        name="final",
    )(x2, ya2, yb2, proj2, proj2, proj2, mk, mv, qn, wb, wo)


def _regroup_kernel(w_ref, o_ref, s_ref, *, o, runs):
    dst = 0
    for a, b in runs:
        o_ref[dst:dst + o[b] - o[a], :] = w_ref[0, o[a]:o[b], :].astype(o_ref.dtype)
        dst += o[b] - o[a]
    o_ref[dst:, :] = jnp.zeros((o_ref.shape[0] - dst, o_ref.shape[1]), o_ref.dtype)
    for sec, at in ((3, CS_LR), (7, CS_NSG)):
        wd = o[sec + 1] - o[sec]
        s_ref[at:at + wd, :] = w_ref[0, o[sec]:o[sec + 1], :].astype(s_ref.dtype)
        s_ref[at + wd:at + 128, :] = jnp.zeros((128 - wd, s_ref.shape[1]), s_ref.dtype)


def _regroup(w_in_t, layer, tc=128):
    _, C, D = w_in_t.shape
    o = [int(v) for v in np.cumsum([0, 512, 512, 1024, 16, 1024, 1024, 1536, 48, 1024, 1024, 1024, 3072])]
    assert o[-1] == C
    runs = ((2, 3), (4, 6), (8, 12), (6, 7), (0, 2))
    return pl.pallas_call(
        functools.partial(_regroup_kernel, o=o, runs=runs),
        grid=(D // tc,),
        in_specs=[pl.BlockSpec((1, C, tc), lambda i: (layer, 0, i))],
        out_specs=[pl.BlockSpec((NP, tc), lambda i: (0, i)), pl.BlockSpec((NP_SMALL, tc), lambda i: (0, i))],
        out_shape=[jax.ShapeDtypeStruct((NP, D), BF16), jax.ShapeDtypeStruct((NP_SMALL, D), BF16)],
        compiler_params=_cp(("parallel",)),
        name="w_regroup",
    )(w_in_t)


def _overlap_matrix_t(S):
    n_cmp = (S - CMP_LEN) // CMP_STRIDE + 1
    n_slc = S // SLC_BLOCK
    cs = np.arange(n_cmp)[:, None] * CMP_STRIDE
    ss = np.arange(n_slc)[None, :] * SLC_BLOCK
    ov = np.clip(np.minimum(cs + CMP_LEN, ss + SLC_BLOCK) - np.maximum(cs, ss), 0, None) / CMP_LEN
    out = np.zeros((SLC_LANES, S // CMP_STRIDE), np.float32)
    out[:n_slc, :n_cmp] = ov.T
    return jnp.asarray(out, BF16)


def _layer(x, mem, positions, norm_gain, mem_norm_gain, w_in, w_gla_alpha, b_gla_alpha, gla_out_norm,
           nsa_q_norm, nsa_k_norm, pe_cmp_k, pe_cmp_v, w_cmp_k1, w_cmp_k2, w_cmp_v1, w_cmp_v2,
           w_mem_kv, mem_q_norm, mem_k_norm, w_branch, w_out):
    B, S, D = x.shape
    assert D == D_MODEL and S % (2 * SLC_CHUNK) == 0 and S >= WIN_KEYS and S // SLC_BLOCK <= SLC_LANES
    assert SLC_TOPN % NSA_STEP_BLOCKS == 0 and SLC_CHUNK % (NSA_STEP_BLOCKS * SLC_BLOCK) == 0

    w_all, w_small = w_in

    x2 = x.reshape(B * S, D)
    proj2, small2 = _proj(x2, norm_gain.reshape(1, D), w_all, w_small)
    proj3 = proj2.reshape(B, S, NP)
    small3 = small2.reshape(B, S, NP_SMALL)

    wa = jnp.pad(w_gla_alpha, ((0, 128 - GLA_RANK), (0, 0)))
    ya = _gla(proj3, small3, wa, b_gla_alpha.reshape(1, -1), gla_out_norm.reshape(1, -1))

    half = NSA_HD // 2
    inv = ROPE_THETA ** (-jnp.arange(half, dtype=F32) / half)
    inv = jnp.tile(inv, 128 // half).reshape(1, 128)
    qn = jnp.tile(nsa_q_norm, NSA_KV_HEADS).reshape(1, -1)
    kn = jnp.tile(nsa_k_norm, (1, NSA_KV_HEADS))
    gid = np.arange(256) // NSA_HD
    gmat = jnp.asarray(gid[:, None] == gid[None, :], BF16)
    qr, kc_tok, vc_tok, ks, vst, vwt = _nsa_prep(proj3, positions.reshape(B, S, 1), inv, qn, kn, gmat)

    def cmp_args(pe, w1, w2):
        pe2 = pe.reshape(2, CMP_STRIDE * NSA_HD)
        pe16 = jnp.concatenate([jnp.broadcast_to(pe2[0:1], (8, pe2.shape[1])), jnp.broadcast_to(pe2[1:2], (8, pe2.shape[1]))], 0)
        w1f = w1.reshape(CMP_LEN * NSA_HD, CMP_HIDDEN).astype(BF16)
        return pe16, w1f[:CMP_STRIDE * NSA_HD], w1f[CMP_STRIDE * NSA_HD:], w2.astype(BF16)

    kc = _compress(kc_tok, *cmp_args(pe_cmp_k, w_cmp_k1, w_cmp_k2))
    tail = jnp.concatenate([jnp.ones((VT_ROWS - NSA_HD, S // CMP_STRIDE), BF16), _overlap_matrix_t(S)], axis=0)
    vct = _compress(vc_tok, *cmp_args(pe_cmp_v, w_cmp_v1, w_cmp_v2), tail=tail)
    yb = _nsa_attn(qr, kc, vct, ks, vst, vwt, proj3, small3)

    mk, mv = _mem_prep(mem, mem_norm_gain.reshape(1, D), w_mem_kv.astype(BF16), mem_k_norm.reshape(1, -1))
    out = _final(x2, ya.reshape(B * S, D), yb.reshape(B * S, D), proj2, mk, mv, mem_q_norm.reshape(1, -1),
                 w_branch.astype(BF16), w_out.astype(BF16), S)
    return out.reshape(B, S, D)


def kernel(x, mem, positions, norm_gain, mem_norm_gain, w_in, w_gla_alpha, b_gla_alpha, gla_out_norm, nsa_q_norm, nsa_k_norm, pe_cmp_k, pe_cmp_v, w_cmp_k1, w_cmp_k2, w_cmp_v1, w_cmp_v2, w_mem_kv, mem_q_norm, mem_k_norm, w_branch, w_out):
    h = x
    for l in range(norm_gain.shape[0]):
        h = _layer(h, mem, positions, norm_gain[l], mem_norm_gain[l], _regroup(jnp.swapaxes(w_in, 1, 2), l),
                   w_gla_alpha[l], b_gla_alpha[l],
                   gla_out_norm[l], nsa_q_norm[l], nsa_k_norm[l], pe_cmp_k[l], pe_cmp_v[l], w_cmp_k1[l],
                   w_cmp_k2[l], w_cmp_v1[l], w_cmp_v2[l], w_mem_kv[l], mem_q_norm[l], mem_k_norm[l],
                   w_branch[l], w_out[l])
    return h
```
